```python
import math
import jax, jax.numpy as jnp
from jax import lax
import numpy as np


D_MODEL = 1024
BATCH = 8
SEQ = 2048
DEPTH = 2

HGRN_HEADS = 4
HGRN_HEAD_DIM = 128
HGRN_WIDTH = HGRN_HEADS * HGRN_HEAD_DIM
HGRN_CHUNK = 64
MLA_HEADS = 8
MLA_NOPE_DIM = 64
MLA_ROPE_DIM = 32
MLA_QK_DIM = MLA_NOPE_DIM + MLA_ROPE_DIM
MLA_V_DIM = 64
MLA_Q_RANK = 384
MLA_KV_RANK = 256
MLA_WIDTH = MLA_HEADS * MLA_V_DIM
ROPE_THETA = 10000.0
Q_BLOCK = 128
POOL_WINDOWS = (2, 4, 8, 16)
POOL_GROUPS = 4
POOL_GROUP_DIM = 128
POOL_WIDTH = POOL_GROUPS * POOL_GROUP_DIM
N_EXPERTS = 32
TOP_K = 4
EXPERT_FF = D_MODEL
SWIGLU_LIMIT = 7.0
SWIGLU_ALPHA = 1.702
NORM_EPS = 1e-6
IN_WIDTHS = (HGRN_WIDTH, HGRN_WIDTH, HGRN_WIDTH, HGRN_WIDTH,
             MLA_Q_RANK, MLA_KV_RANK, MLA_ROPE_DIM, POOL_WIDTH,
             D_MODEL, D_MODEL, D_MODEL)
IN_TOTAL = sum(IN_WIDTHS)

kernel_name = 'hybrid_hgrn2_mla_pool_moe_adaln'


def _rmsnorm(x, g):
    xf = x.astype(jnp.float32)
    y = xf * lax.rsqrt(jnp.mean(xf * xf, axis=-1, keepdims=True) + NORM_EPS)
    return (y * g.astype(jnp.float32)).astype(x.dtype)


def _modulate(xn, shift, scale):
    return xn * (1.0 + scale[:, None, :]) + shift[:, None, :]


def _split_cols(t):
    idx, acc = [], 0
    for w in IN_WIDTHS[:-1]:
        acc += w
        idx.append(acc)
    return jnp.split(t, idx, axis=-1)


def _hgrn2(q, f_logit, i, g, lb, onorm_g):
    B, S, _ = q.shape
    H, Dk, C = HGRN_HEADS, HGRN_HEAD_DIM, HGRN_CHUNK
    f32 = jnp.float32
    z = f_logit.astype(f32).reshape(B, S, H, Dk)
    lbh = lb.astype(f32).reshape(H, Dk)
    log_f = jnp.logaddexp(jnp.log1p(-lbh) + jax.nn.log_sigmoid(z), jnp.log(lbh))
    k = (1.0 - lbh) * jax.nn.sigmoid(-z)
    qf = q.astype(f32).reshape(B, S, H, Dk)
    vf = i.astype(f32).reshape(B, S, H, Dk)
    n = S // C

    def to_chunks(t):
        return t.reshape(B, n, C, H, Dk).transpose(1, 0, 3, 2, 4)

    causal = jnp.tril(jnp.ones((C, C), dtype=bool))

    def step(state, inp):
        qc, kc, vc, lfc = inp
        b = jnp.cumsum(lfc, axis=2)
        diff = b[:, :, :, None, :] - b[:, :, None, :, :]
        decay = jnp.exp(jnp.where(causal[:, :, None], diff, -jnp.inf))
        att = jnp.einsum('bhtd,bhsd,bhtsd->bhts', qc, kc, decay)
        o = jnp.einsum('bhts,bhsv->bhtv', att, vc) + \
            jnp.einsum('bhtd,bhdv->bhtv', qc * jnp.exp(b), state)
        b_last = b[:, :, -1:, :]
        state = jnp.exp(b_last[:, :, 0, :])[..., None] * state + \
            jnp.einsum('bhsd,bhsv->bhdv', kc * jnp.exp(b_last - b), vc)
        return state, o

    state0 = jnp.zeros((B, H, Dk, Dk), f32)
    _, o = lax.scan(step, state0, (to_chunks(qf), to_chunks(k), to_chunks(vf), to_chunks(log_f)))
    o = o.transpose(1, 0, 3, 2, 4).reshape(B, S, H, Dk)
    o = _rmsnorm(o, onorm_g) * jax.nn.silu(g.astype(f32).reshape(B, S, H, Dk))
    return o.reshape(B, S, HGRN_WIDTH).astype(q.dtype)


def _rope(x, cos, sin):
    half = x.shape[-1] // 2
    x1, x2 = x[..., :half], x[..., half:]
    return jnp.concatenate([x1 * cos - x2 * sin, x2 * cos + x1 * sin], axis=-1)


def _causal_attention(q, k, v):
    B, S, H, Dq = q.shape
    Dv = v.shape[-1]
    nb = S // Q_BLOCK
    scale = 1.0 / math.sqrt(Dq)
    qb = q.reshape(B, nb, Q_BLOCK, H, Dq).transpose(1, 0, 3, 2, 4)
    kt = k.transpose(0, 2, 1, 3)
    vt = v.transpose(0, 2, 1, 3)
    key_idx = jnp.arange(S)

    def one_block(args):
        qblk, blk = args
        s = jnp.einsum('bhqd,bhkd->bhqk', qblk, kt).astype(jnp.float32) * scale
        q_idx = blk * Q_BLOCK + jnp.arange(Q_BLOCK)
        s = jnp.where(key_idx[None, :] <= q_idx[:, None], s, -jnp.inf)
        p = jax.nn.softmax(s, axis=-1)
        return jnp.einsum('bhqk,bhkd->bhqd', p.astype(v.dtype), vt)

    out = lax.map(one_block, (qb, jnp.arange(nb)))
    return out.transpose(1, 0, 3, 2, 4).reshape(B, S, H, Dv)


def _mla(q_lat, kv_lat, k_rope, positions, qlat_g, kvlat_g, w_uq, w_ukv, qn_g, kn_g):
    B, S, _ = q_lat.shape
    H = MLA_HEADS
    q = (_rmsnorm(q_lat, qlat_g) @ w_uq).reshape(B, S, H, MLA_QK_DIM)
    kv = (_rmsnorm(kv_lat, kvlat_g) @ w_ukv).reshape(B, S, H, MLA_NOPE_DIM + MLA_V_DIM)
    k_nope, v = kv[..., :MLA_NOPE_DIM], kv[..., MLA_NOPE_DIM:]
    k = jnp.concatenate([k_nope, jnp.broadcast_to(k_rope[:, :, None, :], (B, S, H, MLA_ROPE_DIM))], axis=-1)
    q = _rmsnorm(q, qn_g)
    k = _rmsnorm(k, kn_g)
    inv_freq = 1.0 / (ROPE_THETA ** (jnp.arange(0, MLA_ROPE_DIM, 2, dtype=jnp.float32) / MLA_ROPE_DIM))
    ang = positions.astype(jnp.float32)[..., None] * inv_freq
    cos = jnp.cos(ang)[:, :, None, :].astype(q.dtype)
    sin = jnp.sin(ang)[:, :, None, :].astype(q.dtype)
    q = jnp.concatenate([q[..., :MLA_NOPE_DIM], _rope(q[..., MLA_NOPE_DIM:], cos, sin)], axis=-1)
    k = jnp.concatenate([k[..., :MLA_NOPE_DIM], _rope(k[..., MLA_NOPE_DIM:], cos, sin)], axis=-1)
    return _causal_attention(q, k, v).reshape(B, S, MLA_WIDTH)


def _pool_mixer(u, w_pool, scale):
    B, S, _ = u.shape
    uf = u.astype(jnp.float32).reshape(B, S, POOL_GROUPS, POOL_GROUP_DIM)
    cs = jnp.cumsum(uf, axis=1)
    t = jnp.arange(1, S + 1, dtype=jnp.float32)
    outs = []
    for gi, w in enumerate(POOL_WINDOWS):
        csg = cs[:, :, gi, :]
        lo = jnp.pad(csg, ((0, 0), (w, 0), (0, 0)))[:, :S, :]
        cnt = jnp.minimum(t, float(w))[None, :, None]
        outs.append((csg - lo) / cnt)
    pooled = jnp.stack(outs, axis=2)
    mixed = jnp.einsum('bsgc,gcd->bsgd', pooled - uf, w_pool.astype(jnp.float32))
    return (mixed.reshape(B, S, POOL_WIDTH) * scale.astype(jnp.float32)).astype(u.dtype)


def _moe(h, w_r, b_r, w1, b1, w2, b2):
    B, S, D = h.shape
    hf = h.reshape(B * S, D)
    logits = (hf @ w_r + b_r).astype(jnp.float32)
    vals, idx = lax.top_k(logits, TOP_K)
    wts = jax.nn.softmax(vals, axis=-1)
    combine = jnp.einsum('nk,nke->ne', wts, jax.nn.one_hot(idx, N_EXPERTS, dtype=jnp.float32)).astype(h.dtype)
    y = jnp.zeros_like(hf)
    for e in range(N_EXPERTS):
        a = hf @ w1[e] + b1[e]
        glu = jnp.minimum(a[:, :EXPERT_FF], SWIGLU_LIMIT)
        lin = jnp.clip(a[:, EXPERT_FF:], -SWIGLU_LIMIT, SWIGLU_LIMIT)
        act = glu * jax.nn.sigmoid(SWIGLU_ALPHA * glu) * (lin + 1.0)
        y = y + combine[:, e:e + 1] * (act @ w2[e] + b2[e])
    return y.reshape(B, S, D)


def setup_inputs(seed: int = 0) -> dict:
    key = jax.random.key(seed)
    ks = jax.random.split(key, 32)
    f32 = jnp.float32
    L, D, E, F = DEPTH, D_MODEL, N_EXPERTS, EXPERT_FF

    def nrm(k, shape, fan_in):
        return jax.random.normal(k, shape, f32) * fan_in ** -0.5

    def gain(k, shape):
        return 1.0 + 0.05 * jax.random.normal(k, shape, f32)

    def small(k, shape, s):
        return s * jax.random.normal(k, shape, f32)

    return {
        'x': jax.random.normal(ks[0], (BATCH, SEQ, D), f32),
        'c': jax.random.normal(ks[1], (BATCH, D), f32),
        'positions': jnp.arange(SEQ, dtype=jnp.int32)[None, :] + jax.random.randint(ks[2], (BATCH, 1), 0, 4096, dtype=jnp.int32),
        'ada_w': 0.5 * nrm(ks[3], (L, D, 6 * D), D),
        'ada_b': small(ks[4], (L, 6 * D), 0.02),
        'norm1_g': gain(ks[5], (L, D)),
        'norm2_g': gain(ks[6], (L, D)),
        'w_in': nrm(ks[7], (L, D, IN_TOTAL), D),
        'hgrn_lb': jax.random.normal(ks[8], (L, HGRN_WIDTH), f32),
        'hgrn_onorm_g': gain(ks[9], (L, HGRN_HEAD_DIM)),
        'mla_qlat_g': gain(ks[10], (L, MLA_Q_RANK)),
        'mla_kvlat_g': gain(ks[11], (L, MLA_KV_RANK)),
        'w_uq': nrm(ks[12], (L, MLA_Q_RANK, MLA_HEADS * MLA_QK_DIM), MLA_Q_RANK),
        'w_ukv': nrm(ks[13], (L, MLA_KV_RANK, MLA_HEADS * (MLA_NOPE_DIM + MLA_V_DIM)), MLA_KV_RANK),
        'q_norm_g': gain(ks[14], (L, MLA_QK_DIM)),
        'k_norm_g': gain(ks[15], (L, MLA_QK_DIM)),
        'w_pool': nrm(ks[16], (L, POOL_GROUPS, POOL_GROUP_DIM, POOL_GROUP_DIM), POOL_GROUP_DIM),
        'pool_scale': gain(ks[17], (L, POOL_WIDTH)),
        'w_br_a': nrm(ks[18], (L, HGRN_WIDTH, D), HGRN_WIDTH),
        'w_br_b': nrm(ks[19], (L, MLA_WIDTH, D), MLA_WIDTH),
        'w_br_c': nrm(ks[20], (L, POOL_WIDTH, D), POOL_WIDTH),
        'w_out': nrm(ks[21], (L, D, D), D),
        'w_router': nrm(ks[22], (L, D, E), D),
        'b_router': small(ks[23], (L, E), 0.01),
        'w_exp1': nrm(ks[24], (L, E, D, 2 * F), D),
        'b_exp1': small(ks[25], (L, E, 2 * F), 0.02),
        'w_exp2': nrm(ks[26], (L, E, F, D), F),
        'b_exp2': small(ks[27], (L, E, D), 0.02),
    }


def reference(x, c, positions, ada_w, ada_b, norm1_g, norm2_g, w_in, hgrn_lb, hgrn_onorm_g,
              mla_qlat_g, mla_kvlat_g, w_uq, w_ukv, q_norm_g, k_norm_g, w_pool, pool_scale,
              w_br_a, w_br_b, w_br_c, w_out, w_router, b_router, w_exp1, b_exp1, w_exp2, b_exp2):
    lb_all = jnp.cumsum(jax.nn.softmax(hgrn_lb.astype(jnp.float32), axis=0), axis=0)
    lb_all = lb_all - lb_all[0:1]
    cond = jax.nn.silu(c)
    for l in range(DEPTH):
        mod = cond @ ada_w[l] + ada_b[l]
        sh1, sc1, g1, sh2, sc2, g2 = jnp.split(mod, 6, axis=-1)
        h = _modulate(_rmsnorm(x, norm1_g[l]), sh1, sc1)
        (hq, hf, hi, hg, q_lat, kv_lat, k_rope, pool_in,
         gate_a, gate_b, gate_c) = _split_cols(h @ w_in[l])
        y_a = _hgrn2(hq, hf, hi, hg, lb_all[l], hgrn_onorm_g[l]) @ w_br_a[l]
        y_b = _mla(q_lat, kv_lat, k_rope, positions, mla_qlat_g[l], mla_kvlat_g[l],
                   w_uq[l], w_ukv[l], q_norm_g[l], k_norm_g[l]) @ w_br_b[l]
        y_c = _pool_mixer(pool_in, w_pool[l], pool_scale[l]) @ w_br_c[l]
        merged = jax.nn.sigmoid(gate_a) * y_a + jax.nn.sigmoid(gate_b) * y_b + jax.nn.sigmoid(gate_c) * y_c
        x = x + g1[:, None, :] * (merged @ w_out[l])
        h2 = _modulate(_rmsnorm(x, norm2_g[l]), sh2, sc2)
        x = x + g2[:, None, :] * _moe(h2, w_router[l], b_router[l], w_exp1[l], b_exp1[l], w_exp2[l], b_exp2[l])
    return x
```

```python
import functools
import math

import jax
import jax.numpy as jnp
from jax import lax
from jax.experimental import pallas as pl
from jax.experimental.pallas import tpu as pltpu

F32 = jnp.float32
BF16 = jnp.bfloat16

LANES = 128
NORM_EPS = 1e-6
HGRN_HEADS = 4
HGRN_DIM = 128
HGRN_CHUNK = 64
HGRN_SUB = 16
MLA_HEADS = 8
MLA_NOPE = 64
MLA_ROPE = 32
MLA_QK = MLA_NOPE + MLA_ROPE
MLA_V = 64
MLA_Q_RANK = 384
MLA_KV_RANK = 256
ROPE_THETA = 10000.0
POOL_WINDOWS = (2, 4, 8, 16)
N_EXPERTS = 32
TOP_K = 4
SWIGLU_LIMIT = 7.0
SWIGLU_ALPHA = 1.702
NEG_BIG = -1e30

VMEM_LIMIT = 56 * 1024 * 1024


def _cparams(sem):
    return pltpu.CompilerParams(dimension_semantics=sem, vmem_limit_bytes=VMEM_LIMIT)


def _dot(a, b):
    return jnp.dot(a, b, preferred_element_type=F32)


def _dot_nt(a, b):
    return lax.dot_general(a, b, (((1,), (1,)), ((), ())), preferred_element_type=F32)


def _dot_tn(a, b):
    return lax.dot_general(a, b, (((0,), (0,)), ((), ())), preferred_element_type=F32)


def _split3(x):
    hi = x.astype(BF16)
    r = x - hi.astype(F32)
    mid = r.astype(BF16)
    lo = (r - mid.astype(F32)).astype(BF16)
    return hi, mid, lo


def _sigmoid(x):
    return 1.0 / (1.0 + jnp.exp(-x))


def _ada_kernel(c_ref, w_ref, b_ref, o_ref):
    c = c_ref[...]
    cond = (c * _sigmoid(c)).astype(BF16)
    o_ref[0] = _dot(cond, w_ref[0].astype(BF16)) + b_ref[0]


def _ada(c, ada_w, ada_b):
    L, D, W = ada_w.shape
    B = c.shape[0]
    tn = D
    return pl.pallas_call(
        _ada_kernel,
        grid=(L, W // tn),
        in_specs=[
            pl.BlockSpec((B, D), lambda l, j: (0, 0)),
            pl.BlockSpec((1, D, tn), lambda l, j: (l, 0, j)),
            pl.BlockSpec((1, 1, tn), lambda l, j: (l, 0, j)),
        ],
        out_specs=pl.BlockSpec((1, B, tn), lambda l, j: (l, 0, j)),
        out_shape=jax.ShapeDtypeStruct((L, B, W), F32),
        compiler_params=_cparams(("parallel", "parallel")),
        name="ada_mod",
    )(c, ada_w, ada_b.reshape(L, 1, W))


def _inproj_kernel(x_ref, g_ref, sh_ref, sc_ref, wg_ref, wh_ref, wm_ref, wp_ref,
                   og_ref, oh_ref, om_ref, op_ref):
    x = x_ref[...]
    ms = jnp.mean(x * x, axis=-1, keepdims=True)
    h = x * lax.rsqrt(ms + NORM_EPS) * g_ref[...]
    h = h * (1.0 + sc_ref[0]) + sh_ref[0]
    hb = h.astype(BF16)
    cw = 512
    for w_ref, o_ref in ((wg_ref, og_ref), (wh_ref, oh_ref), (wm_ref, om_ref), (wp_ref, op_ref)):
        width = w_ref.shape[1]
        step = cw if width % cw == 0 else width
        for j in range(0, width, step):
            o_ref[:, j:j + step] = _dot(hb, w_ref[:, j:j + step]).astype(o_ref.dtype)


def _inproj(x2, g, mod3, l, wg, wh, wm, wp, B, S, tr):
    N, D = x2.shape
    nb = S // tr
    base = l * B * 6

    def modspec(j):
        return pl.BlockSpec((1, 1, D), lambda i: (base + (i // nb) * 6 + j, 0, 0))

    def wspec(w):
        return pl.BlockSpec(w.shape, lambda i: (0, 0), pipeline_mode=pl.Buffered(1))

    outs = (
        jax.ShapeDtypeStruct((N, wg.shape[1]), BF16),
        jax.ShapeDtypeStruct((N, wh.shape[1]), F32),
        jax.ShapeDtypeStruct((N, wm.shape[1]), F32),
        jax.ShapeDtypeStruct((N, wp.shape[1]), F32),
    )
    return pl.pallas_call(
        _inproj_kernel,
        grid=(N // tr,),
        in_specs=[
            pl.BlockSpec((tr, D), lambda i: (i, 0)),
            pl.BlockSpec((1, D), lambda i: (0, 0)),
            modspec(0), modspec(1),
            wspec(wg), wspec(wh), wspec(wm), wspec(wp),
        ],
        out_specs=tuple(pl.BlockSpec((tr, o.shape[1]), lambda i: (i, 0)) for o in outs),
        out_shape=outs,
        compiler_params=_cparams(("parallel",)),
        name="in_proj",
    )(x2, g, mod3, mod3, wg, wh, wm, wp)


def _hgrn_kernel(layer, n_chunks, q_ref, f_ref, i_ref, g_ref, lb_ref, on_ref, o_ref, st_ref):
    C, R = HGRN_CHUNK, HGRN_SUB

    @pl.when(pl.program_id(2) == 0)
    def _():
        st_ref[...] = jnp.zeros_like(st_ref)

    lbr = lb_ref[...]
    e = jnp.exp(lbr - jnp.max(lbr, axis=0, keepdims=True))
    p = e / jnp.sum(e, axis=0, keepdims=True)
    lb = jnp.zeros((1, HGRN_DIM), F32)
    for j in range(1, layer + 1):
        lb = lb + p[j:j + 1]
    log_lb = jnp.log(lb)
    log_1mlb = jnp.log1p(-lb)
    one_m_lb = 1.0 - lb
    onorm = on_ref[...]

    row_c = lax.broadcasted_iota(jnp.int32, (C, C), 0)
    col_c = lax.broadcasted_iota(jnp.int32, (C, C), 1)
    tri = (col_c <= row_c).astype(BF16)
    row_r = lax.broadcasted_iota(jnp.int32, (R, 1), 0)

    def chunk(ci, carry):
        sl = pl.ds(pl.multiple_of(ci * C, C), C)
        q = q_ref[0, sl, :]
        z = f_ref[0, sl, :]
        v = i_ref[0, sl, :]
        g = g_ref[0, sl, :]
        ez = jnp.exp(-jnp.abs(z))
        log_sig = jnp.minimum(z, 0.0) - jnp.log1p(ez)
        a = log_1mlb + log_sig
        m = jnp.maximum(a, log_lb)
        lf = m + jnp.log(jnp.exp(a - m) + jnp.exp(log_lb - m))
        kk = one_m_lb * jnp.where(z >= 0.0, ez, 1.0) / (1.0 + ez)
        l1, l2, l3 = _split3(lf)
        b = _dot(tri, l1) + _dot(tri, l2) + _dot(tri, l3)
        st = st_ref[...]
        qs = (q * jnp.exp(b)).astype(BF16)
        o = _dot_nt(qs, st.astype(BF16))
        vb = v.astype(BF16)
        outs = []
        for i in range(C // R):
            r0 = i * R
            bi = b[r0:r0 + R]
            qi = q[r0:r0 + R]
            ki = kk[r0:r0 + R]
            vi = v[r0:r0 + R]
            oi = o[r0:r0 + R]
            if i > 0:
                ref = b[r0 - 1:r0]
                qt = (qi * jnp.exp(bi - ref)).astype(BF16)
                kt = (kk[:r0] * jnp.exp(ref - b[:r0])).astype(BF16)
                att = _dot_nt(qt, kt)
                oi = oi + _dot(att.astype(BF16), vb[:r0])
            for s in range(R):
                es = jnp.exp(jnp.minimum(bi - bi[s:s + 1], 0.0))
                col = jnp.sum(qi * es * ki[s:s + 1], axis=-1, keepdims=True)
                col = jnp.where(row_r >= s, col, 0.0)
                oi = oi + col * vi[s:s + 1]
            outs.append(oi)
        o = jnp.concatenate(outs, axis=0)
        bl = b[C - 1:C]
        kd = (kk * jnp.exp(bl - b)).astype(BF16)
        st_ref[...] = st * jnp.exp(bl) + _dot_tn(vb, kd)
        ms = jnp.mean(o * o, axis=-1, keepdims=True)
        y = o * lax.rsqrt(ms + NORM_EPS) * onorm
        y = y * (g * _sigmoid(g))
        o_ref[0, sl, :] = y.astype(o_ref.dtype)
        return carry

    lax.fori_loop(0, n_chunks, chunk, 0)


def _hgrn(ph, hgrn_lb, onorm_g, layer, B, S, tc):
    L = hgrn_lb.shape[0]
    H = HGRN_HEADS

    def spec(off):
        return pl.BlockSpec((1, tc, HGRN_DIM), lambda b, h, t: (b, t, off * H + h))

    return pl.pallas_call(
        functools.partial(_hgrn_kernel, layer, tc // HGRN_CHUNK),
        grid=(B, H, S // tc),
        in_specs=[spec(0), spec(1), spec(2), spec(3),
                  pl.BlockSpec((L, HGRN_DIM), lambda b, h, t: (0, h)),
                  pl.BlockSpec((1, HGRN_DIM), lambda b, h, t: (0, 0))],
        out_specs=pl.BlockSpec((1, tc, HGRN_DIM), lambda b, h, t: (b, t, h)),
        out_shape=jax.ShapeDtypeStruct((B, S, H * HGRN_DIM), BF16),
        scratch_shapes=[pltpu.VMEM((HGRN_DIM, HGRN_DIM), F32)],
        compiler_params=_cparams(("parallel", "parallel", "arbitrary")),
        name="hgrn2",
    )(ph, ph, ph, ph, hgrn_lb, onorm_g)


def _mla_prep_kernel(pm_ref, pos_ref, fr_ref, qg_ref, kvg_ref, wq_ref, wkv_ref, qn_ref, kn_ref,
                     q_ref, k_ref, v_ref):
    pm = pm_ref[...]
    ql = pm[:, :MLA_Q_RANK]
    kvl = pm[:, MLA_Q_RANK:MLA_Q_RANK + MLA_KV_RANK]
    kr = pm[:, MLA_Q_RANK + MLA_KV_RANK:]

    def rms(x, g):
        return x * lax.rsqrt(jnp.mean(x * x, axis=-1, keepdims=True) + NORM_EPS) * g

    qa = _dot(rms(ql, qg_ref[...]).astype(BF16), wq_ref[...])
    kva = _dot(rms(kvl, kvg_ref[...]).astype(BF16), wkv_ref[...])
    v_ref[...] = kva[:, MLA_HEADS * LANES:].astype(v_ref.dtype)

    ang = pos_ref[...] * fr_ref[...]
    cs = jnp.cos(ang)
    sn = jnp.sin(ang)
    lane = lax.broadcasted_iota(jnp.int32, ang.shape, 1)
    s1 = jnp.where(lane < MLA_NOPE + MLA_ROPE // 2, -sn, 0.0)
    s2 = jnp.where(lane >= MLA_NOPE + MLA_ROPE // 2, sn, 0.0)
    qn = qn_ref[...]
    kn = kn_ref[...]
    scale = 1.0 / math.sqrt(MLA_QK)
    half = MLA_ROPE // 2

    def head_norm_rope(xh, g):
        ss = jnp.sum(xh * xh, axis=-1, keepdims=True) * (1.0 / MLA_QK)
        xh = xh * lax.rsqrt(ss + NORM_EPS) * g
        return xh * cs + pltpu.roll(xh, LANES - half, 1) * s1 + pltpu.roll(xh, half, 1) * s2

    for h in range(MLA_HEADS):
        qh = qa[:, h * LANES:(h + 1) * LANES]
        q_ref[0, h] = (head_norm_rope(qh, qn) * scale).astype(q_ref.dtype)
        kh = kva[:, h * LANES:(h + 1) * LANES] + kr
        k_ref[0, h] = head_norm_rope(kh, kn).astype(k_ref.dtype)


def _mla_prep(pm, posf, freq_row, qlat_g, kvlat_g, wq, wkv, qn, kn, B, S, tr):
    N = pm.shape[0]
    nb = S // tr
    H = MLA_HEADS
    full = lambda a: pl.BlockSpec(a.shape, lambda i: (0,) * a.ndim)
    return pl.pallas_call(
        _mla_prep_kernel,
        grid=(N // tr,),
        in_specs=[pl.BlockSpec((tr, pm.shape[1]), lambda i: (i, 0)),
                  pl.BlockSpec((tr, 1), lambda i: (i, 0)),
                  full(freq_row), full(qlat_g), full(kvlat_g), full(wq), full(wkv), full(qn), full(kn)],
        out_specs=(pl.BlockSpec((1, H, tr, LANES), lambda i: (i // nb, 0, i % nb, 0)),
                   pl.BlockSpec((1, H, tr, LANES), lambda i: (i // nb, 0, i % nb, 0)),
                   pl.BlockSpec((tr, H * MLA_V), lambda i: (i, 0))),
        out_shape=(jax.ShapeDtypeStruct((B, H, S, LANES), BF16),
                   jax.ShapeDtypeStruct((B, H, S, LANES), BF16),
                   jax.ShapeDtypeStruct((N, H * MLA_V), BF16)),
        compiler_params=_cparams(("parallel",)),
        name="mla_prep",
    )(pm, posf, freq_row, qlat_g, kvlat_g, wq, wkv, qn, kn)


def _attn_kernel(q_ref, k_ref, v_ref, o_ref, m_ref, l_ref, acc_ref):
    qi = pl.program_id(2)
    ki = pl.program_id(3)
    T = q_ref.shape[2]

    @pl.when(ki == 0)
    def _():
        m_ref[...] = jnp.full_like(m_ref, NEG_BIG)
        l_ref[...] = jnp.zeros_like(l_ref)
        acc_ref[...] = jnp.zeros_like(acc_ref)

    @pl.when(ki <= qi)
    def _():
        vblk = v_ref[0]
        lane = lax.broadcasted_iota(jnp.int32, vblk.shape, 1)
        row = lax.broadcasted_iota(jnp.int32, (T, T), 0)
        col = lax.broadcasted_iota(jnp.int32, (T, T), 1)
        keep = (col + ki * T) <= (row + qi * T)
        for h in range(2):
            s = _dot_nt(q_ref[0, h], k_ref[0, h])
            s = jnp.where(keep, s, NEG_BIG)
            m_old = m_ref[h]
            m_new = jnp.maximum(m_old, jnp.max(s, axis=-1, keepdims=True))
            alpha = jnp.exp(m_old - m_new)
            p = jnp.exp(s - m_new)
            l_ref[h] = alpha * l_ref[h] + jnp.sum(p, axis=-1, keepdims=True)
            vh = jnp.where((lane < MLA_V) if h == 0 else (lane >= MLA_V), vblk, jnp.zeros_like(vblk))
            acc_ref[h] = alpha * acc_ref[h] + _dot(p.astype(BF16), vh)
            m_ref[h] = m_new

    @pl.when(ki == qi)
    def _():
        o_ref[0] = (acc_ref[0] / l_ref[0] + acc_ref[1] / l_ref[1]).astype(o_ref.dtype)


def _attention(q, k, v3, B, S, T):
    H = MLA_HEADS
    n = S // T
    return pl.pallas_call(
        _attn_kernel,
        grid=(B, H // 2, n, n),
        in_specs=[pl.BlockSpec((1, 2, T, LANES), lambda b, hp, qi, ki: (b, hp, qi, 0)),
                  pl.BlockSpec((1, 2, T, LANES), lambda b, hp, qi, ki: (b, hp, jnp.minimum(ki, qi), 0)),
                  pl.BlockSpec((1, T, LANES), lambda b, hp, qi, ki: (b, jnp.minimum(ki, qi), hp))],
        out_specs=pl.BlockSpec((1, T, LANES), lambda b, hp, qi, ki: (b, qi, hp)),
        out_shape=jax.ShapeDtypeStruct((B, S, H * MLA_V), BF16),
        scratch_shapes=[pltpu.VMEM((2, T, 1), F32), pltpu.VMEM((2, T, 1), F32),
                        pltpu.VMEM((2, T, LANES), F32)],
        compiler_params=_cparams(("parallel", "parallel", "parallel", "arbitrary")),
        name="mla_attention",
    )(q, k, v3)


def _pool_kernel(u_ref, w_ref, sc_ref, o_ref):
    S = u_ref.shape[1]
    row = lax.broadcasted_iota(jnp.int32, (S, LANES), 0)
    t1 = (row + 1).astype(F32)
    for gi, win in enumerate(POOL_WINDOWS):
        u = u_ref[0, :, gi * LANES:(gi + 1) * LANES]
        s = u
        k = 1
        while k < win:
            s = s + jnp.where(row >= k, pltpu.roll(s, k, 0), 0.0)
            k *= 2
        pooled = s / jnp.minimum(t1, float(win))
        mixed = _dot((pooled - u).astype(BF16), w_ref[gi].astype(BF16))
        o_ref[0, :, gi * LANES:(gi + 1) * LANES] = (
            mixed * sc_ref[:, gi * LANES:(gi + 1) * LANES]).astype(o_ref.dtype)


def _pool(pp3, w_pool, scale):
    B, S, W = pp3.shape
    return pl.pallas_call(
        _pool_kernel,
        grid=(B,),
        in_specs=[pl.BlockSpec((1, S, W), lambda b: (b, 0, 0)),
                  pl.BlockSpec(w_pool.shape, lambda b: (0, 0, 0)),
                  pl.BlockSpec((1, W), lambda b: (0, 0))],
        out_specs=pl.BlockSpec((1, S, W), lambda b: (b, 0, 0)),
        out_shape=jax.ShapeDtypeStruct((B, S, W), BF16),
        compiler_params=_cparams(("parallel",)),
        name="pool_mixer",
    )(pp3, w_pool, scale)


def _merge_kernel(x_ref, gt_ref, ya_ref, yb_ref, yc_ref, wa_ref, wb_ref, wc_ref, wo_ref, g1_ref, o_ref):
    D = x_ref.shape[1]
    merged = None
    for j, (y_ref, w_ref) in enumerate(((ya_ref, wa_ref), (yb_ref, wb_ref), (yc_ref, wc_ref))):
        y = _dot(y_ref[...], w_ref[...])
        gate = _sigmoid(gt_ref[:, j * D:(j + 1) * D].astype(F32))
        merged = gate * y if merged is None else merged + gate * y
    o_ref[...] = x_ref[...] + g1_ref[0] * _dot(merged.astype(BF16), wo_ref[...])


def _merge(x2, gates, ya, yb, yc, wa, wb, wc, wo, mod3, l, B, S, tr):
    N, D = x2.shape
    nb = S // tr
    base = l * B * 6
    full = lambda a: pl.BlockSpec(a.shape, lambda i: (0,) * a.ndim)
    rows = lambda a: pl.BlockSpec((tr, a.shape[1]), lambda i: (i, 0))
    return pl.pallas_call(
        _merge_kernel,
        grid=(N // tr,),
        in_specs=[rows(x2), rows(gates), rows(ya), rows(yb), rows(yc),
                  full(wa), full(wb), full(wc), full(wo),
                  pl.BlockSpec((1, 1, D), lambda i: (base + (i // nb) * 6 + 2, 0, 0))],
        out_specs=rows(x2),
        out_shape=jax.ShapeDtypeStruct((N, D), F32),
        compiler_params=_cparams(("parallel",)),
        name="merge_out",
    )(x2, gates, ya, yb, yc, wa, wb, wc, wo, mod3)


def _router_kernel(x_ref, g_ref, sh_ref, sc_ref, wr_ref, br_ref,
                   h_ref, idx_ref, wt_ref, rank_ref, cnt_ref, carry_ref):
    tr = x_ref.shape[0]

    @pl.when(pl.program_id(0) == 0)
    def _():
        carry_ref[...] = jnp.zeros_like(carry_ref)

    x = x_ref[...]
    ms = jnp.mean(x * x, axis=-1, keepdims=True)
    h = x * lax.rsqrt(ms + NORM_EPS) * g_ref[...]
    h = h * (1.0 + sc_ref[0]) + sh_ref[0]
    h_ref[...] = h
    h1, h2, _ = _split3(h)
    w1, w2, _ = _split3(wr_ref[...])
    logits = _dot(h1, w1) + (_dot(h1, w2) + _dot(h2, w1)) + br_ref[...]
    lane = lax.broadcasted_iota(jnp.int32, logits.shape, 1)
    work = logits
    vals, ids = [], []
    for _k in range(TOP_K):
        m = jnp.max(work, axis=-1, keepdims=True)
        sel = jnp.min(jnp.where(work == m, lane, LANES), axis=-1, keepdims=True)
        vals.append(m)
        ids.append(sel)
        work = jnp.where(lane == sel, -jnp.inf, work)
    es = [jnp.exp(v - vals[0]) for v in vals]
    den = es[0] + es[1] + es[2] + es[3]
    onehot = jnp.zeros(logits.shape, F32)
    idx_out = jnp.zeros(logits.shape, jnp.int32)
    wt_out = jnp.zeros(logits.shape, F32)
    for k in range(TOP_K):
        onehot = onehot + (lane == ids[k]).astype(F32)
        idx_out = jnp.where(lane == k, ids[k], idx_out)
        wt_out = jnp.where(lane == k, es[k] / den, wt_out)
    idx_ref[...] = idx_out
    wt_ref[...] = wt_out
    r = lax.broadcasted_iota(jnp.int32, (tr, tr), 0)
    c = lax.broadcasted_iota(jnp.int32, (tr, tr), 1)
    strict = (c < r).astype(BF16)
    before = _dot(strict, onehot.astype(BF16)) + carry_ref[...]
    rank_out = jnp.zeros(logits.shape, jnp.int32)
    for k in range(TOP_K):
        rk = jnp.sum(jnp.where(lane == ids[k], before, 0.0), axis=-1, keepdims=True)
        rank_out = jnp.where(lane == k, rk.astype(jnp.int32), rank_out)
    rank_ref[...] = rank_out
    carry_ref[...] = carry_ref[...] + jnp.sum(onehot, axis=0, keepdims=True)
    cnt_ref[...] = jnp.broadcast_to(carry_ref[...], cnt_ref.shape)


def _router(x2, g, mod3, l, wr_p, br_p, B, S, tr):
    N, D = x2.shape
    nb = S // tr
    base = l * B * 6
    rows = lambda w: pl.BlockSpec((tr, w), lambda i: (i, 0))

    def modspec(j):
        return pl.BlockSpec((1, 1, D), lambda i: (base + (i // nb) * 6 + j, 0, 0))

    return pl.pallas_call(
        _router_kernel,
        grid=(N // tr,),
        in_specs=[rows(D), pl.BlockSpec((1, D), lambda i: (0, 0)), modspec(3), modspec(4),
                  pl.BlockSpec(wr_p.shape, lambda i: (0, 0)), pl.BlockSpec((1, LANES), lambda i: (0, 0))],
        out_specs=(rows(D), rows(LANES), rows(LANES), rows(LANES),
                   pl.BlockSpec((8, LANES), lambda i: (0, 0))),
        out_shape=(jax.ShapeDtypeStruct((N, D), F32),
                   jax.ShapeDtypeStruct((N, LANES), jnp.int32),
                   jax.ShapeDtypeStruct((N, LANES), F32),
                   jax.ShapeDtypeStruct((N, LANES), jnp.int32),
                   jax.ShapeDtypeStruct((8, LANES), F32)),
        scratch_shapes=[pltpu.VMEM((1, LANES), F32)],
        compiler_params=_cparams(("arbitrary",)),
        name="router",
    )(x2, g, mod3, mod3, wr_p, br_p)


def _expert_kernel(te_ref, nu_ref, rt_ref, h_hbm, w1_ref, b1_ref, w2_ref, b2_ref, o_ref, xbuf, sem):
    t = pl.program_id(0)
    tm = xbuf.shape[0]
    F = w2_ref.shape[1]

    @pl.when(t < nu_ref[0])
    def _():
        base = t * tm

        def issue(r, c):
            tok = rt_ref[base + r]
            pltpu.make_async_copy(h_hbm.at[pl.ds(tok, 1)], xbuf.at[pl.ds(r, 1)], sem).start()
            return c

        lax.fori_loop(0, tm, issue, 0)
        pltpu.make_async_copy(h_hbm.at[pl.ds(0, tm)], xbuf, sem).wait()
        x = xbuf[...].astype(BF16)
        a = _dot(x, w1_ref[0]) + b1_ref[0]
        glu = jnp.minimum(a[:, :F], SWIGLU_LIMIT)
        lin = jnp.clip(a[:, F:], -SWIGLU_LIMIT, SWIGLU_LIMIT)
        act = glu * _sigmoid(SWIGLU_ALPHA * glu) * (lin + 1.0)
        o_ref[...] = _dot(act.astype(BF16), w2_ref[0]) + b2_ref[0]

    @pl.when(t >= nu_ref[0])
    def _():
        o_ref[...] = jnp.zeros_like(o_ref)


def _experts(tile_expert, n_used, row_token, h2, w1, b1, w2, b2, tm):
    n_tiles = tile_expert.shape[0]
    N, D = h2.shape
    E, _, F2 = w1.shape
    F = w2.shape[1]
    grid_spec = pltpu.PrefetchScalarGridSpec(
        num_scalar_prefetch=3,
        grid=(n_tiles,),
        in_specs=[pl.BlockSpec(memory_space=pl.ANY),
                  pl.BlockSpec((1, D, F2), lambda t, te, nu, rt: (te[t], 0, 0)),
                  pl.BlockSpec((1, 1, F2), lambda t, te, nu, rt: (te[t], 0, 0)),
                  pl.BlockSpec((1, F, D), lambda t, te, nu, rt: (te[t], 0, 0)),
                  pl.BlockSpec((1, 1, D), lambda t, te, nu, rt: (te[t], 0, 0))],
        out_specs=pl.BlockSpec((tm, D), lambda t, te, nu, rt: (t, 0)),
        scratch_shapes=[pltpu.VMEM((tm, D), F32), pltpu.SemaphoreType.DMA(())],
    )
    return pl.pallas_call(
        _expert_kernel,
        grid_spec=grid_spec,
        out_shape=jax.ShapeDtypeStruct((n_tiles * tm, D), F32),
        compiler_params=_cparams(("arbitrary",)),
        name="expert_mlp",
    )(tile_expert, n_used, row_token, h2, w1, b1.reshape(E, 1, F2), w2, b2.reshape(E, 1, D))


def _combine_kernel(slot_ref, ys_hbm, x_ref, wt_ref, g2_ref, o_ref, buf, sem):
    i = pl.program_id(0)
    tn = x_ref.shape[0]
    base = i * tn * TOP_K

    def issue(r, c):
        for k in range(TOP_K):
            s = slot_ref[base + r * TOP_K + k]
            pltpu.make_async_copy(ys_hbm.at[pl.ds(s, 1)], buf.at[k, pl.ds(r, 1)], sem).start()
        return c

    lax.fori_loop(0, tn, issue, 0)
    for k in range(TOP_K):
        pltpu.make_async_copy(ys_hbm.at[pl.ds(0, tn)], buf.at[k], sem).wait()
    wt = wt_ref[...]
    y = wt[:, 0:1] * buf[0]
    for k in range(1, TOP_K):
        y = y + wt[:, k:k + 1] * buf[k]
    o_ref[...] = x_ref[...] + g2_ref[0] * y


def _combine(slots_flat, ys, x2, wts, mod3, l, B, S, tn):
    N, D = x2.shape
    nb = S // tn
    base = l * B * 6
    grid_spec = pltpu.PrefetchScalarGridSpec(
        num_scalar_prefetch=1,
        grid=(N // tn,),
        in_specs=[pl.BlockSpec(memory_space=pl.ANY),
                  pl.BlockSpec((tn, D), lambda i, sl: (i, 0)),
                  pl.BlockSpec((tn, LANES), lambda i, sl: (i, 0)),
                  pl.BlockSpec((1, 1, D), lambda i, sl: (base + (i // nb) * 6 + 5, 0, 0))],
        out_specs=pl.BlockSpec((tn, D), lambda i, sl: (i, 0)),
        scratch_shapes=[pltpu.VMEM((TOP_K, tn, D), F32), pltpu.SemaphoreType.DMA(())],
    )
    return pl.pallas_call(
        _combine_kernel,
        grid_spec=grid_spec,
        out_shape=jax.ShapeDtypeStruct((N, D), F32),
        compiler_params=_cparams(("arbitrary",)),
        name="moe_combine",
    )(slots_flat, ys, x2, wts, mod3)


def _tile_rows(S, want):
    t = min(S, want)
    while S % t:
        t //= 2
    return t


def _prep_in_weights(w_in_l):
    HW = HGRN_HEADS * HGRN_DIM
    o = 0
    wh = w_in_l[:, o:o + 4 * HW]; o += 4 * HW
    wql = w_in_l[:, o:o + MLA_Q_RANK]; o += MLA_Q_RANK
    wkvl = w_in_l[:, o:o + MLA_KV_RANK]; o += MLA_KV_RANK
    wkr = w_in_l[:, o:o + MLA_ROPE]; o += MLA_ROPE
    wp = w_in_l[:, o:o + HW]; o += HW
    wg = w_in_l[:, o:]
    D = w_in_l.shape[0]
    wkr_p = jnp.zeros((D, LANES), w_in_l.dtype).at[:, MLA_NOPE:MLA_NOPE + MLA_ROPE].set(wkr)
    wm = jnp.concatenate([wql, wkvl, wkr_p], axis=1)
    return wg.astype(BF16), wh.astype(BF16), wm.astype(BF16), wp.astype(BF16)


def _pad_heads(w, n_heads, width):
    K = w.shape[0]
    w3 = w.reshape(K, n_heads, width)
    return jnp.pad(w3, ((0, 0), (0, 0), (0, LANES - width))).reshape(K, n_heads * LANES)


def kernel(x, c, positions, ada_w, ada_b, norm1_g, norm2_g, w_in, hgrn_lb, hgrn_onorm_g, mla_qlat_g, mla_kvlat_g, w_uq, w_ukv, q_norm_g, k_norm_g, w_pool, pool_scale, w_br_a, w_br_b, w_br_c, w_out, w_router, b_router, w_exp1, b_exp1, w_exp2, b_exp2):
    B, S, D = x.shape
    L = ada_w.shape[0]
    N = B * S
    H = MLA_HEADS
    tr = _tile_rows(S, 512)
    tc = _tile_rows(S, 256)
    ta = _tile_rows(S, 512)
    tm = 256
    tn = _tile_rows(S, 256)

    mod = _ada(c, ada_w, ada_b)
    mod3 = mod.reshape(L * B * 6, 1, D)

    inv_freq = 1.0 / (ROPE_THETA ** (jnp.arange(0, MLA_ROPE, 2, dtype=F32) / MLA_ROPE))
    freq_row = jnp.zeros((1, LANES), F32).at[0, MLA_NOPE:MLA_NOPE + MLA_ROPE].set(
        jnp.concatenate([inv_freq, inv_freq]))
    posf = positions.astype(F32).reshape(N, 1)

    n_tiles = (N * TOP_K) // tm + N_EXPERTS
    x2 = x.reshape(N, D)
    for l in range(L):
        wg, wh, wm, wp = _prep_in_weights(w_in[l])
        pg, ph, pm, pp = _inproj(x2, norm1_g[l].reshape(1, D), mod3, l, wg, wh, wm, wp, B, S, tr)

        ya = _hgrn(ph.reshape(B, S, -1), hgrn_lb, hgrn_onorm_g[l].reshape(1, HGRN_DIM), l, B, S, tc)

        wq_p = _pad_heads(w_uq[l], H, MLA_QK).astype(BF16)
        wkv3 = w_ukv[l].reshape(MLA_KV_RANK, H, MLA_NOPE + MLA_V)
        wkv_p = jnp.concatenate(
            [_pad_heads(wkv3[:, :, :MLA_NOPE].reshape(MLA_KV_RANK, H * MLA_NOPE), H, MLA_NOPE),
             wkv3[:, :, MLA_NOPE:].reshape(MLA_KV_RANK, H * MLA_V)], axis=1).astype(BF16)
        qn_p = jnp.pad(q_norm_g[l], (0, LANES - MLA_QK)).reshape(1, LANES)
        kn_p = jnp.pad(k_norm_g[l], (0, LANES - MLA_QK)).reshape(1, LANES)
        q, k, v = _mla_prep(pm, posf, freq_row, mla_qlat_g[l].reshape(1, -1), mla_kvlat_g[l].reshape(1, -1),
                            wq_p, wkv_p, qn_p, kn_p, B, S, tr)
        yb = _attention(q, k, v.reshape(B, S, H * MLA_V), B, S, ta)

        yc = _pool(pp.reshape(B, S, -1), w_pool[l], pool_scale[l].reshape(1, -1))

        x2 = _merge(x2, pg, ya.reshape(N, -1), yb.reshape(N, -1), yc.reshape(N, -1),
                    w_br_a[l].astype(BF16), w_br_b[l].astype(BF16), w_br_c[l].astype(BF16),
                    w_out[l].astype(BF16), mod3, l, B, S, tr)

        wr_p = jnp.pad(w_router[l], ((0, 0), (0, LANES - N_EXPERTS)))
        br_p = jnp.concatenate([b_router[l], jnp.full((LANES - N_EXPERTS,), NEG_BIG, F32)]).reshape(1, LANES)
        h2, idx, wts, rank, cnt = _router(x2, norm2_g[l].reshape(1, D), mod3, l, wr_p, br_p, B, S, tr)

        counts = cnt[0, :N_EXPERTS].astype(jnp.int32)
        tiles_per = (counts + tm - 1) // tm
        tile_end = jnp.cumsum(tiles_per)
        offsets = (tile_end - tiles_per) * tm
        n_used = tile_end[-1:]
        tile_ids = jnp.arange(n_tiles, dtype=jnp.int32)
        tile_expert = jnp.minimum(
            jnp.searchsorted(tile_end, jnp.minimum(tile_ids, n_used[0] - 1), side="right"),
            N_EXPERTS - 1).astype(jnp.int32)
        idx4 = idx[:, :TOP_K]
        slots = offsets[idx4] + rank[:, :TOP_K]
        tok = jnp.broadcast_to(jnp.arange(N, dtype=jnp.int32)[:, None], (N, TOP_K))
        row_token = jnp.zeros((n_tiles * tm,), jnp.int32).at[slots.reshape(-1)].set(tok.reshape(-1))

        ys = _experts(tile_expert, n_used.astype(jnp.int32), row_token, h2,
                      w_exp1[l].astype(BF16), b_exp1[l], w_exp2[l].astype(BF16), b_exp2[l], tm)
        x2 = _combine(slots.reshape(-1).astype(jnp.int32), ys, x2, wts, mod3, l, B, S, tn)
    return x2.reshape(B, S, D)
```

```python
import functools
import math

import jax
import jax.numpy as jnp
from jax import lax
from jax.experimental import pallas as pl
from jax.experimental.pallas import tpu as pltpu

F32 = jnp.float32
BF16 = jnp.bfloat16

LANES = 128
NORM_EPS = 1e-6
HGRN_HEADS = 4
HGRN_DIM = 128
HGRN_CHUNK = 64
HGRN_SUB = 16
MLA_HEADS = 8
MLA_NOPE = 64
MLA_ROPE = 32
MLA_QK = MLA_NOPE + MLA_ROPE
MLA_V = 64
MLA_Q_RANK = 384
MLA_KV_RANK = 256
ROPE_THETA = 10000.0
POOL_WINDOWS = (2, 4, 8, 16)
N_EXPERTS = 32
TOP_K = 4
SWIGLU_LIMIT = 7.0
SWIGLU_ALPHA = 1.702
NEG_BIG = -1e30

VMEM_LIMIT = 56 * 1024 * 1024


def _cparams(sem):
    return pltpu.CompilerParams(dimension_semantics=sem, vmem_limit_bytes=VMEM_LIMIT)


def _dot(a, b):
    return jnp.dot(a, b, preferred_element_type=F32)


def _dot_nt(a, b):
    return lax.dot_general(a, b, (((1,), (1,)), ((), ())), preferred_element_type=F32)


def _dot_tn(a, b):
    return lax.dot_general(a, b, (((0,), (0,)), ((), ())), preferred_element_type=F32)


def _split3(x):
    hi = x.astype(BF16)
    r = x - hi.astype(F32)
    mid = r.astype(BF16)
    lo = (r - mid.astype(F32)).astype(BF16)
    return hi, mid, lo


def _sigmoid(x):
    return 1.0 / (1.0 + jnp.exp(-x))


def _ada_kernel(c_ref, w_ref, b_ref, o_ref):
    c = c_ref[...]
    cond = (c * _sigmoid(c)).astype(BF16)
    o_ref[0] = _dot(cond, w_ref[0].astype(BF16)) + b_ref[0]


def _ada(c, ada_w, ada_b):
    L, D, W = ada_w.shape
    B = c.shape[0]
    tn = D
    return pl.pallas_call(
        _ada_kernel,
        grid=(L, W // tn),
        in_specs=[
            pl.BlockSpec((B, D), lambda l, j: (0, 0)),
            pl.BlockSpec((1, D, tn), lambda l, j: (l, 0, j)),
            pl.BlockSpec((1, 1, tn), lambda l, j: (l, 0, j)),
        ],
        out_specs=pl.BlockSpec((1, B, tn), lambda l, j: (l, 0, j)),
        out_shape=jax.ShapeDtypeStruct((L, B, W), F32),
        compiler_params=_cparams(("parallel", "parallel")),
        name="ada_mod",
    )(c, ada_w, ada_b.reshape(L, 1, W))


def _inproj_kernel(x_ref, g_ref, sh_ref, sc_ref, wg_ref, wh_ref, wm_ref, wp_ref,
                   og_ref, oh_ref, om_ref, op_ref):
    x = x_ref[...]
    ms = jnp.mean(x * x, axis=-1, keepdims=True)
    h = x * lax.rsqrt(ms + NORM_EPS) * g_ref[...]
    h = h * (1.0 + sc_ref[0]) + sh_ref[0]
    hb = h.astype(BF16)
    cw = 512
    for w_ref, o_ref in ((wg_ref, og_ref), (wh_ref, oh_ref), (wm_ref, om_ref), (wp_ref, op_ref)):
        width = w_ref.shape[1]
        step = cw if width % cw == 0 else width
        for j in range(0, width, step):
            o_ref[:, j:j + step] = _dot(hb, w_ref[:, j:j + step]).astype(o_ref.dtype)


def _inproj(x2, g, mod3, l, wg, wh, wm, wp, B, S, tr):
    N, D = x2.shape
    nb = S // tr
    base = l * B * 6

    def modspec(j):
        return pl.BlockSpec((1, 1, D), lambda i: (base + (i // nb) * 6 + j, 0, 0))

    def wspec(w):
        return pl.BlockSpec(w.shape, lambda i: (0, 0), pipeline_mode=pl.Buffered(1))

    outs = (
        jax.ShapeDtypeStruct((N, wg.shape[1]), BF16),
        jax.ShapeDtypeStruct((N, wh.shape[1]), F32),
        jax.ShapeDtypeStruct((N, wm.shape[1]), F32),
        jax.ShapeDtypeStruct((N, wp.shape[1]), F32),
    )
    return pl.pallas_call(
        _inproj_kernel,
        grid=(N // tr,),
        in_specs=[
            pl.BlockSpec((tr, D), lambda i: (i, 0)),
            pl.BlockSpec((1, D), lambda i: (0, 0)),
            modspec(0), modspec(1),
            wspec(wg), wspec(wh), wspec(wm), wspec(wp),
        ],
        out_specs=tuple(pl.BlockSpec((tr, o.shape[1]), lambda i: (i, 0)) for o in outs),
        out_shape=outs,
        compiler_params=_cparams(("parallel",)),
        name="in_proj",
    )(x2, g, mod3, mod3, wg, wh, wm, wp)


def _hgrn_kernel(layer, n_chunks, q_ref, f_ref, i_ref, g_ref, lb_ref, on_ref, o_ref, st_ref):
    C, R = HGRN_CHUNK, HGRN_SUB

    @pl.when(pl.program_id(2) == 0)
    def _():
        st_ref[...] = jnp.zeros_like(st_ref)

    lbr = lb_ref[...]
    e = jnp.exp(lbr - jnp.max(lbr, axis=0, keepdims=True))
    p = e / jnp.sum(e, axis=0, keepdims=True)
    lb = jnp.zeros((1, HGRN_DIM), F32)
    for j in range(1, layer + 1):
        lb = lb + p[j:j + 1]
    log_lb = jnp.log(lb)
    log_1mlb = jnp.log1p(-lb)
    one_m_lb = 1.0 - lb
    onorm = on_ref[...]

    row_c = lax.broadcasted_iota(jnp.int32, (C, C), 0)
    col_c = lax.broadcasted_iota(jnp.int32, (C, C), 1)
    tri = (col_c <= row_c).astype(BF16)
    row_r = lax.broadcasted_iota(jnp.int32, (R, 1), 0)

    def chunk(ci, carry):
        sl = pl.ds(pl.multiple_of(ci * C, C), C)
        q = q_ref[0, sl, :]
        z = f_ref[0, sl, :]
        v = i_ref[0, sl, :]
        g = g_ref[0, sl, :]
        ez = jnp.exp(-jnp.abs(z))
        log_sig = jnp.minimum(z, 0.0) - jnp.log1p(ez)
        a = log_1mlb + log_sig
        m = jnp.maximum(a, log_lb)
        lf = m + jnp.log(jnp.exp(a - m) + jnp.exp(log_lb - m))
        kk = one_m_lb * jnp.where(z >= 0.0, ez, 1.0) / (1.0 + ez)
        l1, l2, l3 = _split3(lf)
        b = _dot(tri, l1) + _dot(tri, l2) + _dot(tri, l3)
        st = st_ref[...]
        qs = (q * jnp.exp(b)).astype(BF16)
        o = _dot_nt(qs, st.astype(BF16))
        vb = v.astype(BF16)
        outs = []
        for i in range(C // R):
            r0 = i * R
            bi = b[r0:r0 + R]
            qi = q[r0:r0 + R]
            ki = kk[r0:r0 + R]
            vi = v[r0:r0 + R]
            oi = o[r0:r0 + R]
            if i > 0:
                ref = b[r0 - 1:r0]
                qt = (qi * jnp.exp(bi - ref)).astype(BF16)
                kt = (kk[:r0] * jnp.exp(ref - b[:r0])).astype(BF16)
                att = _dot_nt(qt, kt)
                oi = oi + _dot(att.astype(BF16), vb[:r0])
            for s in range(R):
                es = jnp.exp(jnp.minimum(bi - bi[s:s + 1], 0.0))
                col = jnp.sum(qi * es * ki[s:s + 1], axis=-1, keepdims=True)
                col = jnp.where(row_r >= s, col, 0.0)
                oi = oi + col * vi[s:s + 1]
            outs.append(oi)
        o = jnp.concatenate(outs, axis=0)
        bl = b[C - 1:C]
        kd = (kk * jnp.exp(bl - b)).astype(BF16)
        st_ref[...] = st * jnp.exp(bl) + _dot_tn(vb, kd)
        ms = jnp.mean(o * o, axis=-1, keepdims=True)
        y = o * lax.rsqrt(ms + NORM_EPS) * onorm
        y = y * (g * _sigmoid(g))
        o_ref[0, sl, :] = y.astype(o_ref.dtype)
        return carry

    lax.fori_loop(0, n_chunks, chunk, 0)


def _hgrn(ph, hgrn_lb, onorm_g, layer, B, S, tc):
    L = hgrn_lb.shape[0]
    H = HGRN_HEADS

    def spec(off):
        return pl.BlockSpec((1, tc, HGRN_DIM), lambda b, h, t: (b, t, off * H + h))

    return pl.pallas_call(
        functools.partial(_hgrn_kernel, layer, tc // HGRN_CHUNK),
        grid=(B, H, S // tc),
        in_specs=[spec(0), spec(1), spec(2), spec(3),
                  pl.BlockSpec((L, HGRN_DIM), lambda b, h, t: (0, h)),
                  pl.BlockSpec((1, HGRN_DIM), lambda b, h, t: (0, 0))],
        out_specs=pl.BlockSpec((1, tc, HGRN_DIM), lambda b, h, t: (b, t, h)),
        out_shape=jax.ShapeDtypeStruct((B, S, H * HGRN_DIM), BF16),
        scratch_shapes=[pltpu.VMEM((HGRN_DIM, HGRN_DIM), F32)],
        compiler_params=_cparams(("parallel", "parallel", "arbitrary")),
        name="hgrn2",
    )(ph, ph, ph, ph, hgrn_lb, onorm_g)


def _mla_prep_kernel(pm_ref, pos_ref, fr_ref, qg_ref, kvg_ref, wq_ref, wkv_ref, qn_ref, kn_ref,
                     q_ref, k_ref, v_ref):
    pm = pm_ref[...]
    ql = pm[:, :MLA_Q_RANK]
    kvl = pm[:, MLA_Q_RANK:MLA_Q_RANK + MLA_KV_RANK]
    kr = pm[:, MLA_Q_RANK + MLA_KV_RANK:]

    def rms(x, g):
        return x * lax.rsqrt(jnp.mean(x * x, axis=-1, keepdims=True) + NORM_EPS) * g

    qa = _dot(rms(ql, qg_ref[...]).astype(BF16), wq_ref[...])
    kva = _dot(rms(kvl, kvg_ref[...]).astype(BF16), wkv_ref[...])
    v_ref[...] = kva[:, MLA_HEADS * LANES:].astype(v_ref.dtype)

    ang = pos_ref[...] * fr_ref[...]
    cs = jnp.cos(ang)
    sn = jnp.sin(ang)
    lane = lax.broadcasted_iota(jnp.int32, ang.shape, 1)
    s1 = jnp.where(lane < MLA_NOPE + MLA_ROPE // 2, -sn, 0.0)
    s2 = jnp.where(lane >= MLA_NOPE + MLA_ROPE // 2, sn, 0.0)
    qn = qn_ref[...]
    kn = kn_ref[...]
    scale = 1.0 / math.sqrt(MLA_QK)
    half = MLA_ROPE // 2

    def head_norm_rope(xh, g):
        ss = jnp.sum(xh * xh, axis=-1, keepdims=True) * (1.0 / MLA_QK)
        xh = xh * lax.rsqrt(ss + NORM_EPS) * g
        return xh * cs + pltpu.roll(xh, LANES - half, 1) * s1 + pltpu.roll(xh, half, 1) * s2

    for h in range(MLA_HEADS):
        qh = qa[:, h * LANES:(h + 1) * LANES]
        q_ref[0, h] = (head_norm_rope(qh, qn) * scale).astype(q_ref.dtype)
        kh = kva[:, h * LANES:(h + 1) * LANES] + kr
        k_ref[0, h] = head_norm_rope(kh, kn).astype(k_ref.dtype)


def _mla_prep(pm, posf, freq_row, qlat_g, kvlat_g, wq, wkv, qn, kn, B, S, tr):
    N = pm.shape[0]
    nb = S // tr
    H = MLA_HEADS
    full = lambda a: pl.BlockSpec(a.shape, lambda i: (0,) * a.ndim)
    return pl.pallas_call(
        _mla_prep_kernel,
        grid=(N // tr,),
        in_specs=[pl.BlockSpec((tr, pm.shape[1]), lambda i: (i, 0)),
                  pl.BlockSpec((tr, 1), lambda i: (i, 0)),
                  full(freq_row), full(qlat_g), full(kvlat_g), full(wq), full(wkv), full(qn), full(kn)],
        out_specs=(pl.BlockSpec((1, H, tr, LANES), lambda i: (i // nb, 0, i % nb, 0)),
                   pl.BlockSpec((1, H, tr, LANES), lambda i: (i // nb, 0, i % nb, 0)),
                   pl.BlockSpec((tr, H * MLA_V), lambda i: (i, 0))),
        out_shape=(jax.ShapeDtypeStruct((B, H, S, LANES), BF16),
                   jax.ShapeDtypeStruct((B, H, S, LANES), BF16),
                   jax.ShapeDtypeStruct((N, H * MLA_V), BF16)),
        compiler_params=_cparams(("parallel",)),
        name="mla_prep",
    )(pm, posf, freq_row, qlat_g, kvlat_g, wq, wkv, qn, kn)


def _attn_kernel(q_ref, k_ref, v_ref, o_ref, m_ref, l_ref, acc_ref):
    qi = pl.program_id(2)
    ki = pl.program_id(3)
    T = q_ref.shape[2]

    @pl.when(ki == 0)
    def _():
        m_ref[...] = jnp.full_like(m_ref, NEG_BIG)
        l_ref[...] = jnp.zeros_like(l_ref)
        acc_ref[...] = jnp.zeros_like(acc_ref)

    def step(masked):
        vblk = v_ref[0]
        lane = lax.broadcasted_iota(jnp.int32, vblk.shape, 1)
        if masked:
            row = lax.broadcasted_iota(jnp.int32, (T, T), 0)
            col = lax.broadcasted_iota(jnp.int32, (T, T), 1)
            keep = col <= row
        for h in range(2):
            s = _dot_nt(q_ref[0, h], k_ref[0, h])
            if masked:
                s = jnp.where(keep, s, NEG_BIG)
            m_old = m_ref[h]
            m_new = jnp.maximum(m_old, jnp.max(s, axis=-1, keepdims=True))
            alpha = jnp.exp(m_old - m_new)
            p = jnp.exp(s - jnp.tile(m_new, (1, T // LANES)))
            l_ref[h] = alpha * l_ref[h] + jnp.sum(p, axis=-1, keepdims=True)
            vh = jnp.where((lane < MLA_V) if h == 0 else (lane >= MLA_V), vblk, jnp.zeros_like(vblk))
            acc_ref[h] = alpha * acc_ref[h] + _dot(p.astype(BF16), vh)
            m_ref[h] = m_new

    @pl.when(ki < qi)
    def _():
        step(False)

    @pl.when(ki == qi)
    def _():
        step(True)
        o_ref[0] = (acc_ref[0] / l_ref[0] + acc_ref[1] / l_ref[1]).astype(o_ref.dtype)


def _attention(q, k, v3, B, S, T):
    H = MLA_HEADS
    n = S // T
    return pl.pallas_call(
        _attn_kernel,
        grid=(B, H // 2, n, n),
        in_specs=[pl.BlockSpec((1, 2, T, LANES), lambda b, hp, qi, ki: (b, hp, qi, 0)),
                  pl.BlockSpec((1, 2, T, LANES), lambda b, hp, qi, ki: (b, hp, jnp.minimum(ki, qi), 0)),
                  pl.BlockSpec((1, T, LANES), lambda b, hp, qi, ki: (b, jnp.minimum(ki, qi), hp))],
        out_specs=pl.BlockSpec((1, T, LANES), lambda b, hp, qi, ki: (b, qi, hp)),
        out_shape=jax.ShapeDtypeStruct((B, S, H * MLA_V), BF16),
        scratch_shapes=[pltpu.VMEM((2, T, LANES), F32), pltpu.VMEM((2, T, LANES), F32),
                        pltpu.VMEM((2, T, LANES), F32)],
        compiler_params=_cparams(("parallel", "parallel", "parallel", "arbitrary")),
        name="mla_attention",
    )(q, k, v3)


def _pool_kernel(u_ref, w_ref, sc_ref, o_ref):
    S = u_ref.shape[1]
    row = lax.broadcasted_iota(jnp.int32, (S, LANES), 0)
    t1 = (row + 1).astype(F32)
    for gi, win in enumerate(POOL_WINDOWS):
        u = u_ref[0, :, gi * LANES:(gi + 1) * LANES]
        s = u
        k = 1
        while k < win:
            s = s + jnp.where(row >= k, pltpu.roll(s, k, 0), 0.0)
            k *= 2
        pooled = s / jnp.minimum(t1, float(win))
        mixed = _dot((pooled - u).astype(BF16), w_ref[gi].astype(BF16))
        o_ref[0, :, gi * LANES:(gi + 1) * LANES] = (
            mixed * sc_ref[:, gi * LANES:(gi + 1) * LANES]).astype(o_ref.dtype)


def _pool(pp3, w_pool, scale):
    B, S, W = pp3.shape
    return pl.pallas_call(
        _pool_kernel,
        grid=(B,),
        in_specs=[pl.BlockSpec((1, S, W), lambda b: (b, 0, 0)),
                  pl.BlockSpec(w_pool.shape, lambda b: (0, 0, 0)),
                  pl.BlockSpec((1, W), lambda b: (0, 0))],
        out_specs=pl.BlockSpec((1, S, W), lambda b: (b, 0, 0)),
        out_shape=jax.ShapeDtypeStruct((B, S, W), BF16),
        compiler_params=_cparams(("parallel",)),
        name="pool_mixer",
    )(pp3, w_pool, scale)


def _merge_kernel(x_ref, gt_ref, ya_ref, yb_ref, yc_ref, wa_ref, wb_ref, wc_ref, wo_ref, g1_ref, o_ref):
    D = x_ref.shape[1]
    merged = None
    for j, (y_ref, w_ref) in enumerate(((ya_ref, wa_ref), (yb_ref, wb_ref), (yc_ref, wc_ref))):
        y = _dot(y_ref[...], w_ref[...])
        gate = _sigmoid(gt_ref[:, j * D:(j + 1) * D].astype(F32))
        merged = gate * y if merged is None else merged + gate * y
    o_ref[...] = x_ref[...] + g1_ref[0] * _dot(merged.astype(BF16), wo_ref[...])


def _merge(x2, gates, ya, yb, yc, wa, wb, wc, wo, mod3, l, B, S, tr):
    N, D = x2.shape
    nb = S // tr
    base = l * B * 6
    full = lambda a: pl.BlockSpec(a.shape, lambda i: (0,) * a.ndim)
    rows = lambda a: pl.BlockSpec((tr, a.shape[1]), lambda i: (i, 0))
    return pl.pallas_call(
        _merge_kernel,
        grid=(N // tr,),
        in_specs=[rows(x2), rows(gates), rows(ya), rows(yb), rows(yc),
                  full(wa), full(wb), full(wc), full(wo),
                  pl.BlockSpec((1, 1, D), lambda i: (base + (i // nb) * 6 + 2, 0, 0))],
        out_specs=rows(x2),
        out_shape=jax.ShapeDtypeStruct((N, D), F32),
        compiler_params=_cparams(("parallel",)),
        name="merge_out",
    )(x2, gates, ya, yb, yc, wa, wb, wc, wo, mod3)


def _router_kernel(x_ref, g_ref, sh_ref, sc_ref, wr_ref, br_ref,
                   h_ref, idx_ref, wt_ref, rank_ref, cnt_ref, carry_ref):
    tr = x_ref.shape[0]

    @pl.when(pl.program_id(0) == 0)
    def _():
        carry_ref[...] = jnp.zeros_like(carry_ref)

    x = x_ref[...]
    ms = jnp.mean(x * x, axis=-1, keepdims=True)
    h = x * lax.rsqrt(ms + NORM_EPS) * g_ref[...]
    h = h * (1.0 + sc_ref[0]) + sh_ref[0]
    h_ref[...] = h
    h1, h2, _ = _split3(h)
    w1, w2, _ = _split3(wr_ref[...])
    logits = _dot(h1, w1) + (_dot(h1, w2) + _dot(h2, w1)) + br_ref[...]
    lane = lax.broadcasted_iota(jnp.int32, logits.shape, 1)
    work = logits
    vals, ids = [], []
    for _k in range(TOP_K):
        m = jnp.max(work, axis=-1, keepdims=True)
        sel = jnp.min(jnp.where(work == m, lane, LANES), axis=-1, keepdims=True)
        vals.append(m)
        ids.append(sel)
        work = jnp.where(lane == sel, -jnp.inf, work)
    es = [jnp.exp(v - vals[0]) for v in vals]
    den = es[0] + es[1] + es[2] + es[3]
    onehot = jnp.zeros(logits.shape, F32)
    idx_out = jnp.zeros(logits.shape, jnp.int32)
    wt_out = jnp.zeros(logits.shape, F32)
    for k in range(TOP_K):
        onehot = onehot + (lane == ids[k]).astype(F32)
        idx_out = jnp.where(lane == k, ids[k], idx_out)
        wt_out = jnp.where(lane == k, es[k] / den, wt_out)
    idx_ref[...] = idx_out
    wt_ref[...] = wt_out
    r = lax.broadcasted_iota(jnp.int32, (tr, tr), 0)
    c = lax.broadcasted_iota(jnp.int32, (tr, tr), 1)
    strict = (c < r).astype(BF16)
    before = _dot(strict, onehot.astype(BF16)) + carry_ref[...]
    rank_out = jnp.zeros(logits.shape, jnp.int32)
    for k in range(TOP_K):
        rk = jnp.sum(jnp.where(lane == ids[k], before, 0.0), axis=-1, keepdims=True)
        rank_out = jnp.where(lane == k, rk.astype(jnp.int32), rank_out)
    rank_ref[...] = rank_out
    carry_ref[...] = carry_ref[...] + jnp.sum(onehot, axis=0, keepdims=True)
    cnt_ref[...] = jnp.broadcast_to(carry_ref[...], cnt_ref.shape)


def _router(x2, g, mod3, l, wr_p, br_p, B, S, tr):
    N, D = x2.shape
    nb = S // tr
    base = l * B * 6
    rows = lambda w: pl.BlockSpec((tr, w), lambda i: (i, 0))

    def modspec(j):
        return pl.BlockSpec((1, 1, D), lambda i: (base + (i // nb) * 6 + j, 0, 0))

    return pl.pallas_call(
        _router_kernel,
        grid=(N // tr,),
        in_specs=[rows(D), pl.BlockSpec((1, D), lambda i: (0, 0)), modspec(3), modspec(4),
                  pl.BlockSpec(wr_p.shape, lambda i: (0, 0)), pl.BlockSpec((1, LANES), lambda i: (0, 0))],
        out_specs=(rows(D), rows(LANES), rows(LANES), rows(LANES),
                   pl.BlockSpec((8, LANES), lambda i: (0, 0))),
        out_shape=(jax.ShapeDtypeStruct((N, D), F32),
                   jax.ShapeDtypeStruct((N, LANES), jnp.int32),
                   jax.ShapeDtypeStruct((N, LANES), F32),
                   jax.ShapeDtypeStruct((N, LANES), jnp.int32),
                   jax.ShapeDtypeStruct((8, LANES), F32)),
        scratch_shapes=[pltpu.VMEM((1, LANES), F32)],
        compiler_params=_cparams(("arbitrary",)),
        name="router",
    )(x2, g, mod3, mod3, wr_p, br_p)


def _dispatch_kernel(slot_ref, zflag_ref, h_ref, xs_hbm, zbuf, sem):
    i = pl.program_id(0)
    tn = h_ref.shape[0]
    tm = zbuf.shape[0]
    n_tiles = xs_hbm.shape[0] // tm

    @pl.when(i == 0)
    def _():
        zbuf[...] = jnp.zeros_like(zbuf)

        def zero_copy(t):
            return pltpu.make_async_copy(zbuf, xs_hbm.at[pl.ds(pl.multiple_of(t * tm, tm), tm)], sem)

        def zstart(t, c):
            @pl.when(zflag_ref[t] > 0)
            def _():
                zero_copy(t).start()
            return c

        def zwait(t, c):
            @pl.when(zflag_ref[t] > 0)
            def _():
                zero_copy(t).wait()
            return c

        lax.fori_loop(0, n_tiles, zstart, 0)
        lax.fori_loop(0, n_tiles, zwait, 0)

    base = i * tn * TOP_K

    def issue(r, c):
        for k in range(TOP_K):
            s = slot_ref[base + r * TOP_K + k]
            pltpu.make_async_copy(h_ref.at[pl.ds(r, 1)], xs_hbm.at[pl.ds(s, 1)], sem).start()
        return c

    lax.fori_loop(0, tn, issue, 0)
    for k in range(TOP_K):
        pltpu.make_async_copy(h_ref, xs_hbm.at[pl.ds(0, tn)], sem).wait()


def _dispatch(slots_flat, zero_flags, h2, n_rows, tn, tm):
    N, D = h2.shape
    grid_spec = pltpu.PrefetchScalarGridSpec(
        num_scalar_prefetch=2,
        grid=(N // tn,),
        in_specs=[pl.BlockSpec((tn, D), lambda i, sl, zf: (i, 0))],
        out_specs=pl.BlockSpec(memory_space=pl.ANY),
        scratch_shapes=[pltpu.VMEM((tm, D), F32), pltpu.SemaphoreType.DMA(())],
    )
    return pl.pallas_call(
        _dispatch_kernel,
        grid_spec=grid_spec,
        out_shape=jax.ShapeDtypeStruct((n_rows, D), F32),
        compiler_params=_cparams(("arbitrary",)),
        name="moe_dispatch",
    )(slots_flat, zero_flags, h2)


def _expert_kernel(te_ref, nu_ref, x_ref, w1_ref, b1_ref, w2_ref, b2_ref, o_ref, w1b, w2b):
    t = pl.program_id(0)
    F = w2_ref.shape[1]

    @pl.when(t < nu_ref[0])
    def _():
        @pl.when(jnp.logical_or(t == 0, te_ref[t] != te_ref[jnp.maximum(t - 1, 0)]))
        def _():
            w1b[...] = w1_ref[0].astype(BF16)
            w2b[...] = w2_ref[0].astype(BF16)

        x = x_ref[...].astype(BF16)
        a = _dot(x, w1b[...]) + b1_ref[0]
        glu = jnp.minimum(a[:, :F], SWIGLU_LIMIT)
        lin = jnp.clip(a[:, F:], -SWIGLU_LIMIT, SWIGLU_LIMIT)
        act = glu * _sigmoid(SWIGLU_ALPHA * glu) * (lin + 1.0)
        o_ref[...] = _dot(act.astype(BF16), w2b[...]) + b2_ref[0]

    @pl.when(t >= nu_ref[0])
    def _():
        o_ref[...] = jnp.zeros_like(o_ref)


def _experts(tile_expert, n_used, xs, w1, b1, w2, b2, tm):
    n_tiles = tile_expert.shape[0]
    D = xs.shape[1]
    E, _, F2 = w1.shape
    F = w2.shape[1]
    grid_spec = pltpu.PrefetchScalarGridSpec(
        num_scalar_prefetch=2,
        grid=(n_tiles,),
        in_specs=[pl.BlockSpec((tm, D), lambda t, te, nu: (jnp.minimum(t, nu[0] - 1), 0)),
                  pl.BlockSpec((1, D, F2), lambda t, te, nu: (te[t], 0, 0)),
                  pl.BlockSpec((1, 1, F2), lambda t, te, nu: (te[t], 0, 0)),
                  pl.BlockSpec((1, F, D), lambda t, te, nu: (te[t], 0, 0)),
                  pl.BlockSpec((1, 1, D), lambda t, te, nu: (te[t], 0, 0))],
        out_specs=pl.BlockSpec((tm, D), lambda t, te, nu: (t, 0)),
        scratch_shapes=[pltpu.VMEM((D, F2), BF16), pltpu.VMEM((F, D), BF16)],
    )
    return pl.pallas_call(
        _expert_kernel,
        grid_spec=grid_spec,
        out_shape=jax.ShapeDtypeStruct((n_tiles * tm, D), F32),
        compiler_params=_cparams(("arbitrary",)),
        name="expert_mlp",
    )(tile_expert, n_used, xs, w1, b1.reshape(E, 1, F2), w2, b2.reshape(E, 1, D))


def _combine_kernel(slot_ref, ys_hbm, x_ref, wt_ref, g2_ref, o_ref, buf, sem):
    i = pl.program_id(0)
    tn = x_ref.shape[0]
    base = i * tn * TOP_K

    def issue(r, c):
        for k in range(TOP_K):
            s = slot_ref[base + r * TOP_K + k]
            pltpu.make_async_copy(ys_hbm.at[pl.ds(s, 1)], buf.at[k, pl.ds(r, 1)], sem).start()
        return c

    lax.fori_loop(0, tn, issue, 0)
    for k in range(TOP_K):
        pltpu.make_async_copy(ys_hbm.at[pl.ds(0, tn)], buf.at[k], sem).wait()
    wt = wt_ref[...]
    y = wt[:, 0:1] * buf[0]
    for k in range(1, TOP_K):
        y = y + wt[:, k:k + 1] * buf[k]
    o_ref[...] = x_ref[...] + g2_ref[0] * y


def _combine(slots_flat, ys, x2, wts, mod3, l, B, S, tn):
    N, D = x2.shape
    nb = S // tn
    base = l * B * 6
    grid_spec = pltpu.PrefetchScalarGridSpec(
        num_scalar_prefetch=1,
        grid=(N // tn,),
        in_specs=[pl.BlockSpec(memory_space=pl.ANY),
                  pl.BlockSpec((tn, D), lambda i, sl: (i, 0)),
                  pl.BlockSpec((tn, LANES), lambda i, sl: (i, 0)),
                  pl.BlockSpec((1, 1, D), lambda i, sl: (base + (i // nb) * 6 + 5, 0, 0))],
        out_specs=pl.BlockSpec((tn, D), lambda i, sl: (i, 0)),
        scratch_shapes=[pltpu.VMEM((TOP_K, tn, D), F32), pltpu.SemaphoreType.DMA(())],
    )
    return pl.pallas_call(
        _combine_kernel,
        grid_spec=grid_spec,
        out_shape=jax.ShapeDtypeStruct((N, D), F32),
        compiler_params=_cparams(("arbitrary",)),
        name="moe_combine",
    )(slots_flat, ys, x2, wts, mod3)


def _tile_rows(S, want):
    t = min(S, want)
    while S % t:
        t //= 2
    return t


def _prep_in_weights(w_in_l):
    HW = HGRN_HEADS * HGRN_DIM
    o = 0
    wh = w_in_l[:, o:o + 4 * HW]; o += 4 * HW
    wql = w_in_l[:, o:o + MLA_Q_RANK]; o += MLA_Q_RANK
    wkvl = w_in_l[:, o:o + MLA_KV_RANK]; o += MLA_KV_RANK
    wkr = w_in_l[:, o:o + MLA_ROPE]; o += MLA_ROPE
    wp = w_in_l[:, o:o + HW]; o += HW
    wg = w_in_l[:, o:]
    D = w_in_l.shape[0]
    wkr_p = jnp.zeros((D, LANES), w_in_l.dtype).at[:, MLA_NOPE:MLA_NOPE + MLA_ROPE].set(wkr)
    wm = jnp.concatenate([wql, wkvl, wkr_p], axis=1)
    return wg.astype(BF16), wh.astype(BF16), wm.astype(BF16), wp.astype(BF16)


def _pad_heads(w, n_heads, width):
    K = w.shape[0]
    w3 = w.reshape(K, n_heads, width)
    return jnp.pad(w3, ((0, 0), (0, 0), (0, LANES - width))).reshape(K, n_heads * LANES)


def kernel(x, c, positions, ada_w, ada_b, norm1_g, norm2_g, w_in, hgrn_lb, hgrn_onorm_g, mla_qlat_g, mla_kvlat_g, w_uq, w_ukv, q_norm_g, k_norm_g, w_pool, pool_scale, w_br_a, w_br_b, w_br_c, w_out, w_router, b_router, w_exp1, b_exp1, w_exp2, b_exp2):
    B, S, D = x.shape
    L = ada_w.shape[0]
    N = B * S
    H = MLA_HEADS
    tr = _tile_rows(S, 512)
    tc = _tile_rows(S, 256)
    ta = _tile_rows(S, 512)
    tm = 512
    tn = _tile_rows(S, 256)

    mod = _ada(c, ada_w, ada_b)
    mod3 = mod.reshape(L * B * 6, 1, D)

    inv_freq = 1.0 / (ROPE_THETA ** (jnp.arange(0, MLA_ROPE, 2, dtype=F32) / MLA_ROPE))
    freq_row = jnp.zeros((1, LANES), F32).at[0, MLA_NOPE:MLA_NOPE + MLA_ROPE].set(
        jnp.concatenate([inv_freq, inv_freq]))
    posf = positions.astype(F32).reshape(N, 1)

    n_tiles = (N * TOP_K) // tm + N_EXPERTS
    x2 = x.reshape(N, D)
    for l in range(L):
        wg, wh, wm, wp = _prep_in_weights(w_in[l])
        pg, ph, pm, pp = _inproj(x2, norm1_g[l].reshape(1, D), mod3, l, wg, wh, wm, wp, B, S, tr)

        ya = _hgrn(ph.reshape(B, S, -1), hgrn_lb, hgrn_onorm_g[l].reshape(1, HGRN_DIM), l, B, S, tc)

        wq_p = _pad_heads(w_uq[l], H, MLA_QK).astype(BF16)
        wkv3 = w_ukv[l].reshape(MLA_KV_RANK, H, MLA_NOPE + MLA_V)
        wkv_p = jnp.concatenate(
            [_pad_heads(wkv3[:, :, :MLA_NOPE].reshape(MLA_KV_RANK, H * MLA_NOPE), H, MLA_NOPE),
             wkv3[:, :, MLA_NOPE:].reshape(MLA_KV_RANK, H * MLA_V)], axis=1).astype(BF16)
        qn_p = jnp.pad(q_norm_g[l], (0, LANES - MLA_QK)).reshape(1, LANES)
        kn_p = jnp.pad(k_norm_g[l], (0, LANES - MLA_QK)).reshape(1, LANES)
        q, k, v = _mla_prep(pm, posf, freq_row, mla_qlat_g[l].reshape(1, -1), mla_kvlat_g[l].reshape(1, -1),
                            wq_p, wkv_p, qn_p, kn_p, B, S, tr)
        yb = _attention(q, k, v.reshape(B, S, H * MLA_V), B, S, ta)

        yc = _pool(pp.reshape(B, S, -1), w_pool[l], pool_scale[l].reshape(1, -1))

        x2 = _merge(x2, pg, ya.reshape(N, -1), yb.reshape(N, -1), yc.reshape(N, -1),
                    w_br_a[l].astype(BF16), w_br_b[l].astype(BF16), w_br_c[l].astype(BF16),
                    w_out[l].astype(BF16), mod3, l, B, S, tr)

        wr_p = jnp.pad(w_router[l], ((0, 0), (0, LANES - N_EXPERTS)))
        br_p = jnp.concatenate([b_router[l], jnp.full((LANES - N_EXPERTS,), NEG_BIG, F32)]).reshape(1, LANES)
        h2, idx, wts, rank, cnt = _router(x2, norm2_g[l].reshape(1, D), mod3, l, wr_p, br_p, B, S, tr)

        counts = cnt[0, :N_EXPERTS].astype(jnp.int32)
        tiles_per = (counts + tm - 1) // tm
        tile_end = jnp.cumsum(tiles_per)
        offsets = (tile_end - tiles_per) * tm
        n_used = tile_end[-1:]
        tile_ids = jnp.minimum(jnp.arange(n_tiles, dtype=jnp.int32), n_used[0] - 1)
        tile_expert = jnp.minimum(
            jnp.sum((tile_end[None, :] <= tile_ids[:, None]).astype(jnp.int32), axis=1), N_EXPERTS - 1)
        idx4 = idx[:, :TOP_K]
        e_ids = jnp.arange(N_EXPERTS, dtype=jnp.int32)
        slots = rank[:, :TOP_K] + jnp.sum(
            jnp.where(idx4[:, :, None] == e_ids[None, None, :], offsets[None, None, :], 0), axis=-1)
        slots_flat = slots.reshape(-1).astype(jnp.int32)
        all_tiles = jnp.arange(n_tiles, dtype=jnp.int32)
        is_tail = jnp.any((tile_end[None, :] - 1 == all_tiles[:, None]) & (tiles_per[None, :] > 0), axis=1)
        zero_flags = (is_tail | (all_tiles >= n_used[0])).astype(jnp.int32)

        xs = _dispatch(slots_flat, zero_flags, h2, n_tiles * tm, tn, tm)
        ys = _experts(tile_expert.astype(jnp.int32), n_used.astype(jnp.int32), xs,
                      w_exp1[l], b_exp1[l], w_exp2[l], b_exp2[l], tm)
        x2 = _combine(slots_flat, ys, x2, wts, mod3, l, B, S, tn)
    return x2.reshape(B, S, D)
```

```python
import functools
import math

import jax
import jax.numpy as jnp
from jax import lax
from jax.experimental import pallas as pl
from jax.experimental.pallas import tpu as pltpu

F32 = jnp.float32
BF16 = jnp.bfloat16

LANES = 128
NORM_EPS = 1e-6
HGRN_HEADS = 4
HGRN_DIM = 128
HGRN_CHUNK = 64
HGRN_SUB = 16
MLA_HEADS = 8
MLA_NOPE = 64
MLA_ROPE = 32
MLA_QK = MLA_NOPE + MLA_ROPE
MLA_V = 64
MLA_Q_RANK = 384
MLA_KV_RANK = 256
ROPE_THETA = 10000.0
POOL_WINDOWS = (2, 4, 8, 16)
N_EXPERTS = 32
TOP_K = 4
SWIGLU_LIMIT = 7.0
SWIGLU_ALPHA = 1.702
NEG_BIG = -1e30

VMEM_LIMIT = 56 * 1024 * 1024


def _cparams(sem):
    return pltpu.CompilerParams(dimension_semantics=sem, vmem_limit_bytes=VMEM_LIMIT)


def _dot(a, b):
    return jnp.dot(a, b, preferred_element_type=F32)


def _dot_nt(a, b):
    return lax.dot_general(a, b, (((1,), (1,)), ((), ())), preferred_element_type=F32)


def _dot_tn(a, b):
    return lax.dot_general(a, b, (((0,), (0,)), ((), ())), preferred_element_type=F32)


def _split3(x):
    hi = x.astype(BF16)
    r = x - hi.astype(F32)
    mid = r.astype(BF16)
    lo = (r - mid.astype(F32)).astype(BF16)
    return hi, mid, lo


def _sigmoid(x):
    return 1.0 / (1.0 + jnp.exp(-x))


def _ada_kernel(c_ref, w_ref, b_ref, o_ref):
    c = c_ref[...]
    cond = (c * _sigmoid(c)).astype(BF16)
    o_ref[0] = _dot(cond, w_ref[0].astype(BF16)) + b_ref[0]


def _ada(c, ada_w, ada_b):
    L, D, W = ada_w.shape
    B = c.shape[0]
    tn = D
    return pl.pallas_call(
        _ada_kernel,
        grid=(L, W // tn),
        in_specs=[
            pl.BlockSpec((B, D), lambda l, j: (0, 0)),
            pl.BlockSpec((1, D, tn), lambda l, j: (l, 0, j)),
            pl.BlockSpec((1, 1, tn), lambda l, j: (l, 0, j)),
        ],
        out_specs=pl.BlockSpec((1, B, tn), lambda l, j: (l, 0, j)),
        out_shape=jax.ShapeDtypeStruct((L, B, W), F32),
        compiler_params=_cparams(("parallel", "parallel")),
        name="ada_mod",
    )(c, ada_w, ada_b.reshape(L, 1, W))


def _inproj_kernel(x_ref, g_ref, sh_ref, sc_ref, wg_ref, wh_ref, wm_ref, wp_ref,
                   og_ref, oh_ref, om_ref, op_ref):
    x = x_ref[...]
    ms = jnp.mean(x * x, axis=-1, keepdims=True)
    h = x * lax.rsqrt(ms + NORM_EPS) * g_ref[...]
    h = h * (1.0 + sc_ref[0]) + sh_ref[0]
    hb = h.astype(BF16)
    cw = 512
    for w_ref, o_ref in ((wg_ref, og_ref), (wh_ref, oh_ref), (wm_ref, om_ref), (wp_ref, op_ref)):
        width = w_ref.shape[1]
        step = cw if width % cw == 0 else width
        for j in range(0, width, step):
            o_ref[:, j:j + step] = _dot(hb, w_ref[:, j:j + step]).astype(o_ref.dtype)


def _inproj(x2, g, mod3, l, wg, wh, wm, wp, B, S, tr):
    N, D = x2.shape
    nb = S // tr
    base = l * B * 6

    def modspec(j):
        return pl.BlockSpec((1, 1, D), lambda i: (base + (i // nb) * 6 + j, 0, 0))

    def wspec(w):
        return pl.BlockSpec(w.shape, lambda i: (0, 0), pipeline_mode=pl.Buffered(1))

    outs = (
        jax.ShapeDtypeStruct((N, wg.shape[1]), BF16),
        jax.ShapeDtypeStruct((N, wh.shape[1]), F32),
        jax.ShapeDtypeStruct((N, wm.shape[1]), F32),
        jax.ShapeDtypeStruct((N, wp.shape[1]), F32),
    )
    return pl.pallas_call(
        _inproj_kernel,
        grid=(N // tr,),
        in_specs=[
            pl.BlockSpec((tr, D), lambda i: (i, 0)),
            pl.BlockSpec((1, D), lambda i: (0, 0)),
            modspec(0), modspec(1),
            wspec(wg), wspec(wh), wspec(wm), wspec(wp),
        ],
        out_specs=tuple(pl.BlockSpec((tr, o.shape[1]), lambda i: (i, 0)) for o in outs),
        out_shape=outs,
        compiler_params=_cparams(("parallel",)),
        name="in_proj",
    )(x2, g, mod3, mod3, wg, wh, wm, wp)


def _hgrn_kernel(layer, n_chunks, q_ref, f_ref, i_ref, g_ref, lb_ref, on_ref, o_ref, st_ref):
    C, R, DH = HGRN_CHUNK, HGRN_SUB, HGRN_DIM

    @pl.when(pl.program_id(1) == 0)
    def _():
        st_ref[...] = jnp.zeros_like(st_ref)

    lbr = lb_ref[...]
    e = jnp.exp(lbr - jnp.max(lbr, axis=0, keepdims=True))
    p = e / jnp.sum(e, axis=0, keepdims=True)
    lb_all = jnp.zeros((1, lbr.shape[1]), F32)
    for j in range(1, layer + 1):
        lb_all = lb_all + p[j:j + 1]
    onorm = on_ref[...]

    row_c = lax.broadcasted_iota(jnp.int32, (C, C), 0)
    col_c = lax.broadcasted_iota(jnp.int32, (C, C), 1)
    tri = (col_c <= row_c).astype(BF16)
    row_r = lax.broadcasted_iota(jnp.int32, (R, 1), 0)

    def head_chunk(sl, h):
        hs = slice(h * DH, (h + 1) * DH)
        lb = lb_all[:, hs]
        log_lb = jnp.log(lb)
        log_1mlb = jnp.log1p(-lb)
        q = q_ref[0, sl, hs]
        z = f_ref[0, sl, hs]
        v = i_ref[0, sl, hs]
        g = g_ref[0, sl, hs]
        ez = jnp.exp(-jnp.abs(z))
        log_sig = jnp.minimum(z, 0.0) - jnp.log1p(ez)
        a = log_1mlb + log_sig
        m = jnp.maximum(a, log_lb)
        lf = m + jnp.log(jnp.exp(a - m) + jnp.exp(log_lb - m))
        kk = (1.0 - lb) * jnp.where(z >= 0.0, ez, 1.0) / (1.0 + ez)
        l1, l2, l3 = _split3(lf)
        b = _dot(tri, l1) + _dot(tri, l2) + _dot(tri, l3)
        st = st_ref[h]
        qs = (q * jnp.exp(b)).astype(BF16)
        o = _dot_nt(qs, st.astype(BF16))
        vb = v.astype(BF16)
        outs = []
        for i in range(C // R):
            r0 = i * R
            bi = b[r0:r0 + R]
            qi = q[r0:r0 + R]
            ki = kk[r0:r0 + R]
            vi = v[r0:r0 + R]
            oi = o[r0:r0 + R]
            if i > 0:
                ref = b[r0 - 1:r0]
                qt = (qi * jnp.exp(bi - ref)).astype(BF16)
                kt = (kk[:r0] * jnp.exp(ref - b[:r0])).astype(BF16)
                att = _dot_nt(qt, kt)
                oi = oi + _dot(att.astype(BF16), vb[:r0])
            for s in range(R):
                es = jnp.exp(jnp.minimum(bi - bi[s:s + 1], 0.0))
                col = jnp.sum(qi * es * ki[s:s + 1], axis=-1, keepdims=True)
                col = jnp.where(row_r >= s, col, 0.0)
                oi = oi + col * vi[s:s + 1]
            outs.append(oi)
        o = jnp.concatenate(outs, axis=0)
        bl = b[C - 1:C]
        kd = (kk * jnp.exp(bl - b)).astype(BF16)
        st_ref[h] = st * jnp.exp(bl) + _dot_tn(vb, kd)
        ms = jnp.mean(o * o, axis=-1, keepdims=True)
        y = o * lax.rsqrt(ms + NORM_EPS) * onorm
        y = y * (g * _sigmoid(g))
        o_ref[0, sl, hs] = y.astype(o_ref.dtype)

    def chunk(ci, carry):
        sl = pl.ds(pl.multiple_of(ci * C, C), C)
        for h in range(HGRN_HEADS):
            head_chunk(sl, h)
        return carry

    lax.fori_loop(0, n_chunks, chunk, 0)


def _hgrn(ph, hgrn_lb, onorm_g, layer, B, S, tc):
    L = hgrn_lb.shape[0]
    W = HGRN_HEADS * HGRN_DIM

    def spec(off):
        return pl.BlockSpec((1, tc, W), lambda b, t: (b, t, off))

    return pl.pallas_call(
        functools.partial(_hgrn_kernel, layer, tc // HGRN_CHUNK),
        grid=(B, S // tc),
        in_specs=[spec(0), spec(1), spec(2), spec(3),
                  pl.BlockSpec((L, W), lambda b, t: (0, 0)),
                  pl.BlockSpec((1, HGRN_DIM), lambda b, t: (0, 0))],
        out_specs=pl.BlockSpec((1, tc, W), lambda b, t: (b, t, 0)),
        out_shape=jax.ShapeDtypeStruct((B, S, W), BF16),
        scratch_shapes=[pltpu.VMEM((HGRN_HEADS, HGRN_DIM, HGRN_DIM), F32)],
        compiler_params=_cparams(("parallel", "arbitrary")),
        name="hgrn2",
    )(ph, ph, ph, ph, hgrn_lb, onorm_g)


def _mla_prep_kernel(pm_ref, pos_ref, fr_ref, qg_ref, kvg_ref, wq_ref, wkv_ref, qn_ref, kn_ref,
                     q_ref, k_ref, v_ref):
    pm = pm_ref[...]
    ql = pm[:, :MLA_Q_RANK]
    kvl = pm[:, MLA_Q_RANK:MLA_Q_RANK + MLA_KV_RANK]
    kr = pm[:, MLA_Q_RANK + MLA_KV_RANK:]

    def rms(x, g):
        return x * lax.rsqrt(jnp.mean(x * x, axis=-1, keepdims=True) + NORM_EPS) * g

    qa = _dot(rms(ql, qg_ref[...]).astype(BF16), wq_ref[...])
    kva = _dot(rms(kvl, kvg_ref[...]).astype(BF16), wkv_ref[...])
    v_ref[...] = kva[:, MLA_HEADS * LANES:].astype(v_ref.dtype)

    ang = pos_ref[...] * fr_ref[...]
    cs = jnp.cos(ang)
    sn = jnp.sin(ang)
    lane = lax.broadcasted_iota(jnp.int32, ang.shape, 1)
    s1 = jnp.where(lane < MLA_NOPE + MLA_ROPE // 2, -sn, 0.0)
    s2 = jnp.where(lane >= MLA_NOPE + MLA_ROPE // 2, sn, 0.0)
    qn = qn_ref[...]
    kn = kn_ref[...]
    scale = 1.0 / math.sqrt(MLA_QK)
    half = MLA_ROPE // 2

    def head_norm_rope(xh, g):
        ss = jnp.sum(xh * xh, axis=-1, keepdims=True) * (1.0 / MLA_QK)
        xh = xh * lax.rsqrt(ss + NORM_EPS) * g
        return xh * cs + pltpu.roll(xh, LANES - half, 1) * s1 + pltpu.roll(xh, half, 1) * s2

    for h in range(MLA_HEADS):
        qh = qa[:, h * LANES:(h + 1) * LANES]
        q_ref[0, h] = (head_norm_rope(qh, qn) * scale).astype(q_ref.dtype)
        kh = kva[:, h * LANES:(h + 1) * LANES] + kr
        k_ref[0, h] = head_norm_rope(kh, kn).astype(k_ref.dtype)


def _mla_prep(pm, posf, freq_row, qlat_g, kvlat_g, wq, wkv, qn, kn, B, S, tr):
    N = pm.shape[0]
    nb = S // tr
    H = MLA_HEADS
    full = lambda a: pl.BlockSpec(a.shape, lambda i: (0,) * a.ndim)
    return pl.pallas_call(
        _mla_prep_kernel,
        grid=(N // tr,),
        in_specs=[pl.BlockSpec((tr, pm.shape[1]), lambda i: (i, 0)),
                  pl.BlockSpec((tr, 1), lambda i: (i, 0)),
                  full(freq_row), full(qlat_g), full(kvlat_g), full(wq), full(wkv), full(qn), full(kn)],
        out_specs=(pl.BlockSpec((1, H, tr, LANES), lambda i: (i // nb, 0, i % nb, 0)),
                   pl.BlockSpec((1, H, tr, LANES), lambda i: (i // nb, 0, i % nb, 0)),
                   pl.BlockSpec((tr, H * MLA_V), lambda i: (i, 0))),
        out_shape=(jax.ShapeDtypeStruct((B, H, S, LANES), BF16),
                   jax.ShapeDtypeStruct((B, H, S, LANES), BF16),
                   jax.ShapeDtypeStruct((N, H * MLA_V), BF16)),
        compiler_params=_cparams(("parallel",)),
        name="mla_prep",
    )(pm, posf, freq_row, qlat_g, kvlat_g, wq, wkv, qn, kn)


def _attn_kernel(q_ref, k_ref, v_ref, o_ref, m_ref, l_ref, acc_ref):
    qi = pl.program_id(2)
    ki = pl.program_id(3)
    T = q_ref.shape[2]

    @pl.when(ki == 0)
    def _():
        m_ref[...] = jnp.full_like(m_ref, NEG_BIG)
        l_ref[...] = jnp.zeros_like(l_ref)
        acc_ref[...] = jnp.zeros_like(acc_ref)

    def step(masked):
        vblk = v_ref[0]
        lane = lax.broadcasted_iota(jnp.int32, vblk.shape, 1)
        if masked:
            row = lax.broadcasted_iota(jnp.int32, (T, T), 0)
            col = lax.broadcasted_iota(jnp.int32, (T, T), 1)
            keep = col <= row
        for h in range(2):
            s = _dot_nt(q_ref[0, h], k_ref[0, h])
            if masked:
                s = jnp.where(keep, s, NEG_BIG)
            m_old = m_ref[h]
            m_new = jnp.maximum(m_old, jnp.max(s, axis=-1, keepdims=True))
            alpha = jnp.exp(m_old - m_new)
            p = jnp.exp(s - jnp.tile(m_new, (1, T // LANES)))
            l_ref[h] = alpha * l_ref[h] + jnp.sum(p, axis=-1, keepdims=True)
            vh = jnp.where((lane < MLA_V) if h == 0 else (lane >= MLA_V), vblk, jnp.zeros_like(vblk))
            acc_ref[h] = alpha * acc_ref[h] + _dot(p.astype(BF16), vh)
            m_ref[h] = m_new

    @pl.when(ki < qi)
    def _():
        step(False)

    @pl.when(ki == qi)
    def _():
        step(True)
        o_ref[0] = (acc_ref[0] / l_ref[0] + acc_ref[1] / l_ref[1]).astype(o_ref.dtype)


def _attention(q, k, v3, B, S, T):
    H = MLA_HEADS
    n = S // T
    return pl.pallas_call(
        _attn_kernel,
        grid=(B, H // 2, n, n),
        in_specs=[pl.BlockSpec((1, 2, T, LANES), lambda b, hp, qi, ki: (b, hp, qi, 0)),
                  pl.BlockSpec((1, 2, T, LANES), lambda b, hp, qi, ki: (b, hp, jnp.minimum(ki, qi), 0)),
                  pl.BlockSpec((1, T, LANES), lambda b, hp, qi, ki: (b, jnp.minimum(ki, qi), hp))],
        out_specs=pl.BlockSpec((1, T, LANES), lambda b, hp, qi, ki: (b, qi, hp)),
        out_shape=jax.ShapeDtypeStruct((B, S, H * MLA_V), BF16),
        scratch_shapes=[pltpu.VMEM((2, T, LANES), F32), pltpu.VMEM((2, T, LANES), F32),
                        pltpu.VMEM((2, T, LANES), F32)],
        compiler_params=_cparams(("parallel", "parallel", "parallel", "arbitrary")),
        name="mla_attention",
    )(q, k, v3)


def _pool_kernel(u_ref, w_ref, sc_ref, o_ref):
    S = u_ref.shape[1]
    row = lax.broadcasted_iota(jnp.int32, (S, LANES), 0)
    t1 = (row + 1).astype(F32)
    for gi, win in enumerate(POOL_WINDOWS):
        u = u_ref[0, :, gi * LANES:(gi + 1) * LANES]
        s = u
        k = 1
        while k < win:
            s = s + jnp.where(row >= k, pltpu.roll(s, k, 0), 0.0)
            k *= 2
        pooled = s / jnp.minimum(t1, float(win))
        mixed = _dot((pooled - u).astype(BF16), w_ref[gi].astype(BF16))
        o_ref[0, :, gi * LANES:(gi + 1) * LANES] = (
            mixed * sc_ref[:, gi * LANES:(gi + 1) * LANES]).astype(o_ref.dtype)


def _pool(pp3, w_pool, scale):
    B, S, W = pp3.shape
    return pl.pallas_call(
        _pool_kernel,
        grid=(B,),
        in_specs=[pl.BlockSpec((1, S, W), lambda b: (b, 0, 0)),
                  pl.BlockSpec(w_pool.shape, lambda b: (0, 0, 0)),
                  pl.BlockSpec((1, W), lambda b: (0, 0))],
        out_specs=pl.BlockSpec((1, S, W), lambda b: (b, 0, 0)),
        out_shape=jax.ShapeDtypeStruct((B, S, W), BF16),
        compiler_params=_cparams(("parallel",)),
        name="pool_mixer",
    )(pp3, w_pool, scale)


def _merge_kernel(x_ref, gt_ref, ya_ref, yb_ref, yc_ref, wa_ref, wb_ref, wc_ref, wo_ref, g1_ref, o_ref):
    D = x_ref.shape[1]
    merged = None
    for j, (y_ref, w_ref) in enumerate(((ya_ref, wa_ref), (yb_ref, wb_ref), (yc_ref, wc_ref))):
        y = _dot(y_ref[...], w_ref[...])
        gate = _sigmoid(gt_ref[:, j * D:(j + 1) * D].astype(F32))
        merged = gate * y if merged is None else merged + gate * y
    o_ref[...] = x_ref[...] + g1_ref[0] * _dot(merged.astype(BF16), wo_ref[...])


def _merge(x2, gates, ya, yb, yc, wa, wb, wc, wo, mod3, l, B, S, tr):
    N, D = x2.shape
    nb = S // tr
    base = l * B * 6
    full = lambda a: pl.BlockSpec(a.shape, lambda i: (0,) * a.ndim)
    rows = lambda a: pl.BlockSpec((tr, a.shape[1]), lambda i: (i, 0))
    return pl.pallas_call(
        _merge_kernel,
        grid=(N // tr,),
        in_specs=[rows(x2), rows(gates), rows(ya), rows(yb), rows(yc),
                  full(wa), full(wb), full(wc), full(wo),
                  pl.BlockSpec((1, 1, D), lambda i: (base + (i // nb) * 6 + 2, 0, 0))],
        out_specs=rows(x2),
        out_shape=jax.ShapeDtypeStruct((N, D), F32),
        compiler_params=_cparams(("parallel",)),
        name="merge_out",
    )(x2, gates, ya, yb, yc, wa, wb, wc, wo, mod3)


def _router_kernel(x_ref, g_ref, sh_ref, sc_ref, wr_ref, br_ref,
                   h_ref, idx_ref, wt_ref, rank_ref, cnt_ref, carry_ref):
    tr = x_ref.shape[0]

    @pl.when(pl.program_id(0) == 0)
    def _():
        carry_ref[...] = jnp.zeros_like(carry_ref)

    x = x_ref[...]
    ms = jnp.mean(x * x, axis=-1, keepdims=True)
    h = x * lax.rsqrt(ms + NORM_EPS) * g_ref[...]
    h = h * (1.0 + sc_ref[0]) + sh_ref[0]
    h_ref[...] = h
    h1, h2, _ = _split3(h)
    w1, w2, _ = _split3(wr_ref[...])
    logits = _dot(h1, w1) + (_dot(h1, w2) + _dot(h2, w1)) + br_ref[...]
    lane = lax.broadcasted_iota(jnp.int32, logits.shape, 1)
    work = logits
    vals, ids = [], []
    for _k in range(TOP_K):
        m = jnp.max(work, axis=-1, keepdims=True)
        sel = jnp.min(jnp.where(work == m, lane, LANES), axis=-1, keepdims=True)
        vals.append(m)
        ids.append(sel)
        work = jnp.where(lane == sel, -jnp.inf, work)
    es = [jnp.exp(v - vals[0]) for v in vals]
    den = es[0] + es[1] + es[2] + es[3]
    onehot = jnp.zeros(logits.shape, F32)
    idx_out = jnp.zeros(logits.shape, jnp.int32)
    wt_out = jnp.zeros(logits.shape, F32)
    for k in range(TOP_K):
        onehot = onehot + (lane == ids[k]).astype(F32)
        idx_out = jnp.where(lane == k, ids[k], idx_out)
        wt_out = jnp.where(lane == k, es[k] / den, wt_out)
    idx_ref[...] = idx_out
    wt_ref[...] = wt_out
    r = lax.broadcasted_iota(jnp.int32, (tr, tr), 0)
    c = lax.broadcasted_iota(jnp.int32, (tr, tr), 1)
    strict = (c < r).astype(BF16)
    before = _dot(strict, onehot.astype(BF16)) + carry_ref[...]
    rank_out = jnp.zeros(logits.shape, jnp.int32)
    for k in range(TOP_K):
        rk = jnp.sum(jnp.where(lane == ids[k], before, 0.0), axis=-1, keepdims=True)
        rank_out = jnp.where(lane == k, rk.astype(jnp.int32), rank_out)
    rank_ref[...] = rank_out
    carry_ref[...] = carry_ref[...] + jnp.sum(onehot, axis=0, keepdims=True)
    cnt_ref[...] = jnp.broadcast_to(carry_ref[...], cnt_ref.shape)


def _router(x2, g, mod3, l, wr_p, br_p, B, S, tr):
    N, D = x2.shape
    nb = S // tr
    base = l * B * 6
    rows = lambda w: pl.BlockSpec((tr, w), lambda i: (i, 0))

    def modspec(j):
        return pl.BlockSpec((1, 1, D), lambda i: (base + (i // nb) * 6 + j, 0, 0))

    return pl.pallas_call(
        _router_kernel,
        grid=(N // tr,),
        in_specs=[rows(D), pl.BlockSpec((1, D), lambda i: (0, 0)), modspec(3), modspec(4),
                  pl.BlockSpec(wr_p.shape, lambda i: (0, 0)), pl.BlockSpec((1, LANES), lambda i: (0, 0))],
        out_specs=(rows(D), rows(LANES), rows(LANES), rows(LANES),
                   pl.BlockSpec((8, LANES), lambda i: (0, 0))),
        out_shape=(jax.ShapeDtypeStruct((N, D), F32),
                   jax.ShapeDtypeStruct((N, LANES), jnp.int32),
                   jax.ShapeDtypeStruct((N, LANES), F32),
                   jax.ShapeDtypeStruct((N, LANES), jnp.int32),
                   jax.ShapeDtypeStruct((8, LANES), F32)),
        scratch_shapes=[pltpu.VMEM((1, LANES), F32)],
        compiler_params=_cparams(("arbitrary",)),
        name="router",
    )(x2, g, mod3, mod3, wr_p, br_p)


def _dispatch_kernel(slot_ref, zflag_ref, h_ref, xs_hbm, zbuf, sem):
    i = pl.program_id(0)
    tn = h_ref.shape[0]
    tm = zbuf.shape[0]
    n_tiles = xs_hbm.shape[0] // tm

    @pl.when(i == 0)
    def _():
        zbuf[...] = jnp.zeros_like(zbuf)

        def zero_copy(t):
            return pltpu.make_async_copy(zbuf, xs_hbm.at[pl.ds(pl.multiple_of(t * tm, tm), tm)], sem)

        def zstart(t, c):
            @pl.when(zflag_ref[t] > 0)
            def _():
                zero_copy(t).start()
            return c

        def zwait(t, c):
            @pl.when(zflag_ref[t] > 0)
            def _():
                zero_copy(t).wait()
            return c

        lax.fori_loop(0, n_tiles, zstart, 0)
        lax.fori_loop(0, n_tiles, zwait, 0)

    base = i * tn * TOP_K

    def issue(r, c):
        for k in range(TOP_K):
            s = slot_ref[base + r * TOP_K + k]
            pltpu.make_async_copy(h_ref.at[pl.ds(r, 1)], xs_hbm.at[pl.ds(s, 1)], sem).start()
        return c

    lax.fori_loop(0, tn, issue, 0)
    for k in range(TOP_K):
        pltpu.make_async_copy(h_ref, xs_hbm.at[pl.ds(0, tn)], sem).wait()


def _dispatch(slots_flat, zero_flags, h2, n_rows, tn, tm):
    N, D = h2.shape
    grid_spec = pltpu.PrefetchScalarGridSpec(
        num_scalar_prefetch=2,
        grid=(N // tn,),
        in_specs=[pl.BlockSpec((tn, D), lambda i, sl, zf: (i, 0))],
        out_specs=pl.BlockSpec(memory_space=pl.ANY),
        scratch_shapes=[pltpu.VMEM((tm, D), F32), pltpu.SemaphoreType.DMA(())],
    )
    return pl.pallas_call(
        _dispatch_kernel,
        grid_spec=grid_spec,
        out_shape=jax.ShapeDtypeStruct((n_rows, D), F32),
        compiler_params=_cparams(("arbitrary",)),
        name="moe_dispatch",
    )(slots_flat, zero_flags, h2)


def _expert_kernel(te_ref, nu_ref, x_ref, w1_ref, b1_ref, w2_ref, b2_ref, o_ref, w1b, w2b):
    t = pl.program_id(0)
    F = w2_ref.shape[1]

    @pl.when(t < nu_ref[0])
    def _():
        @pl.when(jnp.logical_or(t == 0, te_ref[t] != te_ref[jnp.maximum(t - 1, 0)]))
        def _():
            w1b[...] = w1_ref[0].astype(BF16)
            w2b[...] = w2_ref[0].astype(BF16)

        x = x_ref[...].astype(BF16)
        a = _dot(x, w1b[...]) + b1_ref[0]
        glu = jnp.minimum(a[:, :F], SWIGLU_LIMIT)
        lin = jnp.clip(a[:, F:], -SWIGLU_LIMIT, SWIGLU_LIMIT)
        act = glu * _sigmoid(SWIGLU_ALPHA * glu) * (lin + 1.0)
        o_ref[...] = _dot(act.astype(BF16), w2b[...]) + b2_ref[0]

    @pl.when(t >= nu_ref[0])
    def _():
        o_ref[...] = jnp.zeros_like(o_ref)


def _experts(tile_expert, n_used, xs, w1_all, b1_all, w2_all, b2_all, l, tm):
    n_tiles = tile_expert.shape[0]
    D = xs.shape[1]
    L, E, _, F2 = w1_all.shape
    F = w2_all.shape[2]
    w1 = w1_all.reshape(L * E, D, F2)
    w2 = w2_all.reshape(L * E, F, D)
    b1 = b1_all.reshape(L * E, F2)
    b2 = b2_all.reshape(L * E, D)
    E0 = l * E
    E = L * E
    grid_spec = pltpu.PrefetchScalarGridSpec(
        num_scalar_prefetch=2,
        grid=(n_tiles,),
        in_specs=[pl.BlockSpec((tm, D), lambda t, te, nu: (jnp.minimum(t, nu[0] - 1), 0)),
                  pl.BlockSpec((1, D, F2), lambda t, te, nu: (E0 + te[t], 0, 0)),
                  pl.BlockSpec((1, 1, F2), lambda t, te, nu: (E0 + te[t], 0, 0)),
                  pl.BlockSpec((1, F, D), lambda t, te, nu: (E0 + te[t], 0, 0)),
                  pl.BlockSpec((1, 1, D), lambda t, te, nu: (E0 + te[t], 0, 0))],
        out_specs=pl.BlockSpec((tm, D), lambda t, te, nu: (t, 0)),
        scratch_shapes=[pltpu.VMEM((D, F2), BF16), pltpu.VMEM((F, D), BF16)],
    )
    return pl.pallas_call(
        _expert_kernel,
        grid_spec=grid_spec,
        out_shape=jax.ShapeDtypeStruct((n_tiles * tm, D), F32),
        compiler_params=_cparams(("arbitrary",)),
        name="expert_mlp",
    )(tile_expert, n_used, xs, w1, b1.reshape(E, 1, F2), w2, b2.reshape(E, 1, D))


def _combine_kernel(slot_ref, ys_hbm, x_ref, wt_ref, g2_ref, o_ref, buf, sem):
    i = pl.program_id(0)
    tn = x_ref.shape[0]
    base = i * tn * TOP_K

    def issue(r, c):
        for k in range(TOP_K):
            s = slot_ref[base + r * TOP_K + k]
            pltpu.make_async_copy(ys_hbm.at[pl.ds(s, 1)], buf.at[k, pl.ds(r, 1)], sem).start()
        return c

    lax.fori_loop(0, tn, issue, 0)
    for k in range(TOP_K):
        pltpu.make_async_copy(ys_hbm.at[pl.ds(0, tn)], buf.at[k], sem).wait()
    wt = wt_ref[...]
    y = wt[:, 0:1] * buf[0]
    for k in range(1, TOP_K):
        y = y + wt[:, k:k + 1] * buf[k]
    o_ref[...] = x_ref[...] + g2_ref[0] * y


def _combine(slots_flat, ys, x2, wts, mod3, l, B, S, tn):
    N, D = x2.shape
    nb = S // tn
    base = l * B * 6
    grid_spec = pltpu.PrefetchScalarGridSpec(
        num_scalar_prefetch=1,
        grid=(N // tn,),
        in_specs=[pl.BlockSpec(memory_space=pl.ANY),
                  pl.BlockSpec((tn, D), lambda i, sl: (i, 0)),
                  pl.BlockSpec((tn, LANES), lambda i, sl: (i, 0)),
                  pl.BlockSpec((1, 1, D), lambda i, sl: (base + (i // nb) * 6 + 5, 0, 0))],
        out_specs=pl.BlockSpec((tn, D), lambda i, sl: (i, 0)),
        scratch_shapes=[pltpu.VMEM((TOP_K, tn, D), F32), pltpu.SemaphoreType.DMA(())],
    )
    return pl.pallas_call(
        _combine_kernel,
        grid_spec=grid_spec,
        out_shape=jax.ShapeDtypeStruct((N, D), F32),
        compiler_params=_cparams(("arbitrary",)),
        name="moe_combine",
    )(slots_flat, ys, x2, wts, mod3)


def _tile_rows(S, want):
    t = min(S, want)
    while S % t:
        t //= 2
    return t


def _prep_in_weights(w_in_l):
    HW = HGRN_HEADS * HGRN_DIM
    o = 0
    wh = w_in_l[:, o:o + 4 * HW]; o += 4 * HW
    wql = w_in_l[:, o:o + MLA_Q_RANK]; o += MLA_Q_RANK
    wkvl = w_in_l[:, o:o + MLA_KV_RANK]; o += MLA_KV_RANK
    wkr = w_in_l[:, o:o + MLA_ROPE]; o += MLA_ROPE
    wp = w_in_l[:, o:o + HW]; o += HW
    wg = w_in_l[:, o:]
    D = w_in_l.shape[0]
    wkr_p = jnp.zeros((D, LANES), w_in_l.dtype).at[:, MLA_NOPE:MLA_NOPE + MLA_ROPE].set(wkr)
    wm = jnp.concatenate([wql, wkvl, wkr_p], axis=1)
    return wg.astype(BF16), wh.astype(BF16), wm.astype(BF16), wp.astype(BF16)


def _pad_heads(w, n_heads, width):
    K = w.shape[0]
    w3 = w.reshape(K, n_heads, width)
    return jnp.pad(w3, ((0, 0), (0, 0), (0, LANES - width))).reshape(K, n_heads * LANES)


def kernel(x, c, positions, ada_w, ada_b, norm1_g, norm2_g, w_in, hgrn_lb, hgrn_onorm_g, mla_qlat_g, mla_kvlat_g, w_uq, w_ukv, q_norm_g, k_norm_g, w_pool, pool_scale, w_br_a, w_br_b, w_br_c, w_out, w_router, b_router, w_exp1, b_exp1, w_exp2, b_exp2):
    B, S, D = x.shape
    L = ada_w.shape[0]
    N = B * S
    H = MLA_HEADS
    tr = _tile_rows(S, 512)
    tc = _tile_rows(S, 256)
    ta = _tile_rows(S, 512)
    tm = 512
    tn = _tile_rows(S, 256)

    mod = _ada(c, ada_w, ada_b)
    mod3 = mod.reshape(L * B * 6, 1, D)

    inv_freq = 1.0 / (ROPE_THETA ** (jnp.arange(0, MLA_ROPE, 2, dtype=F32) / MLA_ROPE))
    freq_row = jnp.zeros((1, LANES), F32).at[0, MLA_NOPE:MLA_NOPE + MLA_ROPE].set(
        jnp.concatenate([inv_freq, inv_freq]))
    posf = positions.astype(F32).reshape(N, 1)

    n_tiles = (N * TOP_K) // tm + N_EXPERTS
    x2 = x.reshape(N, D)
    for l in range(L):
        wg, wh, wm, wp = _prep_in_weights(w_in[l])
        pg, ph, pm, pp = _inproj(x2, norm1_g[l].reshape(1, D), mod3, l, wg, wh, wm, wp, B, S, tr)

        ya = _hgrn(ph.reshape(B, S, -1), hgrn_lb, hgrn_onorm_g[l].reshape(1, HGRN_DIM), l, B, S, tc)

        wq_p = _pad_heads(w_uq[l], H, MLA_QK).astype(BF16)
        wkv3 = w_ukv[l].reshape(MLA_KV_RANK, H, MLA_NOPE + MLA_V)
        wkv_p = jnp.concatenate(
            [_pad_heads(wkv3[:, :, :MLA_NOPE].reshape(MLA_KV_RANK, H * MLA_NOPE), H, MLA_NOPE),
             wkv3[:, :, MLA_NOPE:].reshape(MLA_KV_RANK, H * MLA_V)], axis=1).astype(BF16)
        qn_p = jnp.pad(q_norm_g[l], (0, LANES - MLA_QK)).reshape(1, LANES)
        kn_p = jnp.pad(k_norm_g[l], (0, LANES - MLA_QK)).reshape(1, LANES)
        q, k, v = _mla_prep(pm, posf, freq_row, mla_qlat_g[l].reshape(1, -1), mla_kvlat_g[l].reshape(1, -1),
                            wq_p, wkv_p, qn_p, kn_p, B, S, tr)
        yb = _attention(q, k, v.reshape(B, S, H * MLA_V), B, S, ta)

        yc = _pool(pp.reshape(B, S, -1), w_pool[l], pool_scale[l].reshape(1, -1))

        x2 = _merge(x2, pg, ya.reshape(N, -1), yb.reshape(N, -1), yc.reshape(N, -1),
                    w_br_a[l].astype(BF16), w_br_b[l].astype(BF16), w_br_c[l].astype(BF16),
                    w_out[l].astype(BF16), mod3, l, B, S, tr)

        wr_p = jnp.pad(w_router[l], ((0, 0), (0, LANES - N_EXPERTS)))
        br_p = jnp.concatenate([b_router[l], jnp.full((LANES - N_EXPERTS,), NEG_BIG, F32)]).reshape(1, LANES)
        h2, idx, wts, rank, cnt = _router(x2, norm2_g[l].reshape(1, D), mod3, l, wr_p, br_p, B, S, tr)

        counts = cnt[0, :N_EXPERTS].astype(jnp.int32)
        tiles_per = (counts + tm - 1) // tm
        tile_end = jnp.cumsum(tiles_per)
        offsets = (tile_end - tiles_per) * tm
        n_used = tile_end[-1:]
        tile_ids = jnp.minimum(jnp.arange(n_tiles, dtype=jnp.int32), n_used[0] - 1)
        tile_expert = jnp.minimum(
            jnp.sum((tile_end[None, :] <= tile_ids[:, None]).astype(jnp.int32), axis=1), N_EXPERTS - 1)
        idx4 = idx[:, :TOP_K]
        e_ids = jnp.arange(N_EXPERTS, dtype=jnp.int32)
        slots = rank[:, :TOP_K] + jnp.sum(
            jnp.where(idx4[:, :, None] == e_ids[None, None, :], offsets[None, None, :], 0), axis=-1)
        slots_flat = slots.reshape(-1).astype(jnp.int32)
        all_tiles = jnp.arange(n_tiles, dtype=jnp.int32)
        is_tail = jnp.any((tile_end[None, :] - 1 == all_tiles[:, None]) & (tiles_per[None, :] > 0), axis=1)
        zero_flags = (is_tail | (all_tiles >= n_used[0])).astype(jnp.int32)

        xs = _dispatch(slots_flat, zero_flags, h2, n_tiles * tm, tn, tm)
        ys = _experts(tile_expert.astype(jnp.int32), n_used.astype(jnp.int32), xs,
                      w_exp1, b_exp1, w_exp2, b_exp2, l, tm)
        x2 = _combine(slots_flat, ys, x2, wts, mod3, l, B, S, tn)
    return x2.reshape(B, S, D)
```

```python
import functools
import math

import jax
import jax.numpy as jnp
from jax import lax
from jax.experimental import pallas as pl
from jax.experimental.pallas import tpu as pltpu

F32 = jnp.float32
BF16 = jnp.bfloat16

LANES = 128
NORM_EPS = 1e-6
HGRN_HEADS = 4
HGRN_DIM = 128
HGRN_CHUNK = 64
HGRN_SUB = 16
MLA_HEADS = 8
MLA_NOPE = 64
MLA_ROPE = 32
MLA_QK = MLA_NOPE + MLA_ROPE
MLA_V = 64
MLA_Q_RANK = 384
MLA_KV_RANK = 256
ROPE_THETA = 10000.0
POOL_WINDOWS = (2, 4, 8, 16)
N_EXPERTS = 32
TOP_K = 4
SWIGLU_LIMIT = 7.0
SWIGLU_ALPHA = 1.702
NEG_BIG = -1e30
LOG2E = 1.4426950408889634

VMEM_LIMIT = 56 * 1024 * 1024


def _cparams(sem):
    return pltpu.CompilerParams(dimension_semantics=sem, vmem_limit_bytes=VMEM_LIMIT)


def _dot(a, b):
    return jnp.dot(a, b, preferred_element_type=F32)


def _dot_nt(a, b):
    return lax.dot_general(a, b, (((1,), (1,)), ((), ())), preferred_element_type=F32)


def _dot_tn(a, b):
    return lax.dot_general(a, b, (((0,), (0,)), ((), ())), preferred_element_type=F32)


def _split3(x):
    hi = x.astype(BF16)
    r = x - hi.astype(F32)
    mid = r.astype(BF16)
    lo = (r - mid.astype(F32)).astype(BF16)
    return hi, mid, lo


def _sigmoid(x):
    return 1.0 / (1.0 + jnp.exp(-x))


def _ada_kernel(c_ref, w_ref, b_ref, o_ref):
    c = c_ref[...]
    cond = (c * _sigmoid(c)).astype(BF16)
    o_ref[0] = _dot(cond, w_ref[0].astype(BF16)) + b_ref[0]


def _ada(c, ada_w, ada_b):
    L, D, W = ada_w.shape
    B = c.shape[0]
    tn = D
    return pl.pallas_call(
        _ada_kernel,
        grid=(L, W // tn),
        in_specs=[
            pl.BlockSpec((B, D), lambda l, j: (0, 0)),
            pl.BlockSpec((1, D, tn), lambda l, j: (l, 0, j)),
            pl.BlockSpec((1, 1, tn), lambda l, j: (l, 0, j)),
        ],
        out_specs=pl.BlockSpec((1, B, tn), lambda l, j: (l, 0, j)),
        out_shape=jax.ShapeDtypeStruct((L, B, W), F32),
        compiler_params=_cparams(("parallel", "parallel")),
        name="ada_mod",
    )(c, ada_w, ada_b.reshape(L, 1, W))


def _inproj_kernel(x_ref, g_ref, sh_ref, sc_ref, wg_ref, wh_ref, wm_ref, wp_ref,
                   og_ref, oh_ref, om_ref, op_ref):
    x = x_ref[...]
    ms = jnp.mean(x * x, axis=-1, keepdims=True)
    h = x * lax.rsqrt(ms + NORM_EPS) * g_ref[...]
    h = h * (1.0 + sc_ref[0]) + sh_ref[0]
    hb = h.astype(BF16)
    cw = 512
    for w_ref, o_ref in ((wg_ref, og_ref), (wh_ref, oh_ref), (wm_ref, om_ref), (wp_ref, op_ref)):
        width = w_ref.shape[1]
        step = cw if width % cw == 0 else width
        for j in range(0, width, step):
            o_ref[:, j:j + step] = _dot(hb, w_ref[:, j:j + step]).astype(o_ref.dtype)


def _inproj(x2, g, mod3, l, wg, wh, wm, wp, B, S, tr):
    N, D = x2.shape
    nb = S // tr
    base = l * B * 6

    def modspec(j):
        return pl.BlockSpec((1, 1, D), lambda i: (base + (i // nb) * 6 + j, 0, 0))

    def wspec(w):
        return pl.BlockSpec(w.shape, lambda i: (0, 0), pipeline_mode=pl.Buffered(1))

    outs = (
        jax.ShapeDtypeStruct((N, wg.shape[1]), BF16),
        jax.ShapeDtypeStruct((N, wh.shape[1]), F32),
        jax.ShapeDtypeStruct((N, wm.shape[1]), F32),
        jax.ShapeDtypeStruct((N, wp.shape[1]), F32),
    )
    return pl.pallas_call(
        _inproj_kernel,
        grid=(N // tr,),
        in_specs=[
            pl.BlockSpec((tr, D), lambda i: (i, 0)),
            pl.BlockSpec((1, D), lambda i: (0, 0)),
            modspec(0), modspec(1),
            wspec(wg), wspec(wh), wspec(wm), wspec(wp),
        ],
        out_specs=tuple(pl.BlockSpec((tr, o.shape[1]), lambda i: (i, 0)) for o in outs),
        out_shape=outs,
        compiler_params=_cparams(("parallel",)),
        name="in_proj",
    )(x2, g, mod3, mod3, wg, wh, wm, wp)


def _hgrn_kernel(layer, n_chunks, q_ref, f_ref, i_ref, g_ref, lb_ref, on_ref, o_ref, st_ref):
    C, R, DH = HGRN_CHUNK, HGRN_SUB, HGRN_DIM

    @pl.when(pl.program_id(1) == 0)
    def _():
        st_ref[...] = jnp.zeros_like(st_ref)

    lbr = lb_ref[...]
    e = jnp.exp(lbr - jnp.max(lbr, axis=0, keepdims=True))
    p = e / jnp.sum(e, axis=0, keepdims=True)
    lb_all = jnp.zeros((1, lbr.shape[1]), F32)
    for j in range(1, layer + 1):
        lb_all = lb_all + p[j:j + 1]
    onorm = on_ref[...]

    row_c = lax.broadcasted_iota(jnp.int32, (C, C), 0)
    col_c = lax.broadcasted_iota(jnp.int32, (C, C), 1)
    tri = (col_c <= row_c).astype(BF16)
    row_r = lax.broadcasted_iota(jnp.int32, (R, 1), 0)

    def head_chunk(sl, h):
        hs = slice(h * DH, (h + 1) * DH)
        lb = lb_all[:, hs]
        log_lb = jnp.log(lb)
        log_1mlb = jnp.log1p(-lb)
        q = q_ref[0, sl, hs]
        z = f_ref[0, sl, hs]
        v = i_ref[0, sl, hs]
        g = g_ref[0, sl, hs]
        ez = jnp.exp(-jnp.abs(z))
        log_sig = jnp.minimum(z, 0.0) - jnp.log1p(ez)
        a = log_1mlb + log_sig
        m = jnp.maximum(a, log_lb)
        lf = m + jnp.log(jnp.exp(a - m) + jnp.exp(log_lb - m))
        kk = (1.0 - lb) * jnp.where(z >= 0.0, ez, 1.0) / (1.0 + ez)
        l1, l2, l3 = _split3(lf)
        b = (_dot(tri, l1) + _dot(tri, l2) + _dot(tri, l3)) * LOG2E
        st = st_ref[h]
        qs = (q * jnp.exp2(b)).astype(BF16)
        o = _dot_nt(qs, st.astype(BF16))
        vb = v.astype(BF16)
        outs = []
        for i in range(C // R):
            r0 = i * R
            bi = b[r0:r0 + R]
            qi = q[r0:r0 + R]
            ki = kk[r0:r0 + R]
            vi = v[r0:r0 + R]
            oi = o[r0:r0 + R]
            if i > 0:
                ref = b[r0 - 1:r0]
                qt = (qi * jnp.exp2(bi - ref)).astype(BF16)
                kt = (kk[:r0] * jnp.exp2(ref - b[:r0])).astype(BF16)
                att = _dot_nt(qt, kt)
                oi = oi + _dot(att.astype(BF16), vb[:r0])
            for s in range(R):
                es = jnp.exp2(jnp.minimum(bi - bi[s:s + 1], 0.0))
                col = jnp.sum(qi * es * ki[s:s + 1], axis=-1, keepdims=True)
                col = jnp.where(row_r >= s, col, 0.0)
                oi = oi + col * vi[s:s + 1]
            outs.append(oi)
        o = jnp.concatenate(outs, axis=0)
        bl = b[C - 1:C]
        kd = (kk * jnp.exp2(bl - b)).astype(BF16)
        st_ref[h] = st * jnp.exp2(bl) + _dot_tn(vb, kd)
        ms = jnp.mean(o * o, axis=-1, keepdims=True)
        y = o * lax.rsqrt(ms + NORM_EPS) * onorm
        y = y * (g * _sigmoid(g))
        o_ref[0, sl, hs] = y.astype(o_ref.dtype)

    def chunk(ci, carry):
        sl = pl.ds(pl.multiple_of(ci * C, C), C)
        for h in range(HGRN_HEADS):
            head_chunk(sl, h)
        return carry

    lax.fori_loop(0, n_chunks, chunk, 0)


def _hgrn(ph, hgrn_lb, onorm_g, layer, B, S, tc):
    L = hgrn_lb.shape[0]
    W = HGRN_HEADS * HGRN_DIM

    def spec(off):
        return pl.BlockSpec((1, tc, W), lambda b, t: (b, t, off))

    return pl.pallas_call(
        functools.partial(_hgrn_kernel, layer, tc // HGRN_CHUNK),
        grid=(B, S // tc),
        in_specs=[spec(0), spec(1), spec(2), spec(3),
                  pl.BlockSpec((L, W), lambda b, t: (0, 0)),
                  pl.BlockSpec((1, HGRN_DIM), lambda b, t: (0, 0))],
        out_specs=pl.BlockSpec((1, tc, W), lambda b, t: (b, t, 0)),
        out_shape=jax.ShapeDtypeStruct((B, S, W), BF16),
        scratch_shapes=[pltpu.VMEM((HGRN_HEADS, HGRN_DIM, HGRN_DIM), F32)],
        compiler_params=_cparams(("parallel", "arbitrary")),
        name="hgrn2",
    )(ph, ph, ph, ph, hgrn_lb, onorm_g)


def _rope_table_kernel(pos_ref, fr_ref, cos_ref, sin_ref):
    ang = pos_ref[...] * fr_ref[...]
    lane = lax.broadcasted_iota(jnp.int32, ang.shape, 1)
    sn = jnp.sin(ang)
    cos_ref[...] = jnp.cos(ang)
    sin_ref[...] = jnp.where(lane < MLA_NOPE + MLA_ROPE // 2, -sn, sn)


def _rope_tables(posf, freq_row, tr):
    N = posf.shape[0]
    out = jax.ShapeDtypeStruct((N, LANES), F32)
    return pl.pallas_call(
        _rope_table_kernel,
        grid=(N // tr,),
        in_specs=[pl.BlockSpec((tr, 1), lambda i: (i, 0)), pl.BlockSpec((1, LANES), lambda i: (0, 0))],
        out_specs=(pl.BlockSpec((tr, LANES), lambda i: (i, 0)), pl.BlockSpec((tr, LANES), lambda i: (i, 0))),
        out_shape=(out, out),
        compiler_params=_cparams(("parallel",)),
        name="rope_tables",
    )(posf, freq_row)


def _mla_prep_kernel(pm_ref, cos_ref, sin_ref, qg_ref, kvg_ref, wq_ref, wqr_ref, wkv_ref,
                     qn_ref, qnr_ref, kn_ref, knr_ref, q_ref, k_ref, v_ref):
    pm = pm_ref[...]
    o = 0
    ql = pm[:, o:o + MLA_Q_RANK]; o += MLA_Q_RANK
    kvl = pm[:, o:o + MLA_KV_RANK]; o += MLA_KV_RANK
    kr = pm[:, o:o + LANES]; o += LANES
    krr = pm[:, o:o + LANES]

    def rms(x, g):
        return x * lax.rsqrt(jnp.mean(x * x, axis=-1, keepdims=True) + NORM_EPS) * g

    qlb = rms(ql, qg_ref[...]).astype(BF16)
    qa = _dot(qlb, wq_ref[...])
    qr = _dot(qlb, wqr_ref[...])
    kva = _dot(rms(kvl, kvg_ref[...]).astype(BF16), wkv_ref[...])
    v_ref[...] = kva[:, MLA_HEADS * LANES:].astype(v_ref.dtype)

    cs = cos_ref[...]
    sn = sin_ref[...]
    scale = 1.0 / math.sqrt(MLA_QK)
    q_cos = qn_ref[...] * cs * scale
    q_sin = qnr_ref[...] * sn * scale
    k_cos = kn_ref[...] * cs
    k_sin = knr_ref[...] * sn
    k_rot = krr * k_sin

    def inv_rms(xh):
        return lax.rsqrt(jnp.sum(xh * xh, axis=-1, keepdims=True) * (1.0 / MLA_QK) + NORM_EPS)

    for h in range(MLA_HEADS):
        hs = slice(h * LANES, (h + 1) * LANES)
        qh = qa[:, hs]
        q_ref[0, h] = ((qh * q_cos + qr[:, hs] * q_sin) * inv_rms(qh)).astype(q_ref.dtype)
        kh = kva[:, hs] + kr
        k_ref[0, h] = ((kh * k_cos + k_rot) * inv_rms(kh)).astype(k_ref.dtype)


def _mla_prep(pm, cos_t, sin_t, qlat_g, kvlat_g, wq, wqr, wkv, qn, qnr, kn, knr, B, S, tr):
    N = pm.shape[0]
    nb = S // tr
    H = MLA_HEADS
    full = lambda a: pl.BlockSpec(a.shape, lambda i: (0,) * a.ndim)
    return pl.pallas_call(
        _mla_prep_kernel,
        grid=(N // tr,),
        in_specs=[pl.BlockSpec((tr, pm.shape[1]), lambda i: (i, 0)),
                  pl.BlockSpec((tr, LANES), lambda i: (i, 0)),
                  pl.BlockSpec((tr, LANES), lambda i: (i, 0)),
                  full(qlat_g), full(kvlat_g), full(wq), full(wqr), full(wkv),
                  full(qn), full(qnr), full(kn), full(knr)],
        out_specs=(pl.BlockSpec((1, H, tr, LANES), lambda i: (i // nb, 0, i % nb, 0)),
                   pl.BlockSpec((1, H, tr, LANES), lambda i: (i // nb, 0, i % nb, 0)),
                   pl.BlockSpec((tr, H * MLA_V), lambda i: (i, 0))),
        out_shape=(jax.ShapeDtypeStruct((B, H, S, LANES), BF16),
                   jax.ShapeDtypeStruct((B, H, S, LANES), BF16),
                   jax.ShapeDtypeStruct((N, H * MLA_V), BF16)),
        compiler_params=_cparams(("parallel",)),
        name="mla_prep",
    )(pm, cos_t, sin_t, qlat_g, kvlat_g, wq, wqr, wkv, qn, qnr, kn, knr)


def _attn_kernel(q_ref, k_ref, v_ref, o_ref, m_ref, l_ref, acc_ref):
    qi = pl.program_id(2)
    T = q_ref.shape[2]

    m_ref[...] = jnp.full_like(m_ref, NEG_BIG)
    l_ref[...] = jnp.zeros_like(l_ref)
    acc_ref[...] = jnp.zeros_like(acc_ref)

    def step(ki, masked):
        rows = pl.ds(pl.multiple_of(ki * T, T), T)
        vblk = v_ref[0, rows, :]
        lane = lax.broadcasted_iota(jnp.int32, vblk.shape, 1)
        if masked:
            row = lax.broadcasted_iota(jnp.int32, (T, T), 0)
            col = lax.broadcasted_iota(jnp.int32, (T, T), 1)
            keep = col <= row
        for h in range(2):
            s = _dot_nt(q_ref[0, h], k_ref[0, h, rows, :])
            if masked:
                s = jnp.where(keep, s, NEG_BIG)
            m_old = m_ref[h]
            m_new = jnp.maximum(m_old, jnp.max(s, axis=-1, keepdims=True))
            alpha = jnp.exp(m_old - m_new)
            p = jnp.exp(s - jnp.tile(m_new, (1, T // LANES)))
            l_ref[h] = alpha * l_ref[h] + jnp.sum(p, axis=-1, keepdims=True)
            vh = jnp.where((lane < MLA_V) if h == 0 else (lane >= MLA_V), vblk, jnp.zeros_like(vblk))
            acc_ref[h] = alpha * acc_ref[h] + _dot(p.astype(BF16), vh)
            m_ref[h] = m_new

    def below_diagonal(ki, c):
        step(ki, False)
        return c

    lax.fori_loop(0, qi, below_diagonal, 0)
    step(qi, True)
    o_ref[0] = (acc_ref[0] / l_ref[0] + acc_ref[1] / l_ref[1]).astype(o_ref.dtype)


def _attention(q, k, v3, B, S, T):
    H = MLA_HEADS
    n = S // T
    return pl.pallas_call(
        _attn_kernel,
        grid=(B, H // 2, n),
        in_specs=[pl.BlockSpec((1, 2, T, LANES), lambda b, hp, qi: (b, hp, qi, 0)),
                  pl.BlockSpec((1, 2, S, LANES), lambda b, hp, qi: (b, hp, 0, 0)),
                  pl.BlockSpec((1, S, LANES), lambda b, hp, qi: (b, 0, hp))],
        out_specs=pl.BlockSpec((1, T, LANES), lambda b, hp, qi: (b, qi, hp)),
        out_shape=jax.ShapeDtypeStruct((B, S, H * MLA_V), BF16),
        scratch_shapes=[pltpu.VMEM((2, T, LANES), F32), pltpu.VMEM((2, T, LANES), F32),
                        pltpu.VMEM((2, T, LANES), F32)],
        compiler_params=_cparams(("parallel", "parallel", "arbitrary")),
        name="mla_attention",
    )(q, k, v3)


def _pool_kernel(u_ref, w_ref, sc_ref, o_ref):
    S = u_ref.shape[1]
    row = lax.broadcasted_iota(jnp.int32, (S, LANES), 0)
    t1 = (row + 1).astype(F32)
    for gi, win in enumerate(POOL_WINDOWS):
        u = u_ref[0, :, gi * LANES:(gi + 1) * LANES]
        s = u
        k = 1
        while k < win:
            s = s + jnp.where(row >= k, pltpu.roll(s, k, 0), 0.0)
            k *= 2
        pooled = s / jnp.minimum(t1, float(win))
        mixed = _dot((pooled - u).astype(BF16), w_ref[gi].astype(BF16))
        o_ref[0, :, gi * LANES:(gi + 1) * LANES] = (
            mixed * sc_ref[:, gi * LANES:(gi + 1) * LANES]).astype(o_ref.dtype)


def _pool(pp3, w_pool, scale):
    B, S, W = pp3.shape
    return pl.pallas_call(
        _pool_kernel,
        grid=(B,),
        in_specs=[pl.BlockSpec((1, S, W), lambda b: (b, 0, 0)),
                  pl.BlockSpec(w_pool.shape, lambda b: (0, 0, 0)),
                  pl.BlockSpec((1, W), lambda b: (0, 0))],
        out_specs=pl.BlockSpec((1, S, W), lambda b: (b, 0, 0)),
        out_shape=jax.ShapeDtypeStruct((B, S, W), BF16),
        compiler_params=_cparams(("parallel",)),
        name="pool_mixer",
    )(pp3, w_pool, scale)


def _merge_kernel(x_ref, gt_ref, ya_ref, yb_ref, yc_ref, wa_ref, wb_ref, wc_ref, wo_ref, g1_ref, o_ref):
    D = x_ref.shape[1]
    merged = None
    for j, (y_ref, w_ref) in enumerate(((ya_ref, wa_ref), (yb_ref, wb_ref), (yc_ref, wc_ref))):
        y = _dot(y_ref[...], w_ref[...])
        gate = _sigmoid(gt_ref[:, j * D:(j + 1) * D].astype(F32))
        merged = gate * y if merged is None else merged + gate * y
    o_ref[...] = x_ref[...] + g1_ref[0] * _dot(merged.astype(BF16), wo_ref[...])


def _merge(x2, gates, ya, yb, yc, wa, wb, wc, wo, mod3, l, B, S, tr):
    N, D = x2.shape
    nb = S // tr
    base = l * B * 6
    full = lambda a: pl.BlockSpec(a.shape, lambda i: (0,) * a.ndim)
    rows = lambda a: pl.BlockSpec((tr, a.shape[1]), lambda i: (i, 0))
    return pl.pallas_call(
        _merge_kernel,
        grid=(N // tr,),
        in_specs=[rows(x2), rows(gates), rows(ya), rows(yb), rows(yc),
                  full(wa), full(wb), full(wc), full(wo),
                  pl.BlockSpec((1, 1, D), lambda i: (base + (i // nb) * 6 + 2, 0, 0))],
        out_specs=rows(x2),
        out_shape=jax.ShapeDtypeStruct((N, D), F32),
        compiler_params=_cparams(("parallel",)),
        name="merge_out",
    )(x2, gates, ya, yb, yc, wa, wb, wc, wo, mod3)


def _router_kernel(x_ref, g_ref, sh_ref, sc_ref, wr_ref, br_ref,
                   h_ref, idx_ref, wt_ref, rank_ref, cnt_ref, carry_ref):
    tr = x_ref.shape[0]

    @pl.when(pl.program_id(0) == 0)
    def _():
        carry_ref[...] = jnp.zeros_like(carry_ref)

    x = x_ref[...]
    ms = jnp.mean(x * x, axis=-1, keepdims=True)
    h = x * lax.rsqrt(ms + NORM_EPS) * g_ref[...]
    h = h * (1.0 + sc_ref[0]) + sh_ref[0]
    h_ref[...] = h
    h1, h2, _ = _split3(h)
    w1, w2, _ = _split3(wr_ref[...])
    logits = _dot(h1, w1) + (_dot(h1, w2) + _dot(h2, w1)) + br_ref[...]
    lane = lax.broadcasted_iota(jnp.int32, logits.shape, 1)
    work = logits
    vals, ids = [], []
    for _k in range(TOP_K):
        m = jnp.max(work, axis=-1, keepdims=True)
        sel = jnp.min(jnp.where(work == m, lane, LANES), axis=-1, keepdims=True)
        vals.append(m)
        ids.append(sel)
        work = jnp.where(lane == sel, -jnp.inf, work)
    es = [jnp.exp(v - vals[0]) for v in vals]
    den = es[0] + es[1] + es[2] + es[3]
    onehot = jnp.zeros(logits.shape, F32)
    idx_out = jnp.zeros(logits.shape, jnp.int32)
    wt_out = jnp.zeros(logits.shape, F32)
    for k in range(TOP_K):
        onehot = onehot + (lane == ids[k]).astype(F32)
        idx_out = jnp.where(lane == k, ids[k], idx_out)
        wt_out = jnp.where(lane == k, es[k] / den, wt_out)
    idx_ref[...] = idx_out
    wt_ref[...] = wt_out
    r = lax.broadcasted_iota(jnp.int32, (tr, tr), 0)
    c = lax.broadcasted_iota(jnp.int32, (tr, tr), 1)
    strict = (c < r).astype(BF16)
    before = _dot(strict, onehot.astype(BF16)) + carry_ref[...]
    rank_out = jnp.zeros(logits.shape, jnp.int32)
    for k in range(TOP_K):
        rk = jnp.sum(jnp.where(lane == ids[k], before, 0.0), axis=-1, keepdims=True)
        rank_out = jnp.where(lane == k, rk.astype(jnp.int32), rank_out)
    rank_ref[...] = rank_out
    carry_ref[...] = carry_ref[...] + jnp.sum(onehot, axis=0, keepdims=True)
    cnt_ref[...] = jnp.broadcast_to(carry_ref[...], cnt_ref.shape)


def _router(x2, g, mod3, l, wr_p, br_p, B, S, tr):
    N, D = x2.shape
    nb = S // tr
    base = l * B * 6
    rows = lambda w: pl.BlockSpec((tr, w), lambda i: (i, 0))

    def modspec(j):
        return pl.BlockSpec((1, 1, D), lambda i: (base + (i // nb) * 6 + j, 0, 0))

    return pl.pallas_call(
        _router_kernel,
        grid=(N // tr,),
        in_specs=[rows(D), pl.BlockSpec((1, D), lambda i: (0, 0)), modspec(3), modspec(4),
                  pl.BlockSpec(wr_p.shape, lambda i: (0, 0)), pl.BlockSpec((1, LANES), lambda i: (0, 0))],
        out_specs=(rows(D), rows(LANES), rows(LANES), rows(LANES),
                   pl.BlockSpec((8, LANES), lambda i: (0, 0))),
        out_shape=(jax.ShapeDtypeStruct((N, D), F32),
                   jax.ShapeDtypeStruct((N, LANES), jnp.int32),
                   jax.ShapeDtypeStruct((N, LANES), F32),
                   jax.ShapeDtypeStruct((N, LANES), jnp.int32),
                   jax.ShapeDtypeStruct((8, LANES), F32)),
        scratch_shapes=[pltpu.VMEM((1, LANES), F32)],
        compiler_params=_cparams(("arbitrary",)),
        name="router",
    )(x2, g, mod3, mod3, wr_p, br_p)


def _dispatch_kernel(slot_ref, zflag_ref, h_ref, xs_hbm, zbuf, sem):
    i = pl.program_id(0)
    tn = h_ref.shape[0]
    tm = zbuf.shape[0]
    n_tiles = xs_hbm.shape[0] // tm

    @pl.when(i == 0)
    def _():
        zbuf[...] = jnp.zeros_like(zbuf)

        def zero_copy(t):
            return pltpu.make_async_copy(zbuf, xs_hbm.at[pl.ds(pl.multiple_of(t * tm, tm), tm)], sem)

        def zstart(t, c):
            @pl.when(zflag_ref[t] > 0)
            def _():
                zero_copy(t).start()
            return c

        def zwait(t, c):
            @pl.when(zflag_ref[t] > 0)
            def _():
                zero_copy(t).wait()
            return c

        lax.fori_loop(0, n_tiles, zstart, 0)
        lax.fori_loop(0, n_tiles, zwait, 0)

    base = i * tn * TOP_K

    def issue(r, c):
        for k in range(TOP_K):
            s = slot_ref[base + r * TOP_K + k]
            pltpu.make_async_copy(h_ref.at[pl.ds(r, 1)], xs_hbm.at[pl.ds(s, 1)], sem).start()
        return c

    lax.fori_loop(0, tn, issue, 0)
    for k in range(TOP_K):
        pltpu.make_async_copy(h_ref, xs_hbm.at[pl.ds(0, tn)], sem).wait()


def _dispatch(slots_flat, zero_flags, h2, n_rows, tn, tm):
    N, D = h2.shape
    grid_spec = pltpu.PrefetchScalarGridSpec(
        num_scalar_prefetch=2,
        grid=(N // tn,),
        in_specs=[pl.BlockSpec((tn, D), lambda i, sl, zf: (i, 0))],
        out_specs=pl.BlockSpec(memory_space=pl.ANY),
        scratch_shapes=[pltpu.VMEM((tm, D), F32), pltpu.SemaphoreType.DMA(())],
    )
    return pl.pallas_call(
        _dispatch_kernel,
        grid_spec=grid_spec,
        out_shape=jax.ShapeDtypeStruct((n_rows, D), F32),
        compiler_params=_cparams(("arbitrary",)),
        name="moe_dispatch",
    )(slots_flat, zero_flags, h2)


def _expert_kernel(te_ref, nu_ref, x_ref, w1_ref, b1_ref, w2_ref, b2_ref, o_ref, w1b, w2b):
    t = pl.program_id(0)
    F = w2_ref.shape[1]

    @pl.when(t < nu_ref[0])
    def _():
        @pl.when(jnp.logical_or(t == 0, te_ref[t] != te_ref[jnp.maximum(t - 1, 0)]))
        def _():
            w1b[...] = w1_ref[0].astype(BF16)
            w2b[...] = w2_ref[0].astype(BF16)

        x = x_ref[...].astype(BF16)
        a = _dot(x, w1b[...]) + b1_ref[0]
        glu = jnp.minimum(a[:, :F], SWIGLU_LIMIT)
        lin = jnp.clip(a[:, F:], -SWIGLU_LIMIT, SWIGLU_LIMIT)
        act = glu * _sigmoid(SWIGLU_ALPHA * glu) * (lin + 1.0)
        o_ref[...] = _dot(act.astype(BF16), w2b[...]) + b2_ref[0]

    @pl.when(t >= nu_ref[0])
    def _():
        o_ref[...] = jnp.zeros_like(o_ref)


def _experts(tile_expert, n_used, xs, w1_all, b1_all, w2_all, b2_all, l, tm):
    n_tiles = tile_expert.shape[0]
    D = xs.shape[1]
    L, E, _, F2 = w1_all.shape
    F = w2_all.shape[2]
    w1 = w1_all.reshape(L * E, D, F2)
    w2 = w2_all.reshape(L * E, F, D)
    b1 = b1_all.reshape(L * E, F2)
    b2 = b2_all.reshape(L * E, D)
    E0 = l * E
    E = L * E
    grid_spec = pltpu.PrefetchScalarGridSpec(
        num_scalar_prefetch=2,
        grid=(n_tiles,),
        in_specs=[pl.BlockSpec((tm, D), lambda t, te, nu: (jnp.minimum(t, nu[0] - 1), 0)),
                  pl.BlockSpec((1, D, F2), lambda t, te, nu: (E0 + te[t], 0, 0)),
                  pl.BlockSpec((1, 1, F2), lambda t, te, nu: (E0 + te[t], 0, 0)),
                  pl.BlockSpec((1, F, D), lambda t, te, nu: (E0 + te[t], 0, 0)),
                  pl.BlockSpec((1, 1, D), lambda t, te, nu: (E0 + te[t], 0, 0))],
        out_specs=pl.BlockSpec((tm, D), lambda t, te, nu: (t, 0)),
        scratch_shapes=[pltpu.VMEM((D, F2), BF16), pltpu.VMEM((F, D), BF16)],
    )
    return pl.pallas_call(
        _expert_kernel,
        grid_spec=grid_spec,
        out_shape=jax.ShapeDtypeStruct((n_tiles * tm, D), F32),
        compiler_params=_cparams(("arbitrary",)),
        name="expert_mlp",
    )(tile_expert, n_used, xs, w1, b1.reshape(E, 1, F2), w2, b2.reshape(E, 1, D))


def _combine_kernel(slot_ref, ys_hbm, x_ref, wt_ref, g2_ref, o_ref, buf, sem):
    i = pl.program_id(0)
    tn = x_ref.shape[0]
    base = i * tn * TOP_K

    def issue(r, c):
        for k in range(TOP_K):
            s = slot_ref[base + r * TOP_K + k]
            pltpu.make_async_copy(ys_hbm.at[pl.ds(s, 1)], buf.at[k, pl.ds(r, 1)], sem).start()
        return c

    lax.fori_loop(0, tn, issue, 0)
    for k in range(TOP_K):
        pltpu.make_async_copy(ys_hbm.at[pl.ds(0, tn)], buf.at[k], sem).wait()
    wt = wt_ref[...]
    y = wt[:, 0:1] * buf[0]
    for k in range(1, TOP_K):
        y = y + wt[:, k:k + 1] * buf[k]
    o_ref[...] = x_ref[...] + g2_ref[0] * y


def _combine(slots_flat, ys, x2, wts, mod3, l, B, S, tn):
    N, D = x2.shape
    nb = S // tn
    base = l * B * 6
    grid_spec = pltpu.PrefetchScalarGridSpec(
        num_scalar_prefetch=1,
        grid=(N // tn,),
        in_specs=[pl.BlockSpec(memory_space=pl.ANY),
                  pl.BlockSpec((tn, D), lambda i, sl: (i, 0)),
                  pl.BlockSpec((tn, LANES), lambda i, sl: (i, 0)),
                  pl.BlockSpec((1, 1, D), lambda i, sl: (base + (i // nb) * 6 + 5, 0, 0))],
        out_specs=pl.BlockSpec((tn, D), lambda i, sl: (i, 0)),
        scratch_shapes=[pltpu.VMEM((TOP_K, tn, D), F32), pltpu.SemaphoreType.DMA(())],
    )
    return pl.pallas_call(
        _combine_kernel,
        grid_spec=grid_spec,
        out_shape=jax.ShapeDtypeStruct((N, D), F32),
        compiler_params=_cparams(("arbitrary",)),
        name="moe_combine",
    )(slots_flat, ys, x2, wts, mod3)


def _tile_rows(S, want):
    t = min(S, want)
    while S % t:
        t //= 2
    return t


def _prep_in_weights(w_in_l):
    HW = HGRN_HEADS * HGRN_DIM
    o = 0
    wh = w_in_l[:, o:o + 4 * HW]; o += 4 * HW
    wql = w_in_l[:, o:o + MLA_Q_RANK]; o += MLA_Q_RANK
    wkvl = w_in_l[:, o:o + MLA_KV_RANK]; o += MLA_KV_RANK
    wkr = w_in_l[:, o:o + MLA_ROPE]; o += MLA_ROPE
    wp = w_in_l[:, o:o + HW]; o += HW
    wg = w_in_l[:, o:]
    D = w_in_l.shape[0]
    wkr_p = jnp.zeros((D, LANES), w_in_l.dtype).at[:, MLA_NOPE:MLA_NOPE + MLA_ROPE].set(wkr)
    wm = jnp.concatenate([wql, wkvl, wkr_p, _rope_partner(wkr_p)], axis=1)
    return wg.astype(BF16), wh.astype(BF16), wm.astype(BF16), wp.astype(BF16)


def _rope_partner(w):
    half = MLA_ROPE // 2
    w3 = w.reshape(w.shape[:-1] + (-1, LANES))
    first = w3[..., MLA_NOPE:MLA_NOPE + half]
    second = w3[..., MLA_NOPE + half:MLA_NOPE + MLA_ROPE]
    out = jnp.concatenate([jnp.zeros_like(w3[..., :MLA_NOPE]), second, first,
                           jnp.zeros_like(w3[..., MLA_NOPE + MLA_ROPE:])], axis=-1)
    return out.reshape(w.shape)


def _pad_heads(w, n_heads, width):
    K = w.shape[0]
    w3 = w.reshape(K, n_heads, width)
    return jnp.pad(w3, ((0, 0), (0, 0), (0, LANES - width))).reshape(K, n_heads * LANES)


def kernel(x, c, positions, ada_w, ada_b, norm1_g, norm2_g, w_in, hgrn_lb, hgrn_onorm_g, mla_qlat_g, mla_kvlat_g, w_uq, w_ukv, q_norm_g, k_norm_g, w_pool, pool_scale, w_br_a, w_br_b, w_br_c, w_out, w_router, b_router, w_exp1, b_exp1, w_exp2, b_exp2):
    B, S, D = x.shape
    L = ada_w.shape[0]
    N = B * S
    H = MLA_HEADS
    tr = _tile_rows(S, 512)
    tc = _tile_rows(S, 256)
    ta = _tile_rows(S, 512)
    tm = 512
    tn = _tile_rows(S, 256)

    mod = _ada(c, ada_w, ada_b)
    mod3 = mod.reshape(L * B * 6, 1, D)

    inv_freq = 1.0 / (ROPE_THETA ** (jnp.arange(0, MLA_ROPE, 2, dtype=F32) / MLA_ROPE))
    freq_row = jnp.zeros((1, LANES), F32).at[0, MLA_NOPE:MLA_NOPE + MLA_ROPE].set(
        jnp.concatenate([inv_freq, inv_freq]))
    posf = positions.astype(F32).reshape(N, 1)
    cos_t, sin_t = _rope_tables(posf, freq_row, tr)

    n_tiles = (N * TOP_K) // tm + N_EXPERTS
    x2 = x.reshape(N, D)
    for l in range(L):
        wg, wh, wm, wp = _prep_in_weights(w_in[l])
        pg, ph, pm, pp = _inproj(x2, norm1_g[l].reshape(1, D), mod3, l, wg, wh, wm, wp, B, S, tr)

        ya = _hgrn(ph.reshape(B, S, -1), hgrn_lb, hgrn_onorm_g[l].reshape(1, HGRN_DIM), l, B, S, tc)

        wq_f = _pad_heads(w_uq[l], H, MLA_QK)
        wq_p = wq_f.astype(BF16)
        wqr_p = _rope_partner(wq_f).astype(BF16)
        wkv3 = w_ukv[l].reshape(MLA_KV_RANK, H, MLA_NOPE + MLA_V)
        wkv_p = jnp.concatenate(
            [_pad_heads(wkv3[:, :, :MLA_NOPE].reshape(MLA_KV_RANK, H * MLA_NOPE), H, MLA_NOPE),
             wkv3[:, :, MLA_NOPE:].reshape(MLA_KV_RANK, H * MLA_V)], axis=1).astype(BF16)
        qn_p = jnp.pad(q_norm_g[l], (0, LANES - MLA_QK)).reshape(1, LANES)
        kn_p = jnp.pad(k_norm_g[l], (0, LANES - MLA_QK)).reshape(1, LANES)
        q, k, v = _mla_prep(pm, cos_t, sin_t, mla_qlat_g[l].reshape(1, -1), mla_kvlat_g[l].reshape(1, -1),
                            wq_p, wqr_p, wkv_p, qn_p, _rope_partner(qn_p), kn_p, _rope_partner(kn_p),
                            B, S, tr)
        yb = _attention(q, k, v.reshape(B, S, H * MLA_V), B, S, ta)

        yc = _pool(pp.reshape(B, S, -1), w_pool[l], pool_scale[l].reshape(1, -1))

        x2 = _merge(x2, pg, ya.reshape(N, -1), yb.reshape(N, -1), yc.reshape(N, -1),
                    w_br_a[l].astype(BF16), w_br_b[l].astype(BF16), w_br_c[l].astype(BF16),
                    w_out[l].astype(BF16), mod3, l, B, S, tr)

        wr_p = jnp.pad(w_router[l], ((0, 0), (0, LANES - N_EXPERTS)))
        br_p = jnp.concatenate([b_router[l], jnp.full((LANES - N_EXPERTS,), NEG_BIG, F32)]).reshape(1, LANES)
        h2, idx, wts, rank, cnt = _router(x2, norm2_g[l].reshape(1, D), mod3, l, wr_p, br_p, B, S, tr)

        counts = cnt[0, :N_EXPERTS].astype(jnp.int32)
        tiles_per = (counts + tm - 1) // tm
        tile_end = jnp.cumsum(tiles_per)
        offsets = (tile_end - tiles_per) * tm
        n_used = tile_end[-1:]
        tile_ids = jnp.minimum(jnp.arange(n_tiles, dtype=jnp.int32), n_used[0] - 1)
        tile_expert = jnp.minimum(
            jnp.sum((tile_end[None, :] <= tile_ids[:, None]).astype(jnp.int32), axis=1), N_EXPERTS - 1)
        idx4 = idx[:, :TOP_K]
        e_ids = jnp.arange(N_EXPERTS, dtype=jnp.int32)
        slots = rank[:, :TOP_K] + jnp.sum(
            jnp.where(idx4[:, :, None] == e_ids[None, None, :], offsets[None, None, :], 0), axis=-1)
        slots_flat = slots.reshape(-1).astype(jnp.int32)
        all_tiles = jnp.arange(n_tiles, dtype=jnp.int32)
        is_tail = jnp.any((tile_end[None, :] - 1 == all_tiles[:, None]) & (tiles_per[None, :] > 0), axis=1)
        zero_flags = (is_tail | (all_tiles >= n_used[0])).astype(jnp.int32)

        xs = _dispatch(slots_flat, zero_flags, h2, n_tiles * tm, tn, tm)
        ys = _experts(tile_expert.astype(jnp.int32), n_used.astype(jnp.int32), xs,
                      w_exp1, b_exp1, w_exp2, b_exp2, l, tm)
        x2 = _combine(slots_flat, ys, x2, wts, mod3, l, B, S, tn)
    return x2.reshape(B, S, D)
```

```python
import functools
import math

import jax
import jax.numpy as jnp
from jax import lax
from jax.experimental import pallas as pl
from jax.experimental.pallas import tpu as pltpu

F32 = jnp.float32
BF16 = jnp.bfloat16

LANES = 128
NORM_EPS = 1e-6
HGRN_HEADS = 4
HGRN_DIM = 128
HGRN_CHUNK = 64
HGRN_SUB = 16
MLA_HEADS = 8
MLA_NOPE = 64
MLA_ROPE = 32
MLA_QK = MLA_NOPE + MLA_ROPE
MLA_V = 64
MLA_Q_RANK = 384
MLA_KV_RANK = 256
ROPE_THETA = 10000.0
POOL_WINDOWS = (2, 4, 8, 16)
N_EXPERTS = 32
TOP_K = 4
SWIGLU_LIMIT = 7.0
SWIGLU_ALPHA = 1.702
NEG_BIG = -1e30
LOG2E = 1.4426950408889634
SEG_ALIGN = 16
SORT_CHUNK = 256

VMEM_LIMIT = 56 * 1024 * 1024


def _cparams(sem):
    return pltpu.CompilerParams(dimension_semantics=sem, vmem_limit_bytes=VMEM_LIMIT)


def _dot(a, b):
    return jnp.dot(a, b, preferred_element_type=F32)


def _dot_nt(a, b):
    return lax.dot_general(a, b, (((1,), (1,)), ((), ())), preferred_element_type=F32)


def _dot_tn(a, b):
    return lax.dot_general(a, b, (((0,), (0,)), ((), ())), preferred_element_type=F32)


def _split3(x):
    hi = x.astype(BF16)
    r = x - hi.astype(F32)
    mid = r.astype(BF16)
    lo = (r - mid.astype(F32)).astype(BF16)
    return hi, mid, lo


def _sigmoid(x):
    return 1.0 / (1.0 + jnp.exp(-x))


def _ada_kernel(c_ref, w_ref, b_ref, o_ref):
    c = c_ref[...]
    cond = (c * _sigmoid(c)).astype(BF16)
    o_ref[0] = _dot(cond, w_ref[0].astype(BF16)) + b_ref[0]


def _ada(c, ada_w, ada_b):
    L, D, W = ada_w.shape
    B = c.shape[0]
    tn = D
    return pl.pallas_call(
        _ada_kernel,
        grid=(L, W // tn),
        in_specs=[
            pl.BlockSpec((B, D), lambda l, j: (0, 0)),
            pl.BlockSpec((1, D, tn), lambda l, j: (l, 0, j)),
            pl.BlockSpec((1, 1, tn), lambda l, j: (l, 0, j)),
        ],
        out_specs=pl.BlockSpec((1, B, tn), lambda l, j: (l, 0, j)),
        out_shape=jax.ShapeDtypeStruct((L, B, W), F32),
        compiler_params=_cparams(("parallel", "parallel")),
        name="ada_mod",
    )(c, ada_w, ada_b.reshape(L, 1, W))


def _inproj_kernel(x_ref, g_ref, sh_ref, sc_ref, wg_ref, wh_ref, wm_ref, wp_ref,
                   og_ref, oh_ref, om_ref, op_ref):
    x = x_ref[...]
    ms = jnp.mean(x * x, axis=-1, keepdims=True)
    h = x * lax.rsqrt(ms + NORM_EPS) * g_ref[...]
    h = h * (1.0 + sc_ref[0]) + sh_ref[0]
    hb = h.astype(BF16)
    cw = 512
    for w_ref, o_ref in ((wg_ref, og_ref), (wh_ref, oh_ref), (wm_ref, om_ref), (wp_ref, op_ref)):
        width = w_ref.shape[1]
        step = cw if width % cw == 0 else width
        for j in range(0, width, step):
            o_ref[:, j:j + step] = _dot(hb, w_ref[:, j:j + step]).astype(o_ref.dtype)


def _inproj(x2, g, mod3, l, wg, wh, wm, wp, B, S, tr):
    N, D = x2.shape
    nb = S // tr
    base = l * B * 6

    def modspec(j):
        return pl.BlockSpec((1, 1, D), lambda i: (base + (i // nb) * 6 + j, 0, 0))

    def wspec(w):
        return pl.BlockSpec(w.shape, lambda i: (0, 0), pipeline_mode=pl.Buffered(1))

    outs = (
        jax.ShapeDtypeStruct((N, wg.shape[1]), BF16),
        jax.ShapeDtypeStruct((N, wh.shape[1]), F32),
        jax.ShapeDtypeStruct((N, wm.shape[1]), F32),
        jax.ShapeDtypeStruct((N, wp.shape[1]), F32),
    )
    return pl.pallas_call(
        _inproj_kernel,
        grid=(N // tr,),
        in_specs=[
            pl.BlockSpec((tr, D), lambda i: (i, 0)),
            pl.BlockSpec((1, D), lambda i: (0, 0)),
            modspec(0), modspec(1),
            wspec(wg), wspec(wh), wspec(wm), wspec(wp),
        ],
        out_specs=tuple(pl.BlockSpec((tr, o.shape[1]), lambda i: (i, 0)) for o in outs),
        out_shape=outs,
        compiler_params=_cparams(("parallel",)),
        name="in_proj",
    )(x2, g, mod3, mod3, wg, wh, wm, wp)


def _hgrn_kernel(layer, n_chunks, q_ref, f_ref, i_ref, g_ref, lb_ref, on_ref, o_ref, st_ref):
    C, R, DH = HGRN_CHUNK, HGRN_SUB, HGRN_DIM

    @pl.when(pl.program_id(1) == 0)
    def _():
        st_ref[...] = jnp.zeros_like(st_ref)

    lbr = lb_ref[...]
    e = jnp.exp(lbr - jnp.max(lbr, axis=0, keepdims=True))
    p = e / jnp.sum(e, axis=0, keepdims=True)
    lb_all = jnp.zeros((1, lbr.shape[1]), F32)
    for j in range(1, layer + 1):
        lb_all = lb_all + p[j:j + 1]
    onorm = on_ref[...]

    row_c = lax.broadcasted_iota(jnp.int32, (C, C), 0)
    col_c = lax.broadcasted_iota(jnp.int32, (C, C), 1)
    tri = (col_c <= row_c).astype(BF16)
    row_r = lax.broadcasted_iota(jnp.int32, (R, 1), 0)

    def head_chunk(sl, h):
        hs = slice(h * DH, (h + 1) * DH)
        lb = lb_all[:, hs]
        log_lb = jnp.log(lb)
        log_1mlb = jnp.log1p(-lb)
        q = q_ref[0, sl, hs]
        z = f_ref[0, sl, hs]
        v = i_ref[0, sl, hs]
        g = g_ref[0, sl, hs]
        ez = jnp.exp(-jnp.abs(z))
        log_sig = jnp.minimum(z, 0.0) - jnp.log1p(ez)
        a = log_1mlb + log_sig
        m = jnp.maximum(a, log_lb)
        lf = m + jnp.log(jnp.exp(a - m) + jnp.exp(log_lb - m))
        kk = (1.0 - lb) * jnp.where(z >= 0.0, ez, 1.0) / (1.0 + ez)
        l1, l2, l3 = _split3(lf)
        b = (_dot(tri, l1) + _dot(tri, l2) + _dot(tri, l3)) * LOG2E
        st = st_ref[h]
        qs = (q * jnp.exp2(b)).astype(BF16)
        o = _dot_nt(qs, st.astype(BF16))
        vb = v.astype(BF16)
        outs = []
        for i in range(C // R):
            r0 = i * R
            bi = b[r0:r0 + R]
            qi = q[r0:r0 + R]
            ki = kk[r0:r0 + R]
            vi = v[r0:r0 + R]
            oi = o[r0:r0 + R]
            if i > 0:
                ref = b[r0 - 1:r0]
                qt = (qi * jnp.exp2(bi - ref)).astype(BF16)
                kt = (kk[:r0] * jnp.exp2(ref - b[:r0])).astype(BF16)
                att = _dot_nt(qt, kt)
                oi = oi + _dot(att.astype(BF16), vb[:r0])
            for s in range(R):
                es = jnp.exp2(jnp.minimum(bi - bi[s:s + 1], 0.0))
                col = jnp.sum(qi * es * ki[s:s + 1], axis=-1, keepdims=True)
                col = jnp.where(row_r >= s, col, 0.0)
                oi = oi + col * vi[s:s + 1]
            outs.append(oi)
        o = jnp.concatenate(outs, axis=0)
        bl = b[C - 1:C]
        kd = (kk * jnp.exp2(bl - b)).astype(BF16)
        st_ref[h] = st * jnp.exp2(bl) + _dot_tn(vb, kd)
        ms = jnp.mean(o * o, axis=-1, keepdims=True)
        y = o * lax.rsqrt(ms + NORM_EPS) * onorm
        y = y * (g * _sigmoid(g))
        o_ref[0, sl, hs] = y.astype(o_ref.dtype)

    def chunk(ci, carry):
        sl = pl.ds(pl.multiple_of(ci * C, C), C)
        for h in range(HGRN_HEADS):
            head_chunk(sl, h)
        return carry

    lax.fori_loop(0, n_chunks, chunk, 0)


def _hgrn(ph, hgrn_lb, onorm_g, layer, B, S, tc):
    L = hgrn_lb.shape[0]
    W = HGRN_HEADS * HGRN_DIM

    def spec(off):
        return pl.BlockSpec((1, tc, W), lambda b, t: (b, t, off))

    return pl.pallas_call(
        functools.partial(_hgrn_kernel, layer, tc // HGRN_CHUNK),
        grid=(B, S // tc),
        in_specs=[spec(0), spec(1), spec(2), spec(3),
                  pl.BlockSpec((L, W), lambda b, t: (0, 0)),
                  pl.BlockSpec((1, HGRN_DIM), lambda b, t: (0, 0))],
        out_specs=pl.BlockSpec((1, tc, W), lambda b, t: (b, t, 0)),
        out_shape=jax.ShapeDtypeStruct((B, S, W), BF16),
        scratch_shapes=[pltpu.VMEM((HGRN_HEADS, HGRN_DIM, HGRN_DIM), F32)],
        compiler_params=_cparams(("parallel", "arbitrary")),
        name="hgrn2",
    )(ph, ph, ph, ph, hgrn_lb, onorm_g)


def _rope_table_kernel(pos_ref, fr_ref, cos_ref, sin_ref):
    ang = pos_ref[...] * fr_ref[...]
    lane = lax.broadcasted_iota(jnp.int32, ang.shape, 1)
    sn = jnp.sin(ang)
    cos_ref[...] = jnp.cos(ang)
    sin_ref[...] = jnp.where(lane < MLA_NOPE + MLA_ROPE // 2, -sn, sn)


def _rope_tables(posf, freq_row, tr):
    N = posf.shape[0]
    out = jax.ShapeDtypeStruct((N, LANES), F32)
    return pl.pallas_call(
        _rope_table_kernel,
        grid=(N // tr,),
        in_specs=[pl.BlockSpec((tr, 1), lambda i: (i, 0)), pl.BlockSpec((1, LANES), lambda i: (0, 0))],
        out_specs=(pl.BlockSpec((tr, LANES), lambda i: (i, 0)), pl.BlockSpec((tr, LANES), lambda i: (i, 0))),
        out_shape=(out, out),
        compiler_params=_cparams(("parallel",)),
        name="rope_tables",
    )(posf, freq_row)


def _mla_prep_kernel(pm_ref, cos_ref, sin_ref, qg_ref, kvg_ref, wq_ref, wqr_ref, wkv_ref,
                     qn_ref, qnr_ref, kn_ref, knr_ref, q_ref, k_ref, v_ref):
    pm = pm_ref[...]
    o = 0
    ql = pm[:, o:o + MLA_Q_RANK]; o += MLA_Q_RANK
    kvl = pm[:, o:o + MLA_KV_RANK]; o += MLA_KV_RANK
    kr = pm[:, o:o + LANES]; o += LANES
    krr = pm[:, o:o + LANES]

    def rms(x, g):
        return x * lax.rsqrt(jnp.mean(x * x, axis=-1, keepdims=True) + NORM_EPS) * g

    qlb = rms(ql, qg_ref[...]).astype(BF16)
    qa = _dot(qlb, wq_ref[...])
    qr = _dot(qlb, wqr_ref[...])
    kva = _dot(rms(kvl, kvg_ref[...]).astype(BF16), wkv_ref[...])
    v_ref[...] = kva[:, MLA_HEADS * LANES:].astype(v_ref.dtype)

    cs = cos_ref[...]
    sn = sin_ref[...]
    scale = 1.0 / math.sqrt(MLA_QK)
    q_cos = qn_ref[...] * cs * scale
    q_sin = qnr_ref[...] * sn * scale
    k_cos = kn_ref[...] * cs
    k_sin = knr_ref[...] * sn
    k_rot = krr * k_sin

    def inv_rms(xh):
        return lax.rsqrt(jnp.sum(xh * xh, axis=-1, keepdims=True) * (1.0 / MLA_QK) + NORM_EPS)

    for h in range(MLA_HEADS):
        hs = slice(h * LANES, (h + 1) * LANES)
        qh = qa[:, hs]
        q_ref[0, h] = ((qh * q_cos + qr[:, hs] * q_sin) * inv_rms(qh)).astype(q_ref.dtype)
        kh = kva[:, hs] + kr
        k_ref[0, h] = ((kh * k_cos + k_rot) * inv_rms(kh)).astype(k_ref.dtype)


def _mla_prep(pm, cos_t, sin_t, qlat_g, kvlat_g, wq, wqr, wkv, qn, qnr, kn, knr, B, S, tr):
    N = pm.shape[0]
    nb = S // tr
    H = MLA_HEADS
    full = lambda a: pl.BlockSpec(a.shape, lambda i: (0,) * a.ndim)
    return pl.pallas_call(
        _mla_prep_kernel,
        grid=(N // tr,),
        in_specs=[pl.BlockSpec((tr, pm.shape[1]), lambda i: (i, 0)),
                  pl.BlockSpec((tr, LANES), lambda i: (i, 0)),
                  pl.BlockSpec((tr, LANES), lambda i: (i, 0)),
                  full(qlat_g), full(kvlat_g), full(wq), full(wqr), full(wkv),
                  full(qn), full(qnr), full(kn), full(knr)],
        out_specs=(pl.BlockSpec((1, H, tr, LANES), lambda i: (i // nb, 0, i % nb, 0)),
                   pl.BlockSpec((1, H, tr, LANES), lambda i: (i // nb, 0, i % nb, 0)),
                   pl.BlockSpec((tr, H * MLA_V), lambda i: (i, 0))),
        out_shape=(jax.ShapeDtypeStruct((B, H, S, LANES), BF16),
                   jax.ShapeDtypeStruct((B, H, S, LANES), BF16),
                   jax.ShapeDtypeStruct((N, H * MLA_V), BF16)),
        compiler_params=_cparams(("parallel",)),
        name="mla_prep",
    )(pm, cos_t, sin_t, qlat_g, kvlat_g, wq, wqr, wkv, qn, qnr, kn, knr)


def _attn_kernel(q_ref, k_ref, v_ref, o_ref, m_ref, l_ref, acc_ref):
    qi = pl.program_id(2)
    T = q_ref.shape[2]

    m_ref[...] = jnp.full_like(m_ref, NEG_BIG)
    l_ref[...] = jnp.zeros_like(l_ref)
    acc_ref[...] = jnp.zeros_like(acc_ref)

    def step(ki, masked):
        rows = pl.ds(pl.multiple_of(ki * T, T), T)
        vblk = v_ref[0, rows, :]
        lane = lax.broadcasted_iota(jnp.int32, vblk.shape, 1)
        if masked:
            row = lax.broadcasted_iota(jnp.int32, (T, T), 0)
            col = lax.broadcasted_iota(jnp.int32, (T, T), 1)
            keep = col <= row
        for h in range(2):
            s = _dot_nt(q_ref[0, h], k_ref[0, h, rows, :])
            if masked:
                s = jnp.where(keep, s, NEG_BIG)
            m_old = m_ref[h]
            m_new = jnp.maximum(m_old, jnp.max(s, axis=-1, keepdims=True))
            alpha = jnp.exp(m_old - m_new)
            p = jnp.exp(s - jnp.tile(m_new, (1, T // LANES)))
            l_ref[h] = alpha * l_ref[h] + jnp.sum(p, axis=-1, keepdims=True)
            vh = jnp.where((lane < MLA_V) if h == 0 else (lane >= MLA_V), vblk, jnp.zeros_like(vblk))
            acc_ref[h] = alpha * acc_ref[h] + _dot(p.astype(BF16), vh)
            m_ref[h] = m_new

    def below_diagonal(ki, c):
        step(ki, False)
        return c

    lax.fori_loop(0, qi, below_diagonal, 0)
    step(qi, True)
    o_ref[0] = (acc_ref[0] / l_ref[0] + acc_ref[1] / l_ref[1]).astype(o_ref.dtype)


def _attention(q, k, v3, B, S, T):
    H = MLA_HEADS
    n = S // T
    return pl.pallas_call(
        _attn_kernel,
        grid=(B, H // 2, n),
        in_specs=[pl.BlockSpec((1, 2, T, LANES), lambda b, hp, qi: (b, hp, qi, 0)),
                  pl.BlockSpec((1, 2, S, LANES), lambda b, hp, qi: (b, hp, 0, 0)),
                  pl.BlockSpec((1, S, LANES), lambda b, hp, qi: (b, 0, hp))],
        out_specs=pl.BlockSpec((1, T, LANES), lambda b, hp, qi: (b, qi, hp)),
        out_shape=jax.ShapeDtypeStruct((B, S, H * MLA_V), BF16),
        scratch_shapes=[pltpu.VMEM((2, T, LANES), F32), pltpu.VMEM((2, T, LANES), F32),
                        pltpu.VMEM((2, T, LANES), F32)],
        compiler_params=_cparams(("parallel", "parallel", "arbitrary")),
        name="mla_attention",
    )(q, k, v3)


def _pool_kernel(u_ref, w_ref, sc_ref, o_ref):
    S = u_ref.shape[1]
    row = lax.broadcasted_iota(jnp.int32, (S, LANES), 0)
    t1 = (row + 1).astype(F32)
    for gi, win in enumerate(POOL_WINDOWS):
        u = u_ref[0, :, gi * LANES:(gi + 1) * LANES]
        s = u
        k = 1
        while k < win:
            s = s + jnp.where(row >= k, pltpu.roll(s, k, 0), 0.0)
            k *= 2
        pooled = s / jnp.minimum(t1, float(win))
        mixed = _dot((pooled - u).astype(BF16), w_ref[gi].astype(BF16))
        o_ref[0, :, gi * LANES:(gi + 1) * LANES] = (
            mixed * sc_ref[:, gi * LANES:(gi + 1) * LANES]).astype(o_ref.dtype)


def _pool(pp3, w_pool, scale):
    B, S, W = pp3.shape
    return pl.pallas_call(
        _pool_kernel,
        grid=(B,),
        in_specs=[pl.BlockSpec((1, S, W), lambda b: (b, 0, 0)),
                  pl.BlockSpec(w_pool.shape, lambda b: (0, 0, 0)),
                  pl.BlockSpec((1, W), lambda b: (0, 0))],
        out_specs=pl.BlockSpec((1, S, W), lambda b: (b, 0, 0)),
        out_shape=jax.ShapeDtypeStruct((B, S, W), BF16),
        compiler_params=_cparams(("parallel",)),
        name="pool_mixer",
    )(pp3, w_pool, scale)


def _merge_kernel(x_ref, gt_ref, ya_ref, yb_ref, yc_ref, wa_ref, wb_ref, wc_ref, wo_ref, g1_ref, o_ref):
    D = x_ref.shape[1]
    merged = None
    for j, (y_ref, w_ref) in enumerate(((ya_ref, wa_ref), (yb_ref, wb_ref), (yc_ref, wc_ref))):
        y = _dot(y_ref[...], w_ref[...])
        gate = _sigmoid(gt_ref[:, j * D:(j + 1) * D].astype(F32))
        merged = gate * y if merged is None else merged + gate * y
    o_ref[...] = x_ref[...] + g1_ref[0] * _dot(merged.astype(BF16), wo_ref[...])


def _merge(x2, gates, ya, yb, yc, wa, wb, wc, wo, mod3, l, B, S, tr):
    N, D = x2.shape
    nb = S // tr
    base = l * B * 6
    full = lambda a: pl.BlockSpec(a.shape, lambda i: (0,) * a.ndim)
    rows = lambda a: pl.BlockSpec((tr, a.shape[1]), lambda i: (i, 0))
    return pl.pallas_call(
        _merge_kernel,
        grid=(N // tr,),
        in_specs=[rows(x2), rows(gates), rows(ya), rows(yb), rows(yc),
                  full(wa), full(wb), full(wc), full(wo),
                  pl.BlockSpec((1, 1, D), lambda i: (base + (i // nb) * 6 + 2, 0, 0))],
        out_specs=rows(x2),
        out_shape=jax.ShapeDtypeStruct((N, D), F32),
        compiler_params=_cparams(("parallel",)),
        name="merge_out",
    )(x2, gates, ya, yb, yc, wa, wb, wc, wo, mod3)


def _router_kernel(x_ref, g_ref, sh_ref, sc_ref, wr_ref, br_ref, xl_ref, ls_ref, wt_ref, cnt_ref):
    tr = x_ref.shape[0]
    rl = xl_ref.shape[0]

    x = x_ref[...]
    ms = jnp.mean(x * x, axis=-1, keepdims=True)
    h = x * lax.rsqrt(ms + NORM_EPS) * g_ref[...]
    h = h * (1.0 + sc_ref[0]) + sh_ref[0]
    h1, h2, _ = _split3(h)
    w1, w2, _ = _split3(wr_ref[...])
    logits = _dot(h1, w1) + (_dot(h1, w2) + _dot(h2, w1)) + br_ref[...]
    lane = lax.broadcasted_iota(jnp.int32, logits.shape, 1)
    work = logits
    vals, ids = [], []
    for _k in range(TOP_K):
        m = jnp.max(work, axis=-1, keepdims=True)
        sel = jnp.min(jnp.where(work == m, lane, LANES), axis=-1, keepdims=True)
        vals.append(m)
        ids.append(sel)
        work = jnp.where(lane == sel, -jnp.inf, work)
    es = [jnp.exp(v - vals[0]) for v in vals]
    den = es[0] + es[1] + es[2] + es[3]
    onehot = jnp.zeros(logits.shape, F32)
    wt_out = jnp.zeros(logits.shape, F32)
    for k in range(TOP_K):
        onehot = onehot + (lane == ids[k]).astype(F32)
        wt_out = jnp.where(lane == k, es[k] / den, wt_out)
    wt_ref[...] = wt_out

    r = lax.broadcasted_iota(jnp.int32, (tr, tr), 0)
    c = lax.broadcasted_iota(jnp.int32, (tr, tr), 1)
    strict = (c < r).astype(BF16)
    before = _dot(strict, onehot.astype(BF16))
    cnt = jnp.sum(onehot, axis=0, keepdims=True)
    cnt_ref[0] = jnp.broadcast_to(cnt, cnt_ref.shape[1:])
    units = jnp.floor((cnt + (SEG_ALIGN - 1.0)) * (1.0 / SEG_ALIGN))
    er = lax.broadcasted_iota(jnp.int32, (LANES, LANES), 0)
    ec = lax.broadcasted_iota(jnp.int32, (LANES, LANES), 1)
    upper = (er < ec).astype(BF16)
    seg_start = _dot(jnp.broadcast_to(units, (8, LANES)).astype(BF16), upper)[0:1] * float(SEG_ALIGN)
    pos = before + seg_start
    ls_out = jnp.full(logits.shape, -1.0, F32)
    for k in range(TOP_K):
        lk = jnp.sum(jnp.where(lane == ids[k], pos, 0.0), axis=-1, keepdims=True)
        ls_out = jnp.where(lane == k, lk, ls_out)
    ls_ref[...] = ls_out.astype(jnp.int32)

    ls_t = jnp.transpose(ls_out)
    targets = [ls_t[k:k + 1, :].astype(jnp.int32) for k in range(TOP_K)]
    hb = h.astype(BF16)
    ch = SORT_CHUNK
    for cidx in range(rl // ch):
        ri = lax.broadcasted_iota(jnp.int32, (ch, tr), 0) + cidx * ch
        hit = ri == targets[0]
        for k in range(1, TOP_K):
            hit = jnp.logical_or(hit, ri == targets[k])
        perm = jnp.where(hit, 1.0, 0.0).astype(BF16)
        xl_ref[cidx * ch:(cidx + 1) * ch, :] = _dot(perm, hb).astype(xl_ref.dtype)


def _local_rows(tb):
    return TOP_K * tb + N_EXPERTS * SEG_ALIGN


def _router(x2, g, mod3, l, wr_p, br_p, B, S, tr):
    N, D = x2.shape
    nb = S // tr
    base = l * B * 6
    rl = _local_rows(tr)
    rows = lambda w: pl.BlockSpec((tr, w), lambda i: (i, 0))

    def modspec(j):
        return pl.BlockSpec((1, 1, D), lambda i: (base + (i // nb) * 6 + j, 0, 0))

    return pl.pallas_call(
        _router_kernel,
        grid=(N // tr,),
        in_specs=[rows(D), pl.BlockSpec((1, D), lambda i: (0, 0)), modspec(3), modspec(4),
                  pl.BlockSpec(wr_p.shape, lambda i: (0, 0)), pl.BlockSpec((1, LANES), lambda i: (0, 0))],
        out_specs=(pl.BlockSpec((rl, D), lambda i: (i, 0)), rows(LANES), rows(LANES),
                   pl.BlockSpec((1, 8, LANES), lambda i: (i, 0, 0))),
        out_shape=(jax.ShapeDtypeStruct((N // tr * rl, D), BF16),
                   jax.ShapeDtypeStruct((N, LANES), jnp.int32),
                   jax.ShapeDtypeStruct((N, LANES), F32),
                   jax.ShapeDtypeStruct((N // tr, 8, LANES), F32)),
        compiler_params=_cparams(("parallel",)),
        name="router",
    )(x2, g, mod3, mod3, wr_p, br_p)


def _segment_copies(j, e, src_ref, dst_ref, len_ref, local_ref, global_ref, sem, to_global, n_bits):
    base = j * N_EXPERTS + e
    units = len_ref[base]
    s0 = src_ref[base]
    d0 = dst_ref[base]
    copies = []
    for bit in range(n_bits):
        rows = SEG_ALIGN << bit
        done = ((units >> (bit + 1)) << (bit + 1)) * SEG_ALIGN
        loc = local_ref.at[pl.ds(pl.multiple_of(s0 + done, SEG_ALIGN), rows)]
        glo = global_ref.at[pl.ds(pl.multiple_of(d0 + done, SEG_ALIGN), rows)]
        cp = pltpu.make_async_copy(loc, glo, sem) if to_global else pltpu.make_async_copy(glo, loc, sem)
        copies.append((((units >> bit) & 1) == 1, cp))
    return copies


def _move_segments(j, src_ref, dst_ref, len_ref, local_ref, global_ref, sem, to_global, n_bits):
    def run(action):
        def per_expert(e, c):
            for pred, cp in _segment_copies(j, e, src_ref, dst_ref, len_ref, local_ref, global_ref,
                                            sem, to_global, n_bits):
                @pl.when(pred)
                def _():
                    action(cp)
            return c
        lax.fori_loop(0, N_EXPERTS, per_expert, 0)

    run(lambda cp: cp.start())
    run(lambda cp: cp.wait())


def _place_kernel(n_bits, src_ref, dst_ref, len_ref, zflag_ref, xl_ref, xs_hbm, zbuf, sem):
    j = pl.program_id(0)
    tm = zbuf.shape[0]
    n_tiles = xs_hbm.shape[0] // tm

    @pl.when(j == 0)
    def _():
        zbuf[...] = jnp.zeros_like(zbuf)

        def zero_copy(t):
            return pltpu.make_async_copy(zbuf, xs_hbm.at[pl.ds(pl.multiple_of(t * tm, tm), tm)], sem)

        def zstart(t, c):
            @pl.when(zflag_ref[t] > 0)
            def _():
                zero_copy(t).start()
            return c

        def zwait(t, c):
            @pl.when(zflag_ref[t] > 0)
            def _():
                zero_copy(t).wait()
            return c

        lax.fori_loop(0, n_tiles, zstart, 0)
        lax.fori_loop(0, n_tiles, zwait, 0)

    _move_segments(j, src_ref, dst_ref, len_ref, xl_ref, xs_hbm, sem, True, n_bits)


def _place(seg_src, seg_dst, seg_len, zero_flags, xl, n_rows, rl, tm, n_bits):
    D = xl.shape[1]
    grid_spec = pltpu.PrefetchScalarGridSpec(
        num_scalar_prefetch=4,
        grid=(xl.shape[0] // rl,),
        in_specs=[pl.BlockSpec((rl, D), lambda j, a, b, c, d: (j, 0))],
        out_specs=pl.BlockSpec(memory_space=pl.ANY),
        scratch_shapes=[pltpu.VMEM((tm, D), xl.dtype), pltpu.SemaphoreType.DMA(())],
    )
    return pl.pallas_call(
        functools.partial(_place_kernel, n_bits),
        grid_spec=grid_spec,
        out_shape=jax.ShapeDtypeStruct((n_rows, D), xl.dtype),
        compiler_params=_cparams(("arbitrary",)),
        name="moe_place",
    )(seg_src, seg_dst, seg_len, zero_flags, xl)


def _expert_kernel(te_ref, nu_ref, x_ref, w1_ref, b1_ref, w2_ref, b2_ref, o_ref, w1b, w2b):
    t = pl.program_id(0)
    F = w2_ref.shape[1]

    @pl.when(t < nu_ref[0])
    def _():
        @pl.when(jnp.logical_or(t == 0, te_ref[t] != te_ref[jnp.maximum(t - 1, 0)]))
        def _():
            w1b[...] = w1_ref[0].astype(BF16)
            w2b[...] = w2_ref[0].astype(BF16)

        a = _dot(x_ref[...], w1b[...]) + b1_ref[0]
        glu = jnp.minimum(a[:, :F], SWIGLU_LIMIT)
        lin = jnp.clip(a[:, F:], -SWIGLU_LIMIT, SWIGLU_LIMIT)
        act = glu * _sigmoid(SWIGLU_ALPHA * glu) * (lin + 1.0)
        o_ref[...] = (_dot(act.astype(BF16), w2b[...]) + b2_ref[0]).astype(o_ref.dtype)

    @pl.when(t >= nu_ref[0])
    def _():
        o_ref[...] = jnp.zeros_like(o_ref)


def _experts(tile_expert, n_used, xs, w1_all, b1_all, w2_all, b2_all, l, tm):
    n_tiles = tile_expert.shape[0]
    D = xs.shape[1]
    L, E, _, F2 = w1_all.shape
    F = w2_all.shape[2]
    w1 = w1_all.reshape(L * E, D, F2)
    w2 = w2_all.reshape(L * E, F, D)
    b1 = b1_all.reshape(L * E, F2)
    b2 = b2_all.reshape(L * E, D)
    E0 = l * E
    E = L * E
    grid_spec = pltpu.PrefetchScalarGridSpec(
        num_scalar_prefetch=2,
        grid=(n_tiles,),
        in_specs=[pl.BlockSpec((tm, D), lambda t, te, nu: (jnp.minimum(t, nu[0] - 1), 0)),
                  pl.BlockSpec((1, D, F2), lambda t, te, nu: (E0 + te[t], 0, 0)),
                  pl.BlockSpec((1, 1, F2), lambda t, te, nu: (E0 + te[t], 0, 0)),
                  pl.BlockSpec((1, F, D), lambda t, te, nu: (E0 + te[t], 0, 0)),
                  pl.BlockSpec((1, 1, D), lambda t, te, nu: (E0 + te[t], 0, 0))],
        out_specs=pl.BlockSpec((tm, D), lambda t, te, nu: (t, 0)),
        scratch_shapes=[pltpu.VMEM((D, F2), BF16), pltpu.VMEM((F, D), BF16)],
    )
    return pl.pallas_call(
        _expert_kernel,
        grid_spec=grid_spec,
        out_shape=jax.ShapeDtypeStruct((n_tiles * tm, D), BF16),
        compiler_params=_cparams(("arbitrary",)),
        name="expert_mlp",
    )(tile_expert, n_used, xs, w1, b1.reshape(E, 1, F2), w2, b2.reshape(E, 1, D))


def _combine_kernel(n_bits, src_ref, dst_ref, len_ref, ys_hbm, x_ref, ls_ref, wt_ref, g2_ref, o_ref,
                    ybuf, sem):
    j = pl.program_id(0)
    tn = x_ref.shape[0]
    rl = ybuf.shape[0]
    ybuf[...] = jnp.zeros_like(ybuf)
    _move_segments(j, src_ref, dst_ref, len_ref, ybuf, ys_hbm, sem, False, n_bits)
    ls = ls_ref[...]
    wt = wt_ref[...]
    ch = SORT_CHUNK
    y = jnp.zeros(x_ref.shape, F32)
    for cidx in range(rl // ch):
        li = lax.broadcasted_iota(jnp.int32, (tn, ch), 1) + cidx * ch
        pw = jnp.zeros((tn, ch), F32)
        for k in range(TOP_K):
            pw = jnp.where(li == ls[:, k:k + 1], wt[:, k:k + 1], pw)
        hi = pw.astype(BF16)
        lo = (pw - hi.astype(F32)).astype(BF16)
        rows = ybuf[cidx * ch:(cidx + 1) * ch, :]
        y = y + _dot(hi, rows) + _dot(lo, rows)
    o_ref[...] = x_ref[...] + g2_ref[0] * y


def _combine(seg_src, seg_dst, seg_len, ys, x2, ls, wts, mod3, l, B, S, tn, n_bits):
    N, D = x2.shape
    nb = S // tn
    base = l * B * 6
    rl = _local_rows(tn)
    grid_spec = pltpu.PrefetchScalarGridSpec(
        num_scalar_prefetch=3,
        grid=(N // tn,),
        in_specs=[pl.BlockSpec(memory_space=pl.ANY),
                  pl.BlockSpec((tn, D), lambda i, a, b, c: (i, 0)),
                  pl.BlockSpec((tn, LANES), lambda i, a, b, c: (i, 0)),
                  pl.BlockSpec((tn, LANES), lambda i, a, b, c: (i, 0)),
                  pl.BlockSpec((1, 1, D), lambda i, a, b, c: (base + (i // nb) * 6 + 5, 0, 0))],
        out_specs=pl.BlockSpec((tn, D), lambda i, a, b, c: (i, 0)),
        scratch_shapes=[pltpu.VMEM((rl, D), ys.dtype), pltpu.SemaphoreType.DMA(())],
    )
    return pl.pallas_call(
        functools.partial(_combine_kernel, n_bits),
        grid_spec=grid_spec,
        out_shape=jax.ShapeDtypeStruct((N, D), F32),
        compiler_params=_cparams(("arbitrary",)),
        name="moe_combine",
    )(seg_src, seg_dst, seg_len, ys, x2, ls, wts, mod3)


def _tile_rows(S, want):
    t = min(S, want)
    while S % t:
        t //= 2
    return t


def _prep_in_weights(w_in_l):
    HW = HGRN_HEADS * HGRN_DIM
    o = 0
    wh = w_in_l[:, o:o + 4 * HW]; o += 4 * HW
    wql = w_in_l[:, o:o + MLA_Q_RANK]; o += MLA_Q_RANK
    wkvl = w_in_l[:, o:o + MLA_KV_RANK]; o += MLA_KV_RANK
    wkr = w_in_l[:, o:o + MLA_ROPE]; o += MLA_ROPE
    wp = w_in_l[:, o:o + HW]; o += HW
    wg = w_in_l[:, o:]
    D = w_in_l.shape[0]
    wkr_p = jnp.zeros((D, LANES), w_in_l.dtype).at[:, MLA_NOPE:MLA_NOPE + MLA_ROPE].set(wkr)
    wm = jnp.concatenate([wql, wkvl, wkr_p, _rope_partner(wkr_p)], axis=1)
    return wg.astype(BF16), wh.astype(BF16), wm.astype(BF16), wp.astype(BF16)


def _rope_partner(w):
    half = MLA_ROPE // 2
    w3 = w.reshape(w.shape[:-1] + (-1, LANES))
    first = w3[..., MLA_NOPE:MLA_NOPE + half]
    second = w3[..., MLA_NOPE + half:MLA_NOPE + MLA_ROPE]
    out = jnp.concatenate([jnp.zeros_like(w3[..., :MLA_NOPE]), second, first,
                           jnp.zeros_like(w3[..., MLA_NOPE + MLA_ROPE:])], axis=-1)
    return out.reshape(w.shape)


def _pad_heads(w, n_heads, width):
    K = w.shape[0]
    w3 = w.reshape(K, n_heads, width)
    return jnp.pad(w3, ((0, 0), (0, 0), (0, LANES - width))).reshape(K, n_heads * LANES)


def kernel(x, c, positions, ada_w, ada_b, norm1_g, norm2_g, w_in, hgrn_lb, hgrn_onorm_g, mla_qlat_g, mla_kvlat_g, w_uq, w_ukv, q_norm_g, k_norm_g, w_pool, pool_scale, w_br_a, w_br_b, w_br_c, w_out, w_router, b_router, w_exp1, b_exp1, w_exp2, b_exp2):
    B, S, D = x.shape
    L = ada_w.shape[0]
    N = B * S
    H = MLA_HEADS
    tr = _tile_rows(S, 512)
    tc = _tile_rows(S, 256)
    ta = _tile_rows(S, 512)
    tm = 512

    mod = _ada(c, ada_w, ada_b)
    mod3 = mod.reshape(L * B * 6, 1, D)

    inv_freq = 1.0 / (ROPE_THETA ** (jnp.arange(0, MLA_ROPE, 2, dtype=F32) / MLA_ROPE))
    freq_row = jnp.zeros((1, LANES), F32).at[0, MLA_NOPE:MLA_NOPE + MLA_ROPE].set(
        jnp.concatenate([inv_freq, inv_freq]))
    posf = positions.astype(F32).reshape(N, 1)
    cos_t, sin_t = _rope_tables(posf, freq_row, tr)

    max_rows = N * TOP_K + (N // tr) * N_EXPERTS * (SEG_ALIGN - 1)
    n_tiles = -(-max_rows // tm) + N_EXPERTS
    n_bits = (tr // SEG_ALIGN).bit_length()
    x2 = x.reshape(N, D)
    for l in range(L):
        wg, wh, wm, wp = _prep_in_weights(w_in[l])
        pg, ph, pm, pp = _inproj(x2, norm1_g[l].reshape(1, D), mod3, l, wg, wh, wm, wp, B, S, tr)

        ya = _hgrn(ph.reshape(B, S, -1), hgrn_lb, hgrn_onorm_g[l].reshape(1, HGRN_DIM), l, B, S, tc)

        wq_f = _pad_heads(w_uq[l], H, MLA_QK)
        wq_p = wq_f.astype(BF16)
        wqr_p = _rope_partner(wq_f).astype(BF16)
        wkv3 = w_ukv[l].reshape(MLA_KV_RANK, H, MLA_NOPE + MLA_V)
        wkv_p = jnp.concatenate(
            [_pad_heads(wkv3[:, :, :MLA_NOPE].reshape(MLA_KV_RANK, H * MLA_NOPE), H, MLA_NOPE),
             wkv3[:, :, MLA_NOPE:].reshape(MLA_KV_RANK, H * MLA_V)], axis=1).astype(BF16)
        qn_p = jnp.pad(q_norm_g[l], (0, LANES - MLA_QK)).reshape(1, LANES)
        kn_p = jnp.pad(k_norm_g[l], (0, LANES - MLA_QK)).reshape(1, LANES)
        q, k, v = _mla_prep(pm, cos_t, sin_t, mla_qlat_g[l].reshape(1, -1), mla_kvlat_g[l].reshape(1, -1),
                            wq_p, wqr_p, wkv_p, qn_p, _rope_partner(qn_p), kn_p, _rope_partner(kn_p),
                            B, S, tr)
        yb = _attention(q, k, v.reshape(B, S, H * MLA_V), B, S, ta)

        yc = _pool(pp.reshape(B, S, -1), w_pool[l], pool_scale[l].reshape(1, -1))

        x2 = _merge(x2, pg, ya.reshape(N, -1), yb.reshape(N, -1), yc.reshape(N, -1),
                    w_br_a[l].astype(BF16), w_br_b[l].astype(BF16), w_br_c[l].astype(BF16),
                    w_out[l].astype(BF16), mod3, l, B, S, tr)

        wr_p = jnp.pad(w_router[l], ((0, 0), (0, LANES - N_EXPERTS)))
        br_p = jnp.concatenate([b_router[l], jnp.full((LANES - N_EXPERTS,), NEG_BIG, F32)]).reshape(1, LANES)
        xl, ls, wts, cnt = _router(x2, norm2_g[l].reshape(1, D), mod3, l, wr_p, br_p, B, S, tr)

        seg = ((cnt[:, 0, :N_EXPERTS].astype(jnp.int32) + SEG_ALIGN - 1) // SEG_ALIGN) * SEG_ALIGN
        seg_src = jnp.cumsum(seg, axis=1) - seg
        group = jnp.sum(seg, axis=0)
        tiles_per = (group + tm - 1) // tm
        tile_end = jnp.cumsum(tiles_per)
        offsets = (tile_end - tiles_per) * tm
        seg_dst = offsets[None, :] + jnp.cumsum(seg, axis=0) - seg
        n_used = tile_end[-1:]
        tile_ids = jnp.minimum(jnp.arange(n_tiles, dtype=jnp.int32), n_used[0] - 1)
        tile_expert = jnp.minimum(
            jnp.sum((tile_end[None, :] <= tile_ids[:, None]).astype(jnp.int32), axis=1), N_EXPERTS - 1)
        all_tiles = jnp.arange(n_tiles, dtype=jnp.int32)
        is_tail = jnp.any((tile_end[None, :] - 1 == all_tiles[:, None]) & (tiles_per[None, :] > 0), axis=1)
        zero_flags = (is_tail | (all_tiles >= n_used[0])).astype(jnp.int32)
        seg_src = seg_src.reshape(-1).astype(jnp.int32)
        seg_dst = seg_dst.reshape(-1).astype(jnp.int32)
        seg_len = (seg // SEG_ALIGN).reshape(-1).astype(jnp.int32)

        xs = _place(seg_src, seg_dst, seg_len, zero_flags, xl, n_tiles * tm, _local_rows(tr), tm, n_bits)
        ys = _experts(tile_expert.astype(jnp.int32), n_used.astype(jnp.int32), xs,
                      w_exp1, b_exp1, w_exp2, b_exp2, l, tm)
        x2 = _combine(seg_src, seg_dst, seg_len, ys, x2, ls, wts, mod3, l, B, S, tr, n_bits)
    return x2.reshape(B, S, D)
```

```python
import functools
import math

import jax
import jax.numpy as jnp
from jax import lax
from jax.experimental import pallas as pl
from jax.experimental.pallas import tpu as pltpu

F32 = jnp.float32
BF16 = jnp.bfloat16

LANES = 128
NORM_EPS = 1e-6
HGRN_HEADS = 4
HGRN_DIM = 128
HGRN_CHUNK = 32
HGRN_ROWS = 4
MLA_HEADS = 8
MLA_NOPE = 64
MLA_ROPE = 32
MLA_QK = MLA_NOPE + MLA_ROPE
MLA_V = 64
MLA_Q_RANK = 384
MLA_KV_RANK = 256
ROPE_THETA = 10000.0
POOL_WINDOWS = (2, 4, 8, 16)
N_EXPERTS = 32
TOP_K = 4
SWIGLU_LIMIT = 7.0
SWIGLU_ALPHA = 1.702
NEG_BIG = -1e30
LOG2E = 1.4426950408889634
FACTOR_LIMIT = 100.0
SEG_ALIGN = 16
SORT_CHUNK = 256

VMEM_LIMIT = 56 * 1024 * 1024


def _cparams(sem):
    return pltpu.CompilerParams(dimension_semantics=sem, vmem_limit_bytes=VMEM_LIMIT)


def _dot(a, b):
    return jnp.dot(a, b, preferred_element_type=F32)


def _dot_nt(a, b):
    return lax.dot_general(a, b, (((1,), (1,)), ((), ())), preferred_element_type=F32)


def _dot_tn(a, b):
    return lax.dot_general(a, b, (((0,), (0,)), ((), ())), preferred_element_type=F32)


def _split3(x):
    hi = x.astype(BF16)
    r = x - hi.astype(F32)
    mid = r.astype(BF16)
    lo = (r - mid.astype(F32)).astype(BF16)
    return hi, mid, lo


def _sigmoid(x):
    return 1.0 / (1.0 + jnp.exp(-x))


def _ada_kernel(c_ref, w_ref, b_ref, o_ref):
    c = c_ref[...]
    cond = (c * _sigmoid(c)).astype(BF16)
    o_ref[0] = _dot(cond, w_ref[0].astype(BF16)) + b_ref[0]


def _ada(c, ada_w, ada_b):
    L, D, W = ada_w.shape
    B = c.shape[0]
    tn = D
    return pl.pallas_call(
        _ada_kernel,
        grid=(L, W // tn),
        in_specs=[
            pl.BlockSpec((B, D), lambda l, j: (0, 0)),
            pl.BlockSpec((1, D, tn), lambda l, j: (l, 0, j)),
            pl.BlockSpec((1, 1, tn), lambda l, j: (l, 0, j)),
        ],
        out_specs=pl.BlockSpec((1, B, tn), lambda l, j: (l, 0, j)),
        out_shape=jax.ShapeDtypeStruct((L, B, W), F32),
        compiler_params=_cparams(("parallel", "parallel")),
        name="ada_mod",
    )(c, ada_w, ada_b.reshape(L, 1, W))


def _inproj_kernel(x_ref, g_ref, sh_ref, sc_ref, wg_ref, wh_ref, wm_ref, wp_ref,
                   og_ref, oh_ref, om_ref, op_ref):
    x = x_ref[...]
    ms = jnp.mean(x * x, axis=-1, keepdims=True)
    h = x * lax.rsqrt(ms + NORM_EPS) * g_ref[...]
    h = h * (1.0 + sc_ref[0]) + sh_ref[0]
    hb = h.astype(BF16)
    cw = 512
    for w_ref, o_ref in ((wg_ref, og_ref), (wh_ref, oh_ref), (wm_ref, om_ref), (wp_ref, op_ref)):
        width = w_ref.shape[1]
        step = cw if width % cw == 0 else width
        for j in range(0, width, step):
            o_ref[:, j:j + step] = _dot(hb, w_ref[:, j:j + step]).astype(o_ref.dtype)


def _inproj(x2, g, mod3, l, wg, wh, wm, wp, B, S, tr):
    N, D = x2.shape
    nb = S // tr
    base = l * B * 6

    def modspec(j):
        return pl.BlockSpec((1, 1, D), lambda i: (base + (i // nb) * 6 + j, 0, 0))

    def wspec(w):
        return pl.BlockSpec(w.shape, lambda i: (0, 0), pipeline_mode=pl.Buffered(1))

    outs = (
        jax.ShapeDtypeStruct((N, wg.shape[1]), BF16),
        jax.ShapeDtypeStruct((N, wh.shape[1]), F32),
        jax.ShapeDtypeStruct((N, wm.shape[1]), F32),
        jax.ShapeDtypeStruct((N, wp.shape[1]), F32),
    )
    return pl.pallas_call(
        _inproj_kernel,
        grid=(N // tr,),
        in_specs=[
            pl.BlockSpec((tr, D), lambda i: (i, 0)),
            pl.BlockSpec((1, D), lambda i: (0, 0)),
            modspec(0), modspec(1),
            wspec(wg), wspec(wh), wspec(wm), wspec(wp),
        ],
        out_specs=tuple(pl.BlockSpec((tr, o.shape[1]), lambda i: (i, 0)) for o in outs),
        out_shape=outs,
        compiler_params=_cparams(("parallel",)),
        name="in_proj",
    )(x2, g, mod3, mod3, wg, wh, wm, wp)


def _hgrn_kernel(layer, n_chunks, q_ref, f_ref, i_ref, g_ref, lb_ref, on_ref, o_ref,
                 st_ref, stb_ref, oi_ref):
    C, DH, H = HGRN_CHUNK, HGRN_DIM, HGRN_HEADS

    @pl.when(pl.program_id(1) == 0)
    def _():
        st_ref[...] = jnp.zeros_like(st_ref)
        stb_ref[...] = jnp.zeros_like(stb_ref)

    lbr = lb_ref[...]
    e = jnp.exp(lbr - jnp.max(lbr, axis=0, keepdims=True))
    p = e / jnp.sum(e, axis=0, keepdims=True)
    lb = jnp.zeros((1, lbr.shape[1]), F32)
    for j in range(1, layer + 1):
        lb = lb + p[j:j + 1]
    log_lb = jnp.log(lb)
    log_1mlb = jnp.log1p(-lb)
    one_m_lb = 1.0 - lb
    onorm = jnp.tile(on_ref[...], (1, H))

    row_c = lax.broadcasted_iota(jnp.int32, (C, C), 0)
    col_c = lax.broadcasted_iota(jnp.int32, (C, C), 1)
    tri = (col_c <= row_c).astype(BF16)
    rs = lax.broadcasted_iota(jnp.int32, (H * C, H * C), 0)
    cs = lax.broadcasted_iota(jnp.int32, (H * C, H * C), 1)
    same_head_causal = jnp.logical_and(cs <= rs, cs >= rs - rs % C)
    row_1 = lax.broadcasted_iota(jnp.int32, (C, 1), 0)
    heads = [slice(h * DH, (h + 1) * DH) for h in range(H)]

    def stack(x):
        return jnp.concatenate([x[:, hs] for hs in heads], axis=0)

    def unstack(x):
        return jnp.concatenate([x[h * C:(h + 1) * C] for h in range(H)], axis=1)

    def prepare(r, sl):
        q = q_ref[r, sl, :]
        z = f_ref[r, sl, :]
        v = i_ref[r, sl, :]
        ez = jnp.exp(-jnp.abs(z))
        log_sig = jnp.minimum(z, 0.0) - jnp.log1p(ez)
        if layer == 0:
            lf = log_sig
            kk = jnp.where(z >= 0.0, ez, 1.0) / (1.0 + ez)
        else:
            a = log_1mlb + log_sig
            m = jnp.maximum(a, log_lb)
            lf = m + jnp.log(jnp.exp(a - m) + jnp.exp(log_lb - m))
            kk = one_m_lb * jnp.where(z >= 0.0, ez, 1.0) / (1.0 + ez)
        l1, l2, l3 = _split3(lf)
        b = (_dot(tri, l1) + _dot(tri, l2) + _dot(tri, l3)) * LOG2E
        qs = (q * jnp.exp2(b)).astype(BF16)
        o_inter = _dot_nt(qs, stb_ref[r])
        return q, v, kk, b, qs, v.astype(BF16), o_inter

    def intra_factored(r, prepared):
        _, _, kk, b, qs, vb, _ = prepared
        ks = (kk * jnp.exp2(-b)).astype(BF16)
        att = _dot_nt(stack(qs), stack(ks))
        att = jnp.where(same_head_causal, att, 0.0)
        oi_ref[r] = unstack(_dot(att.astype(BF16), stack(vb)))

    def intra_termwise(r, prepared):
        q, v, kk, b, _, _, _ = prepared
        for hs in heads:
            bh, qh, kh, vh = b[:, hs], q[:, hs], kk[:, hs], v[:, hs]
            acc = jnp.zeros((C, DH), F32)
            for s in range(C):
                es = jnp.exp2(jnp.minimum(bh - bh[s:s + 1], 0.0))
                col = jnp.sum(qh * es * kh[s:s + 1], axis=-1, keepdims=True)
                acc = acc + jnp.where(row_1 >= s, col, 0.0) * vh[s:s + 1]
            oi_ref[r, :, hs] = acc

    def finish(r, sl, prepared):
        _, _, kk, b, _, vb, o_inter = prepared
        bl = b[C - 1:C]
        o = oi_ref[r] + o_inter
        kd = (kk * jnp.exp2(bl - b)).astype(BF16)
        update = _dot_tn(vb, kd)
        decay = jnp.exp2(bl)
        normed = []
        for hs in heads:
            new = st_ref[r, hs, hs] * decay[:, hs] + update[hs, hs]
            st_ref[r, hs, hs] = new
            stb_ref[r, hs, hs] = new.astype(BF16)
            oh = o[:, hs]
            normed.append(oh * lax.rsqrt(jnp.mean(oh * oh, axis=-1, keepdims=True) + NORM_EPS))
        g = g_ref[r, sl, :]
        y = jnp.concatenate(normed, axis=1) * onorm * (g * _sigmoid(g))
        o_ref[r, sl, :] = y.astype(o_ref.dtype)

    def chunk(ci, carry):
        sl = pl.ds(pl.multiple_of(ci * C, C), C)
        rows = range(q_ref.shape[0])
        prepared = [prepare(r, sl) for r in rows]
        span = -prepared[0][3][C - 1:C]
        for p in prepared[1:]:
            span = jnp.maximum(span, -p[3][C - 1:C])
        factor_ok = jnp.max(span) < FACTOR_LIMIT

        @pl.when(factor_ok)
        def _():
            for r in rows:
                intra_factored(r, prepared[r])

        @pl.when(jnp.logical_not(factor_ok))
        def _():
            for r in rows:
                intra_termwise(r, prepared[r])

        for r in rows:
            finish(r, sl, prepared[r])
        return carry

    lax.fori_loop(0, n_chunks, chunk, 0)


def _hgrn(ph, hgrn_lb, onorm_g, layer, B, S, tc):
    L = hgrn_lb.shape[0]
    W = HGRN_HEADS * HGRN_DIM

    nb = math.gcd(B, HGRN_ROWS)

    def spec(off):
        return pl.BlockSpec((nb, tc, W), lambda b, t: (b, t, off))

    return pl.pallas_call(
        functools.partial(_hgrn_kernel, layer, tc // HGRN_CHUNK),
        grid=(B // nb, S // tc),
        in_specs=[spec(0), spec(1), spec(2), spec(3),
                  pl.BlockSpec((L, W), lambda b, t: (0, 0)),
                  pl.BlockSpec((1, HGRN_DIM), lambda b, t: (0, 0))],
        out_specs=pl.BlockSpec((nb, tc, W), lambda b, t: (b, t, 0)),
        out_shape=jax.ShapeDtypeStruct((B, S, W), BF16),
        scratch_shapes=[pltpu.VMEM((nb, W, W), F32), pltpu.VMEM((nb, W, W), BF16),
                        pltpu.VMEM((nb, HGRN_CHUNK, W), F32)],
        compiler_params=_cparams(("parallel", "arbitrary")),
        name="hgrn2",
    )(ph, ph, ph, ph, hgrn_lb, onorm_g)


def _rope_table_kernel(pos_ref, fr_ref, cos_ref, sin_ref):
    ang = pos_ref[...] * fr_ref[...]
    lane = lax.broadcasted_iota(jnp.int32, ang.shape, 1)
    sn = jnp.sin(ang)
    cos_ref[...] = jnp.cos(ang)
    sin_ref[...] = jnp.where(lane < MLA_NOPE + MLA_ROPE // 2, -sn, sn)


def _rope_tables(posf, freq_row, tr):
    N = posf.shape[0]
    out = jax.ShapeDtypeStruct((N, LANES), F32)
    return pl.pallas_call(
        _rope_table_kernel,
        grid=(N // tr,),
        in_specs=[pl.BlockSpec((tr, 1), lambda i: (i, 0)), pl.BlockSpec((1, LANES), lambda i: (0, 0))],
        out_specs=(pl.BlockSpec((tr, LANES), lambda i: (i, 0)), pl.BlockSpec((tr, LANES), lambda i: (i, 0))),
        out_shape=(out, out),
        compiler_params=_cparams(("parallel",)),
        name="rope_tables",
    )(posf, freq_row)


def _mla_prep_kernel(pm_ref, cos_ref, sin_ref, qg_ref, kvg_ref, wq_ref, wqr_ref, wkv_ref,
                     qn_ref, qnr_ref, kn_ref, knr_ref, q_ref, k_ref, v_ref):
    pm = pm_ref[...]
    o = 0
    ql = pm[:, o:o + MLA_Q_RANK]; o += MLA_Q_RANK
    kvl = pm[:, o:o + MLA_KV_RANK]; o += MLA_KV_RANK
    kr = pm[:, o:o + LANES]; o += LANES
    krr = pm[:, o:o + LANES]

    def rms(x, g):
        return x * lax.rsqrt(jnp.mean(x * x, axis=-1, keepdims=True) + NORM_EPS) * g

    qlb = rms(ql, qg_ref[...]).astype(BF16)
    qa = _dot(qlb, wq_ref[...])
    qr = _dot(qlb, wqr_ref[...])
    kva = _dot(rms(kvl, kvg_ref[...]).astype(BF16), wkv_ref[...])
    v_ref[...] = kva[:, MLA_HEADS * LANES:].astype(v_ref.dtype)

    cs = cos_ref[...]
    sn = sin_ref[...]
    scale = 1.0 / math.sqrt(MLA_QK)
    q_cos = qn_ref[...] * cs * scale
    q_sin = qnr_ref[...] * sn * scale
    k_cos = kn_ref[...] * cs
    k_sin = knr_ref[...] * sn
    k_rot = krr * k_sin

    def inv_rms(xh):
        return lax.rsqrt(jnp.sum(xh * xh, axis=-1, keepdims=True) * (1.0 / MLA_QK) + NORM_EPS)

    for h in range(MLA_HEADS):
        hs = slice(h * LANES, (h + 1) * LANES)
        qh = qa[:, hs]
        q_ref[0, h] = ((qh * q_cos + qr[:, hs] * q_sin) * inv_rms(qh)).astype(q_ref.dtype)
        kh = kva[:, hs] + kr
        k_ref[0, h] = ((kh * k_cos + k_rot) * inv_rms(kh)).astype(k_ref.dtype)


def _mla_prep(pm, cos_t, sin_t, qlat_g, kvlat_g, wq, wqr, wkv, qn, qnr, kn, knr, B, S, tr):
    N = pm.shape[0]
    nb = S // tr
    H = MLA_HEADS
    full = lambda a: pl.BlockSpec(a.shape, lambda i: (0,) * a.ndim)
    return pl.pallas_call(
        _mla_prep_kernel,
        grid=(N // tr,),
        in_specs=[pl.BlockSpec((tr, pm.shape[1]), lambda i: (i, 0)),
                  pl.BlockSpec((tr, LANES), lambda i: (i, 0)),
                  pl.BlockSpec((tr, LANES), lambda i: (i, 0)),
                  full(qlat_g), full(kvlat_g), full(wq), full(wqr), full(wkv),
                  full(qn), full(qnr), full(kn), full(knr)],
        out_specs=(pl.BlockSpec((1, H, tr, LANES), lambda i: (i // nb, 0, i % nb, 0)),
                   pl.BlockSpec((1, H, tr, LANES), lambda i: (i // nb, 0, i % nb, 0)),
                   pl.BlockSpec((tr, H * MLA_V), lambda i: (i, 0))),
        out_shape=(jax.ShapeDtypeStruct((B, H, S, LANES), BF16),
                   jax.ShapeDtypeStruct((B, H, S, LANES), BF16),
                   jax.ShapeDtypeStruct((N, H * MLA_V), BF16)),
        compiler_params=_cparams(("parallel",)),
        name="mla_prep",
    )(pm, cos_t, sin_t, qlat_g, kvlat_g, wq, wqr, wkv, qn, qnr, kn, knr)


def _attn_kernel(q_ref, k_ref, v_ref, o_ref, m_ref, l_ref, acc_ref):
    qi = pl.program_id(2)
    T = q_ref.shape[2]

    m_ref[...] = jnp.full_like(m_ref, NEG_BIG)
    l_ref[...] = jnp.zeros_like(l_ref)
    acc_ref[...] = jnp.zeros_like(acc_ref)

    def step(ki, masked):
        rows = pl.ds(pl.multiple_of(ki * T, T), T)
        vblk = v_ref[0, rows, :]
        lane = lax.broadcasted_iota(jnp.int32, vblk.shape, 1)
        if masked:
            row = lax.broadcasted_iota(jnp.int32, (T, T), 0)
            col = lax.broadcasted_iota(jnp.int32, (T, T), 1)
            keep = col <= row
        for h in range(2):
            s = _dot_nt(q_ref[0, h], k_ref[0, h, rows, :])
            if masked:
                s = jnp.where(keep, s, NEG_BIG)
            m_old = m_ref[h]
            m_new = jnp.maximum(m_old, jnp.max(s, axis=-1, keepdims=True))
            alpha = jnp.exp(m_old - m_new)
            p = jnp.exp(s - jnp.tile(m_new, (1, T // LANES)))
            l_ref[h] = alpha * l_ref[h] + jnp.sum(p, axis=-1, keepdims=True)
            vh = jnp.where((lane < MLA_V) if h == 0 else (lane >= MLA_V), vblk, jnp.zeros_like(vblk))
            acc_ref[h] = alpha * acc_ref[h] + _dot(p.astype(BF16), vh)
            m_ref[h] = m_new

    def below_diagonal(ki, c):
        step(ki, False)
        return c

    lax.fori_loop(0, qi, below_diagonal, 0)
    step(qi, True)
    o_ref[0] = (acc_ref[0] / l_ref[0] + acc_ref[1] / l_ref[1]).astype(o_ref.dtype)


def _attention(q, k, v3, B, S, T):
    H = MLA_HEADS
    n = S // T
    return pl.pallas_call(
        _attn_kernel,
        grid=(B, H // 2, n),
        in_specs=[pl.BlockSpec((1, 2, T, LANES), lambda b, hp, qi: (b, hp, qi, 0)),
                  pl.BlockSpec((1, 2, S, LANES), lambda b, hp, qi: (b, hp, 0, 0)),
                  pl.BlockSpec((1, S, LANES), lambda b, hp, qi: (b, 0, hp))],
        out_specs=pl.BlockSpec((1, T, LANES), lambda b, hp, qi: (b, qi, hp)),
        out_shape=jax.ShapeDtypeStruct((B, S, H * MLA_V), BF16),
        scratch_shapes=[pltpu.VMEM((2, T, LANES), F32), pltpu.VMEM((2, T, LANES), F32),
                        pltpu.VMEM((2, T, LANES), F32)],
        compiler_params=_cparams(("parallel", "parallel", "arbitrary")),
        name="mla_attention",
    )(q, k, v3)


def _pool_kernel(u_ref, w_ref, sc_ref, o_ref):
    S = u_ref.shape[1]
    row = lax.broadcasted_iota(jnp.int32, (S, LANES), 0)
    t1 = (row + 1).astype(F32)
    for gi, win in enumerate(POOL_WINDOWS):
        u = u_ref[0, :, gi * LANES:(gi + 1) * LANES]
        s = u
        k = 1
        while k < win:
            s = s + jnp.where(row >= k, pltpu.roll(s, k, 0), 0.0)
            k *= 2
        pooled = s / jnp.minimum(t1, float(win))
        mixed = _dot((pooled - u).astype(BF16), w_ref[gi].astype(BF16))
        o_ref[0, :, gi * LANES:(gi + 1) * LANES] = (
            mixed * sc_ref[:, gi * LANES:(gi + 1) * LANES]).astype(o_ref.dtype)


def _pool(pp3, w_pool, scale):
    B, S, W = pp3.shape
    return pl.pallas_call(
        _pool_kernel,
        grid=(B,),
        in_specs=[pl.BlockSpec((1, S, W), lambda b: (b, 0, 0)),
                  pl.BlockSpec(w_pool.shape, lambda b: (0, 0, 0)),
                  pl.BlockSpec((1, W), lambda b: (0, 0))],
        out_specs=pl.BlockSpec((1, S, W), lambda b: (b, 0, 0)),
        out_shape=jax.ShapeDtypeStruct((B, S, W), BF16),
        compiler_params=_cparams(("parallel",)),
        name="pool_mixer",
    )(pp3, w_pool, scale)


def _merge_kernel(x_ref, gt_ref, ya_ref, yb_ref, yc_ref, wa_ref, wb_ref, wc_ref, wo_ref, g1_ref, o_ref):
    D = x_ref.shape[1]
    merged = None
    for j, (y_ref, w_ref) in enumerate(((ya_ref, wa_ref), (yb_ref, wb_ref), (yc_ref, wc_ref))):
        y = _dot(y_ref[...], w_ref[...])
        gate = _sigmoid(gt_ref[:, j * D:(j + 1) * D].astype(F32))
        merged = gate * y if merged is None else merged + gate * y
    o_ref[...] = x_ref[...] + g1_ref[0] * _dot(merged.astype(BF16), wo_ref[...])


def _merge(x2, gates, ya, yb, yc, wa, wb, wc, wo, mod3, l, B, S, tr):
    N, D = x2.shape
    nb = S // tr
    base = l * B * 6
    full = lambda a: pl.BlockSpec(a.shape, lambda i: (0,) * a.ndim)
    rows = lambda a: pl.BlockSpec((tr, a.shape[1]), lambda i: (i, 0))
    return pl.pallas_call(
        _merge_kernel,
        grid=(N // tr,),
        in_specs=[rows(x2), rows(gates), rows(ya), rows(yb), rows(yc),
                  full(wa), full(wb), full(wc), full(wo),
                  pl.BlockSpec((1, 1, D), lambda i: (base + (i // nb) * 6 + 2, 0, 0))],
        out_specs=rows(x2),
        out_shape=jax.ShapeDtypeStruct((N, D), F32),
        compiler_params=_cparams(("parallel",)),
        name="merge_out",
    )(x2, gates, ya, yb, yc, wa, wb, wc, wo, mod3)


def _router_kernel(x_ref, g_ref, sh_ref, sc_ref, wr_ref, br_ref, xl_ref, ls_ref, wt_ref, cnt_ref):
    tr = x_ref.shape[0]
    rl = xl_ref.shape[0]

    x = x_ref[...]
    ms = jnp.mean(x * x, axis=-1, keepdims=True)
    h = x * lax.rsqrt(ms + NORM_EPS) * g_ref[...]
    h = h * (1.0 + sc_ref[0]) + sh_ref[0]
    h1, h2, _ = _split3(h)
    w1, w2, _ = _split3(wr_ref[...])
    logits = _dot(h1, w1) + (_dot(h1, w2) + _dot(h2, w1)) + br_ref[...]
    lane = lax.broadcasted_iota(jnp.int32, logits.shape, 1)
    work = logits
    vals, ids = [], []
    for _k in range(TOP_K):
        m = jnp.max(work, axis=-1, keepdims=True)
        sel = jnp.min(jnp.where(work == m, lane, LANES), axis=-1, keepdims=True)
        vals.append(m)
        ids.append(sel)
        work = jnp.where(lane == sel, -jnp.inf, work)
    es = [jnp.exp(v - vals[0]) for v in vals]
    den = es[0] + es[1] + es[2] + es[3]
    onehot = jnp.zeros(logits.shape, F32)
    wt_out = jnp.zeros(logits.shape, F32)
    for k in range(TOP_K):
        onehot = onehot + (lane == ids[k]).astype(F32)
        wt_out = jnp.where(lane == k, es[k] / den, wt_out)
    wt_ref[...] = wt_out

    r = lax.broadcasted_iota(jnp.int32, (tr, tr), 0)
    c = lax.broadcasted_iota(jnp.int32, (tr, tr), 1)
    strict = (c < r).astype(BF16)
    before = _dot(strict, onehot.astype(BF16))
    cnt = jnp.sum(onehot, axis=0, keepdims=True)
    cnt_ref[0] = jnp.broadcast_to(cnt, cnt_ref.shape[1:])
    units = jnp.floor((cnt + (SEG_ALIGN - 1.0)) * (1.0 / SEG_ALIGN))
    er = lax.broadcasted_iota(jnp.int32, (LANES, LANES), 0)
    ec = lax.broadcasted_iota(jnp.int32, (LANES, LANES), 1)
    upper = (er < ec).astype(BF16)
    seg_start = _dot(jnp.broadcast_to(units, (8, LANES)).astype(BF16), upper)[0:1] * float(SEG_ALIGN)
    pos = before + seg_start
    ls_out = jnp.full(logits.shape, -1.0, F32)
    for k in range(TOP_K):
        lk = jnp.sum(jnp.where(lane == ids[k], pos, 0.0), axis=-1, keepdims=True)
        ls_out = jnp.where(lane == k, lk, ls_out)
    ls_ref[...] = ls_out.astype(jnp.int32)

    ls_t = jnp.transpose(ls_out)
    targets = [ls_t[k:k + 1, :].astype(jnp.int32) for k in range(TOP_K)]
    hb = h.astype(BF16)
    ch = SORT_CHUNK
    for cidx in range(rl // ch):
        ri = lax.broadcasted_iota(jnp.int32, (ch, tr), 0) + cidx * ch
        hit = ri == targets[0]
        for k in range(1, TOP_K):
            hit = jnp.logical_or(hit, ri == targets[k])
        perm = jnp.where(hit, 1.0, 0.0).astype(BF16)
        xl_ref[cidx * ch:(cidx + 1) * ch, :] = _dot(perm, hb).astype(xl_ref.dtype)


def _local_rows(tb):
    return TOP_K * tb + N_EXPERTS * SEG_ALIGN


def _router(x2, g, mod3, l, wr_p, br_p, B, S, tr):
    N, D = x2.shape
    nb = S // tr
    base = l * B * 6
    rl = _local_rows(tr)
    rows = lambda w: pl.BlockSpec((tr, w), lambda i: (i, 0))

    def modspec(j):
        return pl.BlockSpec((1, 1, D), lambda i: (base + (i // nb) * 6 + j, 0, 0))

    return pl.pallas_call(
        _router_kernel,
        grid=(N // tr,),
        in_specs=[rows(D), pl.BlockSpec((1, D), lambda i: (0, 0)), modspec(3), modspec(4),
                  pl.BlockSpec(wr_p.shape, lambda i: (0, 0)), pl.BlockSpec((1, LANES), lambda i: (0, 0))],
        out_specs=(pl.BlockSpec((rl, D), lambda i: (i, 0)), rows(LANES), rows(LANES),
                   pl.BlockSpec((1, 8, LANES), lambda i: (i, 0, 0))),
        out_shape=(jax.ShapeDtypeStruct((N // tr * rl, D), BF16),
                   jax.ShapeDtypeStruct((N, LANES), jnp.int32),
                   jax.ShapeDtypeStruct((N, LANES), F32),
                   jax.ShapeDtypeStruct((N // tr, 8, LANES), F32)),
        compiler_params=_cparams(("parallel",)),
        name="router",
    )(x2, g, mod3, mod3, wr_p, br_p)


def _segment_copies(j, e, src_ref, dst_ref, len_ref, local_ref, global_ref, sem, to_global, n_bits):
    base = j * N_EXPERTS + e
    units = len_ref[base]
    s0 = src_ref[base]
    d0 = dst_ref[base]
    copies = []
    for bit in range(n_bits):
        rows = SEG_ALIGN << bit
        done = ((units >> (bit + 1)) << (bit + 1)) * SEG_ALIGN
        loc = local_ref.at[pl.ds(pl.multiple_of(s0 + done, SEG_ALIGN), rows)]
        glo = global_ref.at[pl.ds(pl.multiple_of(d0 + done, SEG_ALIGN), rows)]
        cp = pltpu.make_async_copy(loc, glo, sem) if to_global else pltpu.make_async_copy(glo, loc, sem)
        copies.append((((units >> bit) & 1) == 1, cp))
    return copies


def _each_segment_copy(j, src_ref, dst_ref, len_ref, local_ref, global_ref, sem, to_global, n_bits, action):
    def per_expert(e, c):
        for pred, cp in _segment_copies(j, e, src_ref, dst_ref, len_ref, local_ref, global_ref,
                                        sem, to_global, n_bits):
            @pl.when(pred)
            def _():
                action(cp)
        return c
    lax.fori_loop(0, N_EXPERTS, per_expert, 0)


def _move_segments(j, src_ref, dst_ref, len_ref, local_ref, global_ref, sem, to_global, n_bits):
    args = (j, src_ref, dst_ref, len_ref, local_ref, global_ref, sem, to_global, n_bits)
    _each_segment_copy(*args, lambda cp: cp.start())
    _each_segment_copy(*args, lambda cp: cp.wait())


def _place_kernel(n_bits, src_ref, dst_ref, len_ref, zflag_ref, xl_ref, xs_hbm, zbuf, sem):
    j = pl.program_id(0)
    tm = zbuf.shape[0]
    n_tiles = xs_hbm.shape[0] // tm

    @pl.when(j == 0)
    def _():
        zbuf[...] = jnp.zeros_like(zbuf)

        def zero_copy(t):
            return pltpu.make_async_copy(zbuf, xs_hbm.at[pl.ds(pl.multiple_of(t * tm, tm), tm)], sem)

        def zstart(t, c):
            @pl.when(zflag_ref[t] > 0)
            def _():
                zero_copy(t).start()
            return c

        def zwait(t, c):
            @pl.when(zflag_ref[t] > 0)
            def _():
                zero_copy(t).wait()
            return c

        lax.fori_loop(0, n_tiles, zstart, 0)
        lax.fori_loop(0, n_tiles, zwait, 0)

    _move_segments(j, src_ref, dst_ref, len_ref, xl_ref, xs_hbm, sem, True, n_bits)


def _place(seg_src, seg_dst, seg_len, zero_flags, xl, n_rows, rl, tm, n_bits):
    D = xl.shape[1]
    grid_spec = pltpu.PrefetchScalarGridSpec(
        num_scalar_prefetch=4,
        grid=(xl.shape[0] // rl,),
        in_specs=[pl.BlockSpec((rl, D), lambda j, a, b, c, d: (j, 0))],
        out_specs=pl.BlockSpec(memory_space=pl.ANY),
        scratch_shapes=[pltpu.VMEM((tm, D), xl.dtype), pltpu.SemaphoreType.DMA(())],
    )
    return pl.pallas_call(
        functools.partial(_place_kernel, n_bits),
        grid_spec=grid_spec,
        out_shape=jax.ShapeDtypeStruct((n_rows, D), xl.dtype),
        compiler_params=_cparams(("arbitrary",)),
        name="moe_place",
    )(seg_src, seg_dst, seg_len, zero_flags, xl)


def _expert_kernel(te_ref, nu_ref, x_ref, w1_ref, b1_ref, w2_ref, b2_ref, o_ref, w1b, w2b):
    t = pl.program_id(0)
    F = w2_ref.shape[1]

    @pl.when(t < nu_ref[0])
    def _():
        @pl.when(jnp.logical_or(t == 0, te_ref[t] != te_ref[jnp.maximum(t - 1, 0)]))
        def _():
            w1b[...] = w1_ref[0].astype(BF16)
            w2b[...] = w2_ref[0].astype(BF16)

        a = _dot(x_ref[...], w1b[...]) + b1_ref[0]
        glu = jnp.minimum(a[:, :F], SWIGLU_LIMIT)
        lin = jnp.clip(a[:, F:], -SWIGLU_LIMIT, SWIGLU_LIMIT)
        act = glu * _sigmoid(SWIGLU_ALPHA * glu) * (lin + 1.0)
        o_ref[...] = (_dot(act.astype(BF16), w2b[...]) + b2_ref[0]).astype(o_ref.dtype)

    @pl.when(t >= nu_ref[0])
    def _():
        o_ref[...] = jnp.zeros_like(o_ref)


def _experts(tile_expert, n_used, xs, w1_all, b1_all, w2_all, b2_all, l, tm):
    n_tiles = tile_expert.shape[0]
    D = xs.shape[1]
    L, E, _, F2 = w1_all.shape
    F = w2_all.shape[2]
    w1 = w1_all.reshape(L * E, D, F2)
    w2 = w2_all.reshape(L * E, F, D)
    b1 = b1_all.reshape(L * E, F2)
    b2 = b2_all.reshape(L * E, D)
    E0 = l * E
    E = L * E
    grid_spec = pltpu.PrefetchScalarGridSpec(
        num_scalar_prefetch=2,
        grid=(n_tiles,),
        in_specs=[pl.BlockSpec((tm, D), lambda t, te, nu: (jnp.minimum(t, nu[0] - 1), 0)),
                  pl.BlockSpec((1, D, F2), lambda t, te, nu: (E0 + te[t], 0, 0)),
                  pl.BlockSpec((1, 1, F2), lambda t, te, nu: (E0 + te[t], 0, 0)),
                  pl.BlockSpec((1, F, D), lambda t, te, nu: (E0 + te[t], 0, 0)),
                  pl.BlockSpec((1, 1, D), lambda t, te, nu: (E0 + te[t], 0, 0))],
        out_specs=pl.BlockSpec((tm, D), lambda t, te, nu: (t, 0)),
        scratch_shapes=[pltpu.VMEM((D, F2), BF16), pltpu.VMEM((F, D), BF16)],
    )
    return pl.pallas_call(
        _expert_kernel,
        grid_spec=grid_spec,
        out_shape=jax.ShapeDtypeStruct((n_tiles * tm, D), BF16),
        compiler_params=_cparams(("arbitrary",)),
        name="expert_mlp",
    )(tile_expert, n_used, xs, w1, b1.reshape(E, 1, F2), w2, b2.reshape(E, 1, D))


def _combine_kernel(n_bits, src_ref, dst_ref, len_ref, ys_hbm, x_ref, ls_ref, wt_ref, g2_ref, o_ref,
                    ybuf, sem):
    j = pl.program_id(0)
    tn = x_ref.shape[0]
    rl = ybuf.shape[1]
    slot = j % 2

    def fetch(block, buf_slot, action):
        _each_segment_copy(block, src_ref, dst_ref, len_ref, ybuf.at[buf_slot], ys_hbm,
                           sem.at[buf_slot], False, n_bits, action)

    def start_fetch(block, buf_slot):
        ybuf[buf_slot] = jnp.zeros((rl, ybuf.shape[2]), ybuf.dtype)
        fetch(block, buf_slot, lambda cp: cp.start())

    @pl.when(j == 0)
    def _():
        start_fetch(j, slot)

    @pl.when(j + 1 < pl.num_programs(0))
    def _():
        start_fetch(j + 1, 1 - slot)

    fetch(j, slot, lambda cp: cp.wait())
    ls = ls_ref[...]
    wt = wt_ref[...]
    ch = SORT_CHUNK
    y = jnp.zeros(x_ref.shape, F32)
    for cidx in range(rl // ch):
        li = lax.broadcasted_iota(jnp.int32, (tn, ch), 1) + cidx * ch
        pw = jnp.zeros((tn, ch), F32)
        for k in range(TOP_K):
            pw = jnp.where(li == ls[:, k:k + 1], wt[:, k:k + 1], pw)
        hi = pw.astype(BF16)
        lo = (pw - hi.astype(F32)).astype(BF16)
        rows = ybuf[slot, cidx * ch:(cidx + 1) * ch, :]
        y = y + _dot(hi, rows) + _dot(lo, rows)
    o_ref[...] = x_ref[...] + g2_ref[0] * y


def _combine(seg_src, seg_dst, seg_len, ys, x2, ls, wts, mod3, l, B, S, tn, n_bits):
    N, D = x2.shape
    nb = S // tn
    base = l * B * 6
    rl = _local_rows(tn)
    grid_spec = pltpu.PrefetchScalarGridSpec(
        num_scalar_prefetch=3,
        grid=(N // tn,),
        in_specs=[pl.BlockSpec(memory_space=pl.ANY),
                  pl.BlockSpec((tn, D), lambda i, a, b, c: (i, 0)),
                  pl.BlockSpec((tn, LANES), lambda i, a, b, c: (i, 0)),
                  pl.BlockSpec((tn, LANES), lambda i, a, b, c: (i, 0)),
                  pl.BlockSpec((1, 1, D), lambda i, a, b, c: (base + (i // nb) * 6 + 5, 0, 0))],
        out_specs=pl.BlockSpec((tn, D), lambda i, a, b, c: (i, 0)),
        scratch_shapes=[pltpu.VMEM((2, rl, D), ys.dtype), pltpu.SemaphoreType.DMA((2,))],
    )
    return pl.pallas_call(
        functools.partial(_combine_kernel, n_bits),
        grid_spec=grid_spec,
        out_shape=jax.ShapeDtypeStruct((N, D), F32),
        compiler_params=_cparams(("arbitrary",)),
        name="moe_combine",
    )(seg_src, seg_dst, seg_len, ys, x2, ls, wts, mod3)


def _tile_rows(S, want):
    t = min(S, want)
    while S % t:
        t //= 2
    return t


def _prep_in_weights(w_in_l):
    HW = HGRN_HEADS * HGRN_DIM
    o = 0
    wh = w_in_l[:, o:o + 4 * HW]; o += 4 * HW
    wql = w_in_l[:, o:o + MLA_Q_RANK]; o += MLA_Q_RANK
    wkvl = w_in_l[:, o:o + MLA_KV_RANK]; o += MLA_KV_RANK
    wkr = w_in_l[:, o:o + MLA_ROPE]; o += MLA_ROPE
    wp = w_in_l[:, o:o + HW]; o += HW
    wg = w_in_l[:, o:]
    D = w_in_l.shape[0]
    wkr_p = jnp.zeros((D, LANES), w_in_l.dtype).at[:, MLA_NOPE:MLA_NOPE + MLA_ROPE].set(wkr)
    wm = jnp.concatenate([wql, wkvl, wkr_p, _rope_partner(wkr_p)], axis=1)
    return wg.astype(BF16), wh.astype(BF16), wm.astype(BF16), wp.astype(BF16)


def _rope_partner(w):
    half = MLA_ROPE // 2
    w3 = w.reshape(w.shape[:-1] + (-1, LANES))
    first = w3[..., MLA_NOPE:MLA_NOPE + half]
    second = w3[..., MLA_NOPE + half:MLA_NOPE + MLA_ROPE]
    out = jnp.concatenate([jnp.zeros_like(w3[..., :MLA_NOPE]), second, first,
                           jnp.zeros_like(w3[..., MLA_NOPE + MLA_ROPE:])], axis=-1)
    return out.reshape(w.shape)


def _pad_heads(w, n_heads, width):
    K = w.shape[0]
    w3 = w.reshape(K, n_heads, width)
    return jnp.pad(w3, ((0, 0), (0, 0), (0, LANES - width))).reshape(K, n_heads * LANES)


def kernel(x, c, positions, ada_w, ada_b, norm1_g, norm2_g, w_in, hgrn_lb, hgrn_onorm_g, mla_qlat_g, mla_kvlat_g, w_uq, w_ukv, q_norm_g, k_norm_g, w_pool, pool_scale, w_br_a, w_br_b, w_br_c, w_out, w_router, b_router, w_exp1, b_exp1, w_exp2, b_exp2):
    B, S, D = x.shape
    L = ada_w.shape[0]
    N = B * S
    H = MLA_HEADS
    tr = _tile_rows(S, 512)
    tc = _tile_rows(S, 256)
    ta = _tile_rows(S, 512)
    tm = 512

    mod = _ada(c, ada_w, ada_b)
    mod3 = mod.reshape(L * B * 6, 1, D)

    inv_freq = 1.0 / (ROPE_THETA ** (jnp.arange(0, MLA_ROPE, 2, dtype=F32) / MLA_ROPE))
    freq_row = jnp.zeros((1, LANES), F32).at[0, MLA_NOPE:MLA_NOPE + MLA_ROPE].set(
        jnp.concatenate([inv_freq, inv_freq]))
    posf = positions.astype(F32).reshape(N, 1)
    cos_t, sin_t = _rope_tables(posf, freq_row, tr)

    max_rows = N * TOP_K + (N // tr) * N_EXPERTS * (SEG_ALIGN - 1)
    n_tiles = -(-max_rows // tm) + N_EXPERTS
    n_bits = (tr // SEG_ALIGN).bit_length()
    x2 = x.reshape(N, D)
    for l in range(L):
        wg, wh, wm, wp = _prep_in_weights(w_in[l])
        pg, ph, pm, pp = _inproj(x2, norm1_g[l].reshape(1, D), mod3, l, wg, wh, wm, wp, B, S, tr)

        ya = _hgrn(ph.reshape(B, S, -1), hgrn_lb, hgrn_onorm_g[l].reshape(1, HGRN_DIM), l, B, S, tc)

        wq_f = _pad_heads(w_uq[l], H, MLA_QK)
        wq_p = wq_f.astype(BF16)
        wqr_p = _rope_partner(wq_f).astype(BF16)
        wkv3 = w_ukv[l].reshape(MLA_KV_RANK, H, MLA_NOPE + MLA_V)
        wkv_p = jnp.concatenate(
            [_pad_heads(wkv3[:, :, :MLA_NOPE].reshape(MLA_KV_RANK, H * MLA_NOPE), H, MLA_NOPE),
             wkv3[:, :, MLA_NOPE:].reshape(MLA_KV_RANK, H * MLA_V)], axis=1).astype(BF16)
        qn_p = jnp.pad(q_norm_g[l], (0, LANES - MLA_QK)).reshape(1, LANES)
        kn_p = jnp.pad(k_norm_g[l], (0, LANES - MLA_QK)).reshape(1, LANES)
        q, k, v = _mla_prep(pm, cos_t, sin_t, mla_qlat_g[l].reshape(1, -1), mla_kvlat_g[l].reshape(1, -1),
                            wq_p, wqr_p, wkv_p, qn_p, _rope_partner(qn_p), kn_p, _rope_partner(kn_p),
                            B, S, tr)
        yb = _attention(q, k, v.reshape(B, S, H * MLA_V), B, S, ta)

        yc = _pool(pp.reshape(B, S, -1), w_pool[l], pool_scale[l].reshape(1, -1))

        x2 = _merge(x2, pg, ya.reshape(N, -1), yb.reshape(N, -1), yc.reshape(N, -1),
                    w_br_a[l].astype(BF16), w_br_b[l].astype(BF16), w_br_c[l].astype(BF16),
                    w_out[l].astype(BF16), mod3, l, B, S, tr)

        wr_p = jnp.pad(w_router[l], ((0, 0), (0, LANES - N_EXPERTS)))
        br_p = jnp.concatenate([b_router[l], jnp.full((LANES - N_EXPERTS,), NEG_BIG, F32)]).reshape(1, LANES)
        xl, ls, wts, cnt = _router(x2, norm2_g[l].reshape(1, D), mod3, l, wr_p, br_p, B, S, tr)

        seg = ((cnt[:, 0, :N_EXPERTS].astype(jnp.int32) + SEG_ALIGN - 1) // SEG_ALIGN) * SEG_ALIGN
        seg_src = jnp.cumsum(seg, axis=1) - seg
        group = jnp.sum(seg, axis=0)
        tiles_per = (group + tm - 1) // tm
        tile_end = jnp.cumsum(tiles_per)
        offsets = (tile_end - tiles_per) * tm
        seg_dst = offsets[None, :] + jnp.cumsum(seg, axis=0) - seg
        n_used = tile_end[-1:]
        tile_ids = jnp.minimum(jnp.arange(n_tiles, dtype=jnp.int32), n_used[0] - 1)
        tile_expert = jnp.minimum(
            jnp.sum((tile_end[None, :] <= tile_ids[:, None]).astype(jnp.int32), axis=1), N_EXPERTS - 1)
        all_tiles = jnp.arange(n_tiles, dtype=jnp.int32)
        is_tail = jnp.any((tile_end[None, :] - 1 == all_tiles[:, None]) & (tiles_per[None, :] > 0), axis=1)
        zero_flags = (is_tail | (all_tiles >= n_used[0])).astype(jnp.int32)
        seg_src = seg_src.reshape(-1).astype(jnp.int32)
        seg_dst = seg_dst.reshape(-1).astype(jnp.int32)
        seg_len = (seg // SEG_ALIGN).reshape(-1).astype(jnp.int32)

        xs = _place(seg_src, seg_dst, seg_len, zero_flags, xl, n_tiles * tm, _local_rows(tr), tm, n_bits)
        ys = _experts(tile_expert.astype(jnp.int32), n_used.astype(jnp.int32), xs,
                      w_exp1, b_exp1, w_exp2, b_exp2, l, tm)
        x2 = _combine(seg_src, seg_dst, seg_len, ys, x2, ls, wts, mod3, l, B, S, tr, n_bits)
    return x2.reshape(B, S, D)
```

```python
import functools
import math

import jax
import jax.numpy as jnp
from jax import lax
from jax.experimental import pallas as pl
from jax.experimental.pallas import tpu as pltpu

F32 = jnp.float32
BF16 = jnp.bfloat16

LANES = 128
NORM_EPS = 1e-6
HGRN_HEADS = 4
HGRN_DIM = 128
HGRN_CHUNK = 32
HGRN_ROWS = 8
MLA_HEADS = 8
MLA_NOPE = 64
MLA_ROPE = 32
MLA_QK = MLA_NOPE + MLA_ROPE
MLA_V = 64
MLA_Q_RANK = 384
MLA_KV_RANK = 256
ROPE_THETA = 10000.0
POOL_WINDOWS = (2, 4, 8, 16)
N_EXPERTS = 32
TOP_K = 4
SWIGLU_LIMIT = 7.0
SWIGLU_ALPHA = 1.702
NEG_BIG = -1e30
LOG2E = 1.4426950408889634
FACTOR_LIMIT = 100.0
SEG_ALIGN = 16
SORT_CHUNK = 256

VMEM_LIMIT = 56 * 1024 * 1024


def _cparams(sem):
    return pltpu.CompilerParams(dimension_semantics=sem, vmem_limit_bytes=VMEM_LIMIT)


def _dot(a, b):
    return jnp.dot(a, b, preferred_element_type=F32)


def _dot_nt(a, b):
    return lax.dot_general(a, b, (((1,), (1,)), ((), ())), preferred_element_type=F32)


def _dot_tn(a, b):
    return lax.dot_general(a, b, (((0,), (0,)), ((), ())), preferred_element_type=F32)


def _split3(x):
    hi = x.astype(BF16)
    r = x - hi.astype(F32)
    mid = r.astype(BF16)
    lo = (r - mid.astype(F32)).astype(BF16)
    return hi, mid, lo


def _sigmoid(x):
    return 1.0 / (1.0 + jnp.exp(-x))


def _ada_kernel(c_ref, w_ref, b_ref, o_ref):
    c = c_ref[...]
    cond = (c * _sigmoid(c)).astype(BF16)
    o_ref[0] = _dot(cond, w_ref[0].astype(BF16)) + b_ref[0]


def _ada(c, ada_w, ada_b):
    L, D, W = ada_w.shape
    B = c.shape[0]
    tn = D
    return pl.pallas_call(
        _ada_kernel,
        grid=(L, W // tn),
        in_specs=[
            pl.BlockSpec((B, D), lambda l, j: (0, 0)),
            pl.BlockSpec((1, D, tn), lambda l, j: (l, 0, j)),
            pl.BlockSpec((1, 1, tn), lambda l, j: (l, 0, j)),
        ],
        out_specs=pl.BlockSpec((1, B, tn), lambda l, j: (l, 0, j)),
        out_shape=jax.ShapeDtypeStruct((L, B, W), F32),
        compiler_params=_cparams(("parallel", "parallel")),
        name="ada_mod",
    )(c, ada_w, ada_b.reshape(L, 1, W))


def _inproj_kernel(x_ref, g_ref, sh_ref, sc_ref, wg_ref, wh_ref, wm_ref, wp_ref,
                   og_ref, oh_ref, om_ref, op_ref):
    x = x_ref[...]
    ms = jnp.mean(x * x, axis=-1, keepdims=True)
    h = x * lax.rsqrt(ms + NORM_EPS) * g_ref[...]
    h = h * (1.0 + sc_ref[0]) + sh_ref[0]
    hb = h.astype(BF16)
    cw = 512
    for w_ref, o_ref in ((wg_ref, og_ref), (wh_ref, oh_ref), (wm_ref, om_ref), (wp_ref, op_ref)):
        width = w_ref.shape[1]
        step = cw if width % cw == 0 else width
        for j in range(0, width, step):
            o_ref[:, j:j + step] = _dot(hb, w_ref[:, j:j + step]).astype(o_ref.dtype)


def _inproj(x2, g, mod3, l, wg, wh, wm, wp, B, S, tr):
    N, D = x2.shape
    nb = S // tr
    base = l * B * 6

    def modspec(j):
        return pl.BlockSpec((1, 1, D), lambda i: (base + (i // nb) * 6 + j, 0, 0))

    def wspec(w):
        return pl.BlockSpec(w.shape, lambda i: (0, 0), pipeline_mode=pl.Buffered(1))

    outs = (
        jax.ShapeDtypeStruct((N, wg.shape[1]), BF16),
        jax.ShapeDtypeStruct((N, wh.shape[1]), F32),
        jax.ShapeDtypeStruct((N, wm.shape[1]), F32),
        jax.ShapeDtypeStruct((N, wp.shape[1]), F32),
    )
    return pl.pallas_call(
        _inproj_kernel,
        grid=(N // tr,),
        in_specs=[
            pl.BlockSpec((tr, D), lambda i: (i, 0)),
            pl.BlockSpec((1, D), lambda i: (0, 0)),
            modspec(0), modspec(1),
            wspec(wg), wspec(wh), wspec(wm), wspec(wp),
        ],
        out_specs=tuple(pl.BlockSpec((tr, o.shape[1]), lambda i: (i, 0)) for o in outs),
        out_shape=outs,
        compiler_params=_cparams(("parallel",)),
        name="in_proj",
    )(x2, g, mod3, mod3, wg, wh, wm, wp)


def _hgrn_kernel(layer, n_chunks, q_ref, f_ref, i_ref, g_ref, lb_ref, on_ref, o_ref,
                 st_ref, stb_ref, oi_ref):
    C, DH, H = HGRN_CHUNK, HGRN_DIM, HGRN_HEADS

    @pl.when(pl.program_id(1) == 0)
    def _():
        st_ref[...] = jnp.zeros_like(st_ref)
        stb_ref[...] = jnp.zeros_like(stb_ref)

    lbr = lb_ref[...]
    e = jnp.exp(lbr - jnp.max(lbr, axis=0, keepdims=True))
    p = e / jnp.sum(e, axis=0, keepdims=True)
    lb = jnp.zeros((1, lbr.shape[1]), F32)
    for j in range(1, layer + 1):
        lb = lb + p[j:j + 1]
    log_lb = jnp.log(lb)
    log_1mlb = jnp.log1p(-lb)
    one_m_lb = 1.0 - lb
    onorm = jnp.tile(on_ref[...], (1, H))

    row_c = lax.broadcasted_iota(jnp.int32, (C, C), 0)
    col_c = lax.broadcasted_iota(jnp.int32, (C, C), 1)
    tri = (col_c <= row_c).astype(BF16)
    rs = lax.broadcasted_iota(jnp.int32, (H * C, H * C), 0)
    cs = lax.broadcasted_iota(jnp.int32, (H * C, H * C), 1)
    same_head_causal = jnp.logical_and(cs <= rs, cs >= rs - rs % C)
    row_1 = lax.broadcasted_iota(jnp.int32, (C, 1), 0)
    heads = [slice(h * DH, (h + 1) * DH) for h in range(H)]

    def stack(x):
        return jnp.concatenate([x[:, hs] for hs in heads], axis=0)

    def unstack(x):
        return jnp.concatenate([x[h * C:(h + 1) * C] for h in range(H)], axis=1)

    def prepare(r, sl):
        q = q_ref[r, sl, :]
        z = f_ref[r, sl, :]
        v = i_ref[r, sl, :]
        ez = jnp.exp(-jnp.abs(z))
        log_sig = jnp.minimum(z, 0.0) - jnp.log1p(ez)
        if layer == 0:
            lf = log_sig
            kk = jnp.where(z >= 0.0, ez, 1.0) / (1.0 + ez)
        else:
            a = log_1mlb + log_sig
            m = jnp.maximum(a, log_lb)
            lf = m + jnp.log(jnp.exp(a - m) + jnp.exp(log_lb - m))
            kk = one_m_lb * jnp.where(z >= 0.0, ez, 1.0) / (1.0 + ez)
        l1, l2, l3 = _split3(lf)
        b = (_dot(tri, l1) + _dot(tri, l2) + _dot(tri, l3)) * LOG2E
        qs = (q * jnp.exp2(b)).astype(BF16)
        o_inter = _dot_nt(qs, stb_ref[r])
        return q, v, kk, b, qs, v.astype(BF16), o_inter

    def intra_factored(r, prepared):
        _, _, kk, b, qs, vb, _ = prepared
        ks = (kk * jnp.exp2(-b)).astype(BF16)
        att = _dot_nt(stack(qs), stack(ks))
        att = jnp.where(same_head_causal, att, 0.0)
        oi_ref[r] = unstack(_dot(att.astype(BF16), stack(vb)))

    def intra_termwise(r, prepared):
        q, v, kk, b, _, _, _ = prepared
        for hs in heads:
            bh, qh, kh, vh = b[:, hs], q[:, hs], kk[:, hs], v[:, hs]
            acc = jnp.zeros((C, DH), F32)
            for s in range(C):
                es = jnp.exp2(jnp.minimum(bh - bh[s:s + 1], 0.0))
                col = jnp.sum(qh * es * kh[s:s + 1], axis=-1, keepdims=True)
                acc = acc + jnp.where(row_1 >= s, col, 0.0) * vh[s:s + 1]
            oi_ref[r, :, hs] = acc

    def finish(r, sl, prepared):
        _, _, kk, b, _, vb, o_inter = prepared
        bl = b[C - 1:C]
        o = oi_ref[r] + o_inter
        kd = (kk * jnp.exp2(bl - b)).astype(BF16)
        update = _dot_tn(vb, kd)
        decay = jnp.exp2(bl)
        normed = []
        for hs in heads:
            new = st_ref[r, hs, hs] * decay[:, hs] + update[hs, hs]
            st_ref[r, hs, hs] = new
            stb_ref[r, hs, hs] = new.astype(BF16)
            oh = o[:, hs]
            normed.append(oh * lax.rsqrt(jnp.mean(oh * oh, axis=-1, keepdims=True) + NORM_EPS))
        g = g_ref[r, sl, :]
        y = jnp.concatenate(normed, axis=1) * onorm * (g * _sigmoid(g))
        o_ref[r, sl, :] = y.astype(o_ref.dtype)

    def chunk(ci, carry):
        sl = pl.ds(pl.multiple_of(ci * C, C), C)
        rows = range(q_ref.shape[0])
        prepared = [prepare(r, sl) for r in rows]
        span = -prepared[0][3][C - 1:C]
        for p in prepared[1:]:
            span = jnp.maximum(span, -p[3][C - 1:C])
        factor_ok = jnp.max(span) < FACTOR_LIMIT

        @pl.when(factor_ok)
        def _():
            for r in rows:
                intra_factored(r, prepared[r])

        @pl.when(jnp.logical_not(factor_ok))
        def _():
            for r in rows:
                intra_termwise(r, prepared[r])

        for r in rows:
            finish(r, sl, prepared[r])
        return carry

    lax.fori_loop(0, n_chunks, chunk, 0)


def _hgrn(ph, hgrn_lb, onorm_g, layer, B, S, tc):
    L = hgrn_lb.shape[0]
    W = HGRN_HEADS * HGRN_DIM

    nb = math.gcd(B, HGRN_ROWS)

    def spec(off):
        return pl.BlockSpec((nb, tc, W), lambda b, t: (b, t, off))

    return pl.pallas_call(
        functools.partial(_hgrn_kernel, layer, tc // HGRN_CHUNK),
        grid=(B // nb, S // tc),
        in_specs=[spec(0), spec(1), spec(2), spec(3),
                  pl.BlockSpec((L, W), lambda b, t: (0, 0)),
                  pl.BlockSpec((1, HGRN_DIM), lambda b, t: (0, 0))],
        out_specs=pl.BlockSpec((nb, tc, W), lambda b, t: (b, t, 0)),
        out_shape=jax.ShapeDtypeStruct((B, S, W), BF16),
        scratch_shapes=[pltpu.VMEM((nb, W, W), F32), pltpu.VMEM((nb, W, W), BF16),
                        pltpu.VMEM((nb, HGRN_CHUNK, W), F32)],
        compiler_params=_cparams(("parallel", "arbitrary")),
        name="hgrn2",
    )(ph, ph, ph, ph, hgrn_lb, onorm_g)


def _rope_table_kernel(pos_ref, fr_ref, cos_ref, sin_ref):
    ang = pos_ref[...] * fr_ref[...]
    lane = lax.broadcasted_iota(jnp.int32, ang.shape, 1)
    sn = jnp.sin(ang)
    cos_ref[...] = jnp.cos(ang)
    sin_ref[...] = jnp.where(lane < MLA_NOPE + MLA_ROPE // 2, -sn, sn)


def _rope_tables(posf, freq_row, tr):
    N = posf.shape[0]
    out = jax.ShapeDtypeStruct((N, LANES), F32)
    return pl.pallas_call(
        _rope_table_kernel,
        grid=(N // tr,),
        in_specs=[pl.BlockSpec((tr, 1), lambda i: (i, 0)), pl.BlockSpec((1, LANES), lambda i: (0, 0))],
        out_specs=(pl.BlockSpec((tr, LANES), lambda i: (i, 0)), pl.BlockSpec((tr, LANES), lambda i: (i, 0))),
        out_shape=(out, out),
        compiler_params=_cparams(("parallel",)),
        name="rope_tables",
    )(posf, freq_row)


def _mla_prep_kernel(pm_ref, cos_ref, sin_ref, qg_ref, kvg_ref, wq_ref, wqr_ref, wkv_ref,
                     qn_ref, qnr_ref, kn_ref, knr_ref, q_ref, k_ref, v_ref):
    pm = pm_ref[...]
    o = 0
    ql = pm[:, o:o + MLA_Q_RANK]; o += MLA_Q_RANK
    kvl = pm[:, o:o + MLA_KV_RANK]; o += MLA_KV_RANK
    kr = pm[:, o:o + LANES]; o += LANES
    krr = pm[:, o:o + LANES]

    def rms(x, g):
        return x * lax.rsqrt(jnp.mean(x * x, axis=-1, keepdims=True) + NORM_EPS) * g

    qlb = rms(ql, qg_ref[...]).astype(BF16)
    qa = _dot(qlb, wq_ref[...])
    qr = _dot(qlb, wqr_ref[...])
    kva = _dot(rms(kvl, kvg_ref[...]).astype(BF16), wkv_ref[...])
    v_ref[...] = kva[:, MLA_HEADS * LANES:].astype(v_ref.dtype)

    cs = cos_ref[...]
    sn = sin_ref[...]
    scale = LOG2E / math.sqrt(MLA_QK)
    q_cos = qn_ref[...] * cs * scale
    q_sin = qnr_ref[...] * sn * scale
    k_cos = kn_ref[...] * cs
    k_sin = knr_ref[...] * sn
    k_rot = krr * k_sin

    def inv_rms(xh):
        return lax.rsqrt(jnp.sum(xh * xh, axis=-1, keepdims=True) * (1.0 / MLA_QK) + NORM_EPS)

    for h in range(MLA_HEADS):
        hs = slice(h * LANES, (h + 1) * LANES)
        qh = qa[:, hs]
        q_ref[0, h] = ((qh * q_cos + qr[:, hs] * q_sin) * inv_rms(qh)).astype(q_ref.dtype)
        kh = kva[:, hs] + kr
        k_ref[0, h] = ((kh * k_cos + k_rot) * inv_rms(kh)).astype(k_ref.dtype)


def _mla_prep(pm, cos_t, sin_t, qlat_g, kvlat_g, wq, wqr, wkv, qn, qnr, kn, knr, B, S, tr):
    N = pm.shape[0]
    nb = S // tr
    H = MLA_HEADS
    full = lambda a: pl.BlockSpec(a.shape, lambda i: (0,) * a.ndim)
    return pl.pallas_call(
        _mla_prep_kernel,
        grid=(N // tr,),
        in_specs=[pl.BlockSpec((tr, pm.shape[1]), lambda i: (i, 0)),
                  pl.BlockSpec((tr, LANES), lambda i: (i, 0)),
                  pl.BlockSpec((tr, LANES), lambda i: (i, 0)),
                  full(qlat_g), full(kvlat_g), full(wq), full(wqr), full(wkv),
                  full(qn), full(qnr), full(kn), full(knr)],
        out_specs=(pl.BlockSpec((1, H, tr, LANES), lambda i: (i // nb, 0, i % nb, 0)),
                   pl.BlockSpec((1, H, tr, LANES), lambda i: (i // nb, 0, i % nb, 0)),
                   pl.BlockSpec((tr, H * MLA_V), lambda i: (i, 0))),
        out_shape=(jax.ShapeDtypeStruct((B, H, S, LANES), BF16),
                   jax.ShapeDtypeStruct((B, H, S, LANES), BF16),
                   jax.ShapeDtypeStruct((N, H * MLA_V), BF16)),
        compiler_params=_cparams(("parallel",)),
        name="mla_prep",
    )(pm, cos_t, sin_t, qlat_g, kvlat_g, wq, wqr, wkv, qn, qnr, kn, knr)


def _attn_kernel(q_ref, k_ref, v_ref, o_ref, m_ref, l_ref, acc_ref):
    qi = pl.program_id(2)
    T = q_ref.shape[2]

    m_ref[...] = jnp.full_like(m_ref, NEG_BIG)
    l_ref[...] = jnp.zeros_like(l_ref)
    acc_ref[...] = jnp.zeros_like(acc_ref)

    def step(ki, masked):
        rows = pl.ds(pl.multiple_of(ki * T, T), T)
        vblk = v_ref[0, rows, :]
        lane = lax.broadcasted_iota(jnp.int32, vblk.shape, 1)
        if masked:
            row = lax.broadcasted_iota(jnp.int32, (T, T), 0)
            col = lax.broadcasted_iota(jnp.int32, (T, T), 1)
            keep = col <= row
        for h in range(2):
            s = _dot_nt(q_ref[0, h], k_ref[0, h, rows, :])
            if masked:
                s = jnp.where(keep, s, NEG_BIG)
            m_old = m_ref[h]
            m_new = jnp.maximum(m_old, jnp.max(s, axis=-1, keepdims=True))
            alpha = jnp.exp2(m_old - m_new)
            p = jnp.exp2(s - jnp.tile(m_new, (1, T // LANES)))
            l_ref[h] = alpha * l_ref[h] + jnp.sum(p, axis=-1, keepdims=True)
            vh = jnp.where((lane < MLA_V) if h == 0 else (lane >= MLA_V), vblk, jnp.zeros_like(vblk))
            acc_ref[h] = alpha * acc_ref[h] + _dot(p.astype(BF16), vh)
            m_ref[h] = m_new

    def below_diagonal(ki, c):
        step(ki, False)
        return c

    lax.fori_loop(0, qi, below_diagonal, 0)
    step(qi, True)
    o_ref[0] = (acc_ref[0] / l_ref[0] + acc_ref[1] / l_ref[1]).astype(o_ref.dtype)


def _attention(q, k, v3, B, S, T):
    H = MLA_HEADS
    n = S // T
    return pl.pallas_call(
        _attn_kernel,
        grid=(B, H // 2, n),
        in_specs=[pl.BlockSpec((1, 2, T, LANES), lambda b, hp, qi: (b, hp, qi, 0)),
                  pl.BlockSpec((1, 2, S, LANES), lambda b, hp, qi: (b, hp, 0, 0)),
                  pl.BlockSpec((1, S, LANES), lambda b, hp, qi: (b, 0, hp))],
        out_specs=pl.BlockSpec((1, T, LANES), lambda b, hp, qi: (b, qi, hp)),
        out_shape=jax.ShapeDtypeStruct((B, S, H * MLA_V), BF16),
        scratch_shapes=[pltpu.VMEM((2, T, LANES), F32), pltpu.VMEM((2, T, LANES), F32),
                        pltpu.VMEM((2, T, LANES), F32)],
        compiler_params=_cparams(("parallel", "parallel", "arbitrary")),
        name="mla_attention",
    )(q, k, v3)


def _pool_kernel(u_ref, w_ref, sc_ref, o_ref):
    S = u_ref.shape[1]
    row = lax.broadcasted_iota(jnp.int32, (S, LANES), 0)
    t1 = (row + 1).astype(F32)
    for gi, win in enumerate(POOL_WINDOWS):
        u = u_ref[0, :, gi * LANES:(gi + 1) * LANES]
        s = u
        k = 1
        while k < win:
            s = s + jnp.where(row >= k, pltpu.roll(s, k, 0), 0.0)
            k *= 2
        pooled = s / jnp.minimum(t1, float(win))
        mixed = _dot((pooled - u).astype(BF16), w_ref[gi].astype(BF16))
        o_ref[0, :, gi * LANES:(gi + 1) * LANES] = (
            mixed * sc_ref[:, gi * LANES:(gi + 1) * LANES]).astype(o_ref.dtype)


def _pool(pp3, w_pool, scale):
    B, S, W = pp3.shape
    return pl.pallas_call(
        _pool_kernel,
        grid=(B,),
        in_specs=[pl.BlockSpec((1, S, W), lambda b: (b, 0, 0)),
                  pl.BlockSpec(w_pool.shape, lambda b: (0, 0, 0)),
                  pl.BlockSpec((1, W), lambda b: (0, 0))],
        out_specs=pl.BlockSpec((1, S, W), lambda b: (b, 0, 0)),
        out_shape=jax.ShapeDtypeStruct((B, S, W), BF16),
        compiler_params=_cparams(("parallel",)),
        name="pool_mixer",
    )(pp3, w_pool, scale)


def _merge_router_kernel(x_ref, gt_ref, ya_ref, yb_ref, yc_ref, wa_ref, wb_ref, wc_ref, wo_ref, g1_ref,
                         g_ref, sh_ref, sc_ref, wr_ref, br_ref, xo_ref, xl_ref, ls_ref, wt_ref, cnt_ref):
    D = x_ref.shape[1]
    merged = None
    for j, (y_ref, w_ref) in enumerate(((ya_ref, wa_ref), (yb_ref, wb_ref), (yc_ref, wc_ref))):
        y = _dot(y_ref[...], w_ref[...])
        gate = _sigmoid(gt_ref[:, j * D:(j + 1) * D].astype(F32))
        merged = gate * y if merged is None else merged + gate * y
    x = x_ref[...] + g1_ref[0] * _dot(merged.astype(BF16), wo_ref[...])
    xo_ref[...] = x
    _route_and_sort(x, g_ref, sh_ref, sc_ref, wr_ref, br_ref, xl_ref, ls_ref, wt_ref, cnt_ref)


def _route_and_sort(x, g_ref, sh_ref, sc_ref, wr_ref, br_ref, xl_ref, ls_ref, wt_ref, cnt_ref):
    tr = x.shape[0]
    rl = xl_ref.shape[0]

    ms = jnp.mean(x * x, axis=-1, keepdims=True)
    h = x * lax.rsqrt(ms + NORM_EPS) * g_ref[...]
    h = h * (1.0 + sc_ref[0]) + sh_ref[0]
    h1, h2, _ = _split3(h)
    w1, w2, _ = _split3(wr_ref[...])
    logits = _dot(h1, w1) + (_dot(h1, w2) + _dot(h2, w1)) + br_ref[...]
    lane = lax.broadcasted_iota(jnp.int32, logits.shape, 1)
    work = logits
    vals, ids = [], []
    for _k in range(TOP_K):
        m = jnp.max(work, axis=-1, keepdims=True)
        sel = jnp.min(jnp.where(work == m, lane, LANES), axis=-1, keepdims=True)
        vals.append(m)
        ids.append(sel)
        work = jnp.where(lane == sel, -jnp.inf, work)
    es = [jnp.exp(v - vals[0]) for v in vals]
    den = es[0] + es[1] + es[2] + es[3]
    onehot = jnp.zeros(logits.shape, F32)
    wt_out = jnp.zeros(logits.shape, F32)
    for k in range(TOP_K):
        onehot = onehot + (lane == ids[k]).astype(F32)
        wt_out = jnp.where(lane == k, es[k] / den, wt_out)
    wt_ref[...] = wt_out

    r = lax.broadcasted_iota(jnp.int32, (tr, tr), 0)
    c = lax.broadcasted_iota(jnp.int32, (tr, tr), 1)
    strict = (c < r).astype(BF16)
    before = _dot(strict, onehot.astype(BF16))
    cnt = jnp.sum(onehot, axis=0, keepdims=True)
    cnt_ref[0] = jnp.broadcast_to(cnt, cnt_ref.shape[1:])
    units = jnp.floor((cnt + (SEG_ALIGN - 1.0)) * (1.0 / SEG_ALIGN))
    er = lax.broadcasted_iota(jnp.int32, (LANES, LANES), 0)
    ec = lax.broadcasted_iota(jnp.int32, (LANES, LANES), 1)
    upper = (er < ec).astype(BF16)
    seg_start = _dot(jnp.broadcast_to(units, (8, LANES)).astype(BF16), upper)[0:1] * float(SEG_ALIGN)
    pos = before + seg_start
    ls_out = jnp.full(logits.shape, -1.0, F32)
    for k in range(TOP_K):
        lk = jnp.sum(jnp.where(lane == ids[k], pos, 0.0), axis=-1, keepdims=True)
        ls_out = jnp.where(lane == k, lk, ls_out)
    ls_ref[...] = ls_out.astype(jnp.int32)

    ls_t = jnp.transpose(ls_out)
    targets = [ls_t[k:k + 1, :].astype(jnp.int32) for k in range(TOP_K)]
    hb = h.astype(BF16)
    ch = _sort_chunk(rl)
    for cidx in range(rl // ch):
        ri = lax.broadcasted_iota(jnp.int32, (ch, tr), 0) + cidx * ch
        hit = ri == targets[0]
        for k in range(1, TOP_K):
            hit = jnp.logical_or(hit, ri == targets[k])
        perm = jnp.where(hit, 1.0, 0.0).astype(BF16)
        xl_ref[cidx * ch:(cidx + 1) * ch, :] = _dot(perm, hb).astype(xl_ref.dtype)


def _local_rows(tb):
    return TOP_K * tb + N_EXPERTS * SEG_ALIGN


def _sort_chunk(rl):
    return SORT_CHUNK if rl % SORT_CHUNK == 0 else SORT_CHUNK // 2


def _merge_router(x2, gates, ya, yb, yc, wa, wb, wc, wo, g, mod3, l, wr_p, br_p, B, S, tr):
    N, D = x2.shape
    nb = S // tr
    base = l * B * 6
    rl = _local_rows(tr)
    full = lambda a: pl.BlockSpec(a.shape, lambda i: (0,) * a.ndim)
    rows = lambda w: pl.BlockSpec((tr, w), lambda i: (i, 0))

    def modspec(j):
        return pl.BlockSpec((1, 1, D), lambda i: (base + (i // nb) * 6 + j, 0, 0))

    return pl.pallas_call(
        _merge_router_kernel,
        grid=(N // tr,),
        in_specs=[rows(D), rows(gates.shape[1]), rows(ya.shape[1]), rows(yb.shape[1]), rows(yc.shape[1]),
                  full(wa), full(wb), full(wc), full(wo), modspec(2),
                  full(g), modspec(3), modspec(4), full(wr_p), full(br_p)],
        out_specs=(rows(D), pl.BlockSpec((rl, D), lambda i: (i, 0)), rows(LANES), rows(LANES),
                   pl.BlockSpec((1, 8, LANES), lambda i: (i, 0, 0))),
        out_shape=(jax.ShapeDtypeStruct((N, D), F32),
                   jax.ShapeDtypeStruct((N // tr * rl, D), BF16),
                   jax.ShapeDtypeStruct((N, LANES), jnp.int32),
                   jax.ShapeDtypeStruct((N, LANES), F32),
                   jax.ShapeDtypeStruct((N // tr, 8, LANES), F32)),
        compiler_params=_cparams(("parallel",)),
        name="merge_router",
    )(x2, gates, ya, yb, yc, wa, wb, wc, wo, mod3, g, mod3, mod3, wr_p, br_p)


def _segment_copies(j, e, src_ref, dst_ref, len_ref, local_ref, global_ref, sem, to_global, n_bits):
    base = j * N_EXPERTS + e
    units = len_ref[base]
    s0 = src_ref[base]
    d0 = dst_ref[base]
    copies = []
    for bit in range(n_bits):
        rows = SEG_ALIGN << bit
        done = ((units >> (bit + 1)) << (bit + 1)) * SEG_ALIGN
        loc = local_ref.at[pl.ds(pl.multiple_of(s0 + done, SEG_ALIGN), rows)]
        glo = global_ref.at[pl.ds(pl.multiple_of(d0 + done, SEG_ALIGN), rows)]
        cp = pltpu.make_async_copy(loc, glo, sem) if to_global else pltpu.make_async_copy(glo, loc, sem)
        copies.append((((units >> bit) & 1) == 1, cp))
    return copies


def _each_segment_copy(j, src_ref, dst_ref, len_ref, local_ref, global_ref, sem, to_global, n_bits, action):
    def per_expert(e, c):
        for pred, cp in _segment_copies(j, e, src_ref, dst_ref, len_ref, local_ref, global_ref,
                                        sem, to_global, n_bits):
            @pl.when(pred)
            def _():
                action(cp)
        return c
    lax.fori_loop(0, N_EXPERTS, per_expert, 0)


def _move_segments(j, src_ref, dst_ref, len_ref, local_ref, global_ref, sem, to_global, n_bits):
    args = (j, src_ref, dst_ref, len_ref, local_ref, global_ref, sem, to_global, n_bits)
    _each_segment_copy(*args, lambda cp: cp.start())
    _each_segment_copy(*args, lambda cp: cp.wait())


def _place_kernel(n_bits, src_ref, dst_ref, len_ref, zflag_ref, xl_ref, xs_hbm, zbuf, sem):
    j = pl.program_id(0)
    tm = zbuf.shape[0]
    n_tiles = xs_hbm.shape[0] // tm

    @pl.when(j == 0)
    def _():
        zbuf[...] = jnp.zeros_like(zbuf)

        def zero_copy(t):
            return pltpu.make_async_copy(zbuf, xs_hbm.at[pl.ds(pl.multiple_of(t * tm, tm), tm)], sem)

        def zstart(t, c):
            @pl.when(zflag_ref[t] > 0)
            def _():
                zero_copy(t).start()
            return c

        def zwait(t, c):
            @pl.when(zflag_ref[t] > 0)
            def _():
                zero_copy(t).wait()
            return c

        lax.fori_loop(0, n_tiles, zstart, 0)
        lax.fori_loop(0, n_tiles, zwait, 0)

    _move_segments(j, src_ref, dst_ref, len_ref, xl_ref, xs_hbm, sem, True, n_bits)


def _place(seg_src, seg_dst, seg_len, zero_flags, xl, n_rows, rl, tm, n_bits):
    D = xl.shape[1]
    grid_spec = pltpu.PrefetchScalarGridSpec(
        num_scalar_prefetch=4,
        grid=(xl.shape[0] // rl,),
        in_specs=[pl.BlockSpec((rl, D), lambda j, a, b, c, d: (j, 0))],
        out_specs=pl.BlockSpec(memory_space=pl.ANY),
        scratch_shapes=[pltpu.VMEM((tm, D), xl.dtype), pltpu.SemaphoreType.DMA(())],
    )
    return pl.pallas_call(
        functools.partial(_place_kernel, n_bits),
        grid_spec=grid_spec,
        out_shape=jax.ShapeDtypeStruct((n_rows, D), xl.dtype),
        compiler_params=_cparams(("arbitrary",)),
        name="moe_place",
    )(seg_src, seg_dst, seg_len, zero_flags, xl)


def _expert_kernel(te_ref, nu_ref, x_ref, w1_ref, b1_ref, w2_ref, b2_ref, o_ref, w1b, w2b):
    t = pl.program_id(0)
    F = w2_ref.shape[1]

    @pl.when(t < nu_ref[0])
    def _():
        @pl.when(jnp.logical_or(t == 0, te_ref[t] != te_ref[jnp.maximum(t - 1, 0)]))
        def _():
            w1b[...] = w1_ref[0].astype(BF16)
            w2b[...] = w2_ref[0].astype(BF16)

        a = _dot(x_ref[...], w1b[...]) + b1_ref[0]
        glu = jnp.minimum(a[:, :F], SWIGLU_LIMIT)
        lin = jnp.clip(a[:, F:], -SWIGLU_LIMIT, SWIGLU_LIMIT)
        act = glu * _sigmoid(SWIGLU_ALPHA * glu) * (lin + 1.0)
        o_ref[...] = (_dot(act.astype(BF16), w2b[...]) + b2_ref[0]).astype(o_ref.dtype)

    @pl.when(t >= nu_ref[0])
    def _():
        o_ref[...] = jnp.zeros_like(o_ref)


def _experts(tile_expert, n_used, xs, w1_all, b1_all, w2_all, b2_all, l, tm):
    n_tiles = tile_expert.shape[0]
    D = xs.shape[1]
    L, E, _, F2 = w1_all.shape
    F = w2_all.shape[2]
    w1 = w1_all.reshape(L * E, D, F2)
    w2 = w2_all.reshape(L * E, F, D)
    b1 = b1_all.reshape(L * E, F2)
    b2 = b2_all.reshape(L * E, D)
    E0 = l * E
    E = L * E
    grid_spec = pltpu.PrefetchScalarGridSpec(
        num_scalar_prefetch=2,
        grid=(n_tiles,),
        in_specs=[pl.BlockSpec((tm, D), lambda t, te, nu: (jnp.minimum(t, nu[0] - 1), 0)),
                  pl.BlockSpec((1, D, F2), lambda t, te, nu: (E0 + te[t], 0, 0)),
                  pl.BlockSpec((1, 1, F2), lambda t, te, nu: (E0 + te[t], 0, 0)),
                  pl.BlockSpec((1, F, D), lambda t, te, nu: (E0 + te[t], 0, 0)),
                  pl.BlockSpec((1, 1, D), lambda t, te, nu: (E0 + te[t], 0, 0))],
        out_specs=pl.BlockSpec((tm, D), lambda t, te, nu: (t, 0)),
        scratch_shapes=[pltpu.VMEM((D, F2), BF16), pltpu.VMEM((F, D), BF16)],
    )
    return pl.pallas_call(
        _expert_kernel,
        grid_spec=grid_spec,
        out_shape=jax.ShapeDtypeStruct((n_tiles * tm, D), BF16),
        compiler_params=_cparams(("arbitrary",)),
        name="expert_mlp",
    )(tile_expert, n_used, xs, w1, b1.reshape(E, 1, F2), w2, b2.reshape(E, 1, D))


def _combine_kernel(n_bits, src_ref, dst_ref, len_ref, ys_hbm, x_ref, ls_ref, wt_ref, g2_ref, o_ref,
                    ybuf, sem):
    j = pl.program_id(0)
    tn = x_ref.shape[0]
    rl = ybuf.shape[1]
    slot = j % 2

    def fetch(block, buf_slot, action):
        _each_segment_copy(block, src_ref, dst_ref, len_ref, ybuf.at[buf_slot], ys_hbm,
                           sem.at[buf_slot], False, n_bits, action)

    def start_fetch(block, buf_slot):
        ybuf[buf_slot] = jnp.zeros((rl, ybuf.shape[2]), ybuf.dtype)
        fetch(block, buf_slot, lambda cp: cp.start())

    @pl.when(j == 0)
    def _():
        start_fetch(j, slot)

    @pl.when(j + 1 < pl.num_programs(0))
    def _():
        start_fetch(j + 1, 1 - slot)

    fetch(j, slot, lambda cp: cp.wait())
    ls = ls_ref[...]
    wt = wt_ref[...]
    ch = _sort_chunk(rl)
    y = jnp.zeros(x_ref.shape, F32)
    for cidx in range(rl // ch):
        li = lax.broadcasted_iota(jnp.int32, (tn, ch), 1) + cidx * ch
        pw = jnp.zeros((tn, ch), F32)
        for k in range(TOP_K):
            pw = jnp.where(li == ls[:, k:k + 1], wt[:, k:k + 1], pw)
        y = y + _dot(pw.astype(BF16), ybuf[slot, cidx * ch:(cidx + 1) * ch, :])
    o_ref[...] = x_ref[...] + g2_ref[0] * y


def _combine(seg_src, seg_dst, seg_len, ys, x2, ls, wts, mod3, l, B, S, tn, n_bits):
    N, D = x2.shape
    nb = S // tn
    base = l * B * 6
    rl = _local_rows(tn)
    grid_spec = pltpu.PrefetchScalarGridSpec(
        num_scalar_prefetch=3,
        grid=(N // tn,),
        in_specs=[pl.BlockSpec(memory_space=pl.ANY),
                  pl.BlockSpec((tn, D), lambda i, a, b, c: (i, 0)),
                  pl.BlockSpec((tn, LANES), lambda i, a, b, c: (i, 0)),
                  pl.BlockSpec((tn, LANES), lambda i, a, b, c: (i, 0)),
                  pl.BlockSpec((1, 1, D), lambda i, a, b, c: (base + (i // nb) * 6 + 5, 0, 0))],
        out_specs=pl.BlockSpec((tn, D), lambda i, a, b, c: (i, 0)),
        scratch_shapes=[pltpu.VMEM((2, rl, D), ys.dtype), pltpu.SemaphoreType.DMA((2,))],
    )
    return pl.pallas_call(
        functools.partial(_combine_kernel, n_bits),
        grid_spec=grid_spec,
        out_shape=jax.ShapeDtypeStruct((N, D), F32),
        compiler_params=_cparams(("arbitrary",)),
        name="moe_combine",
    )(seg_src, seg_dst, seg_len, ys, x2, ls, wts, mod3)


def _tile_rows(S, want):
    t = min(S, want)
    while S % t:
        t //= 2
    return t


def _prep_in_weights(w_in_l):
    HW = HGRN_HEADS * HGRN_DIM
    o = 0
    wh = w_in_l[:, o:o + 4 * HW]; o += 4 * HW
    wql = w_in_l[:, o:o + MLA_Q_RANK]; o += MLA_Q_RANK
    wkvl = w_in_l[:, o:o + MLA_KV_RANK]; o += MLA_KV_RANK
    wkr = w_in_l[:, o:o + MLA_ROPE]; o += MLA_ROPE
    wp = w_in_l[:, o:o + HW]; o += HW
    wg = w_in_l[:, o:]
    D = w_in_l.shape[0]
    wkr_p = jnp.zeros((D, LANES), w_in_l.dtype).at[:, MLA_NOPE:MLA_NOPE + MLA_ROPE].set(wkr)
    wm = jnp.concatenate([wql, wkvl, wkr_p, _rope_partner(wkr_p)], axis=1)
    return wg.astype(BF16), wh.astype(BF16), wm.astype(BF16), wp.astype(BF16)


def _rope_partner(w):
    half = MLA_ROPE // 2
    w3 = w.reshape(w.shape[:-1] + (-1, LANES))
    first = w3[..., MLA_NOPE:MLA_NOPE + half]
    second = w3[..., MLA_NOPE + half:MLA_NOPE + MLA_ROPE]
    out = jnp.concatenate([jnp.zeros_like(w3[..., :MLA_NOPE]), second, first,
                           jnp.zeros_like(w3[..., MLA_NOPE + MLA_ROPE:])], axis=-1)
    return out.reshape(w.shape)


def _pad_heads(w, n_heads, width):
    K = w.shape[0]
    w3 = w.reshape(K, n_heads, width)
    return jnp.pad(w3, ((0, 0), (0, 0), (0, LANES - width))).reshape(K, n_heads * LANES)


def kernel(x, c, positions, ada_w, ada_b, norm1_g, norm2_g, w_in, hgrn_lb, hgrn_onorm_g, mla_qlat_g, mla_kvlat_g, w_uq, w_ukv, q_norm_g, k_norm_g, w_pool, pool_scale, w_br_a, w_br_b, w_br_c, w_out, w_router, b_router, w_exp1, b_exp1, w_exp2, b_exp2):
    B, S, D = x.shape
    L = ada_w.shape[0]
    N = B * S
    H = MLA_HEADS
    tr = _tile_rows(S, 512)
    tc = _tile_rows(S, 128)
    ta = _tile_rows(S, 512)
    tm = 512

    mod = _ada(c, ada_w, ada_b)
    mod3 = mod.reshape(L * B * 6, 1, D)

    inv_freq = 1.0 / (ROPE_THETA ** (jnp.arange(0, MLA_ROPE, 2, dtype=F32) / MLA_ROPE))
    freq_row = jnp.zeros((1, LANES), F32).at[0, MLA_NOPE:MLA_NOPE + MLA_ROPE].set(
        jnp.concatenate([inv_freq, inv_freq]))
    posf = positions.astype(F32).reshape(N, 1)
    cos_t, sin_t = _rope_tables(posf, freq_row, tr)

    max_rows = N * TOP_K + (N // tr) * N_EXPERTS * (SEG_ALIGN - 1)
    n_tiles = -(-max_rows // tm) + N_EXPERTS
    n_bits = (tr // SEG_ALIGN).bit_length()
    x2 = x.reshape(N, D)
    for l in range(L):
        wg, wh, wm, wp = _prep_in_weights(w_in[l])
        pg, ph, pm, pp = _inproj(x2, norm1_g[l].reshape(1, D), mod3, l, wg, wh, wm, wp, B, S, tr)

        ya = _hgrn(ph.reshape(B, S, -1), hgrn_lb, hgrn_onorm_g[l].reshape(1, HGRN_DIM), l, B, S, tc)

        wq_f = _pad_heads(w_uq[l], H, MLA_QK)
        wq_p = wq_f.astype(BF16)
        wqr_p = _rope_partner(wq_f).astype(BF16)
        wkv3 = w_ukv[l].reshape(MLA_KV_RANK, H, MLA_NOPE + MLA_V)
        wkv_p = jnp.concatenate(
            [_pad_heads(wkv3[:, :, :MLA_NOPE].reshape(MLA_KV_RANK, H * MLA_NOPE), H, MLA_NOPE),
             wkv3[:, :, MLA_NOPE:].reshape(MLA_KV_RANK, H * MLA_V)], axis=1).astype(BF16)
        qn_p = jnp.pad(q_norm_g[l], (0, LANES - MLA_QK)).reshape(1, LANES)
        kn_p = jnp.pad(k_norm_g[l], (0, LANES - MLA_QK)).reshape(1, LANES)
        q, k, v = _mla_prep(pm, cos_t, sin_t, mla_qlat_g[l].reshape(1, -1), mla_kvlat_g[l].reshape(1, -1),
                            wq_p, wqr_p, wkv_p, qn_p, _rope_partner(qn_p), kn_p, _rope_partner(kn_p),
                            B, S, tr)
        yb = _attention(q, k, v.reshape(B, S, H * MLA_V), B, S, ta)

        yc = _pool(pp.reshape(B, S, -1), w_pool[l], pool_scale[l].reshape(1, -1))

        wr_p = jnp.pad(w_router[l], ((0, 0), (0, LANES - N_EXPERTS)))
        br_p = jnp.concatenate([b_router[l], jnp.full((LANES - N_EXPERTS,), NEG_BIG, F32)]).reshape(1, LANES)
        x2, xl, ls, wts, cnt = _merge_router(
            x2, pg, ya.reshape(N, -1), yb.reshape(N, -1), yc.reshape(N, -1),
            w_br_a[l].astype(BF16), w_br_b[l].astype(BF16), w_br_c[l].astype(BF16), w_out[l].astype(BF16),
            norm2_g[l].reshape(1, D), mod3, l, wr_p, br_p, B, S, tr)

        seg = ((cnt[:, 0, :N_EXPERTS].astype(jnp.int32) + SEG_ALIGN - 1) // SEG_ALIGN) * SEG_ALIGN
        seg_src = jnp.cumsum(seg, axis=1) - seg
        group = jnp.sum(seg, axis=0)
        tiles_per = (group + tm - 1) // tm
        tile_end = jnp.cumsum(tiles_per)
        offsets = (tile_end - tiles_per) * tm
        seg_dst = offsets[None, :] + jnp.cumsum(seg, axis=0) - seg
        n_used = tile_end[-1:]
        tile_ids = jnp.minimum(jnp.arange(n_tiles, dtype=jnp.int32), n_used[0] - 1)
        tile_expert = jnp.minimum(
            jnp.sum((tile_end[None, :] <= tile_ids[:, None]).astype(jnp.int32), axis=1), N_EXPERTS - 1)
        all_tiles = jnp.arange(n_tiles, dtype=jnp.int32)
        is_tail = jnp.any((tile_end[None, :] - 1 == all_tiles[:, None]) & (tiles_per[None, :] > 0), axis=1)
        zero_flags = (is_tail | (all_tiles >= n_used[0])).astype(jnp.int32)
        seg_src = seg_src.reshape(-1).astype(jnp.int32)
        seg_dst = seg_dst.reshape(-1).astype(jnp.int32)
        seg_len = (seg // SEG_ALIGN).reshape(-1).astype(jnp.int32)

        xs = _place(seg_src, seg_dst, seg_len, zero_flags, xl, n_tiles * tm, _local_rows(tr), tm, n_bits)
        ys = _experts(tile_expert.astype(jnp.int32), n_used.astype(jnp.int32), xs,
                      w_exp1, b_exp1, w_exp2, b_exp2, l, tm)
        x2 = _combine(seg_src, seg_dst, seg_len, ys, x2, ls, wts, mod3, l, B, S, tr, n_bits)
    return x2.reshape(B, S, D)
```

```python
import functools
import math

import jax
import jax.numpy as jnp
from jax import lax
from jax.experimental import pallas as pl
from jax.experimental.pallas import tpu as pltpu

F32 = jnp.float32
BF16 = jnp.bfloat16

LANES = 128
NORM_EPS = 1e-6
HGRN_HEADS = 4
HGRN_DIM = 128
HGRN_CHUNK = 32
HGRN_ROWS = 8
MLA_HEADS = 8
MLA_NOPE = 64
MLA_ROPE = 32
MLA_QK = MLA_NOPE + MLA_ROPE
MLA_V = 64
MLA_Q_RANK = 384
MLA_KV_RANK = 256
ROPE_THETA = 10000.0
POOL_WINDOWS = (2, 4, 8, 16)
N_EXPERTS = 32
TOP_K = 4
SWIGLU_LIMIT = 7.0
SWIGLU_ALPHA = 1.702
NEG_BIG = -1e30
LOG2E = 1.4426950408889634
FACTOR_LIMIT = 100.0
SEG_ALIGN = 16
SORT_CHUNK = 256
EXPERT_ROW_STEPS = 4

VMEM_LIMIT = 56 * 1024 * 1024


def _cparams(sem):
    return pltpu.CompilerParams(dimension_semantics=sem, vmem_limit_bytes=VMEM_LIMIT)


def _dot(a, b):
    return jnp.dot(a, b, preferred_element_type=F32)


def _dot_nt(a, b):
    return lax.dot_general(a, b, (((1,), (1,)), ((), ())), preferred_element_type=F32)


def _dot_tn(a, b):
    return lax.dot_general(a, b, (((0,), (0,)), ((), ())), preferred_element_type=F32)


def _split3(x):
    hi = x.astype(BF16)
    r = x - hi.astype(F32)
    mid = r.astype(BF16)
    lo = (r - mid.astype(F32)).astype(BF16)
    return hi, mid, lo


def _sigmoid(x):
    return 1.0 / (1.0 + jnp.exp(-x))


def _ada_kernel(c_ref, w_ref, b_ref, o_ref):
    c = c_ref[...]
    cond = (c * _sigmoid(c)).astype(BF16)
    o_ref[0] = _dot(cond, w_ref[0].astype(BF16)) + b_ref[0]


def _ada(c, ada_w, ada_b):
    L, D, W = ada_w.shape
    B = c.shape[0]
    tn = D
    return pl.pallas_call(
        _ada_kernel,
        grid=(L, W // tn),
        in_specs=[
            pl.BlockSpec((B, D), lambda l, j: (0, 0)),
            pl.BlockSpec((1, D, tn), lambda l, j: (l, 0, j)),
            pl.BlockSpec((1, 1, tn), lambda l, j: (l, 0, j)),
        ],
        out_specs=pl.BlockSpec((1, B, tn), lambda l, j: (l, 0, j)),
        out_shape=jax.ShapeDtypeStruct((L, B, W), F32),
        compiler_params=_cparams(("parallel", "parallel")),
        name="ada_mod",
    )(c, ada_w, ada_b.reshape(L, 1, W))


def _inproj_kernel(x_ref, g_ref, sh_ref, sc_ref, wg_ref, wh_ref, wm_ref, wp_ref,
                   og_ref, oh_ref, om_ref, op_ref):
    x = x_ref[...]
    ms = jnp.mean(x * x, axis=-1, keepdims=True)
    h = x * lax.rsqrt(ms + NORM_EPS) * g_ref[...]
    h = h * (1.0 + sc_ref[0]) + sh_ref[0]
    hb = h.astype(BF16)
    cw = 512
    for w_ref, o_ref in ((wg_ref, og_ref), (wh_ref, oh_ref), (wm_ref, om_ref), (wp_ref, op_ref)):
        width = w_ref.shape[1]
        step = cw if width % cw == 0 else width
        for j in range(0, width, step):
            o_ref[:, j:j + step] = _dot(hb, w_ref[:, j:j + step]).astype(o_ref.dtype)


def _inproj(x2, g, mod3, l, wg, wh, wm, wp, B, S, tr):
    N, D = x2.shape
    nb = S // tr
    base = l * B * 6

    def modspec(j):
        return pl.BlockSpec((1, 1, D), lambda i: (base + (i // nb) * 6 + j, 0, 0))

    def wspec(w):
        return pl.BlockSpec(w.shape, lambda i: (0, 0), pipeline_mode=pl.Buffered(1))

    outs = (
        jax.ShapeDtypeStruct((N, wg.shape[1]), BF16),
        jax.ShapeDtypeStruct((N, wh.shape[1]), F32),
        jax.ShapeDtypeStruct((N, wm.shape[1]), F32),
        jax.ShapeDtypeStruct((N, wp.shape[1]), F32),
    )
    return pl.pallas_call(
        _inproj_kernel,
        grid=(N // tr,),
        in_specs=[
            pl.BlockSpec((tr, D), lambda i: (i, 0)),
            pl.BlockSpec((1, D), lambda i: (0, 0)),
            modspec(0), modspec(1),
            wspec(wg), wspec(wh), wspec(wm), wspec(wp),
        ],
        out_specs=tuple(pl.BlockSpec((tr, o.shape[1]), lambda i: (i, 0)) for o in outs),
        out_shape=outs,
        compiler_params=_cparams(("parallel",)),
        name="in_proj",
    )(x2, g, mod3, mod3, wg, wh, wm, wp)


def _hgrn_kernel(layer, n_chunks, q_ref, f_ref, i_ref, g_ref, lb_ref, on_ref, o_ref,
                 st_ref, stb_ref, oi_ref):
    C, DH, H = HGRN_CHUNK, HGRN_DIM, HGRN_HEADS

    @pl.when(pl.program_id(1) == 0)
    def _():
        st_ref[...] = jnp.zeros_like(st_ref)
        stb_ref[...] = jnp.zeros_like(stb_ref)

    lbr = lb_ref[...]
    e = jnp.exp(lbr - jnp.max(lbr, axis=0, keepdims=True))
    p = e / jnp.sum(e, axis=0, keepdims=True)
    lb = jnp.zeros((1, lbr.shape[1]), F32)
    for j in range(1, layer + 1):
        lb = lb + p[j:j + 1]
    log_lb = jnp.log(lb)
    log_1mlb = jnp.log1p(-lb)
    one_m_lb = 1.0 - lb
    onorm = jnp.tile(on_ref[...], (1, H))

    row_c = lax.broadcasted_iota(jnp.int32, (C, C), 0)
    col_c = lax.broadcasted_iota(jnp.int32, (C, C), 1)
    tri = (col_c <= row_c).astype(BF16)
    rs = lax.broadcasted_iota(jnp.int32, (H * C, H * C), 0)
    cs = lax.broadcasted_iota(jnp.int32, (H * C, H * C), 1)
    same_head_causal = jnp.logical_and(cs <= rs, cs >= rs - rs % C)
    row_1 = lax.broadcasted_iota(jnp.int32, (C, 1), 0)
    heads = [slice(h * DH, (h + 1) * DH) for h in range(H)]

    def stack(x):
        return jnp.concatenate([x[:, hs] for hs in heads], axis=0)

    def unstack(x):
        return jnp.concatenate([x[h * C:(h + 1) * C] for h in range(H)], axis=1)

    def prepare(r, sl):
        q = q_ref[r, sl, :]
        z = f_ref[r, sl, :]
        v = i_ref[r, sl, :]
        ez = jnp.exp(-jnp.abs(z))
        log_sig = jnp.minimum(z, 0.0) - jnp.log1p(ez)
        if layer == 0:
            lf = log_sig
            kk = jnp.where(z >= 0.0, ez, 1.0) / (1.0 + ez)
        else:
            a = log_1mlb + log_sig
            m = jnp.maximum(a, log_lb)
            lf = m + jnp.log(jnp.exp(a - m) + jnp.exp(log_lb - m))
            kk = one_m_lb * jnp.where(z >= 0.0, ez, 1.0) / (1.0 + ez)
        l1, l2, l3 = _split3(lf)
        b = (_dot(tri, l1) + _dot(tri, l2) + _dot(tri, l3)) * LOG2E
        qs = (q * jnp.exp2(b)).astype(BF16)
        o_inter = _dot_nt(qs, stb_ref[r])
        return q, v, kk, b, qs, v.astype(BF16), o_inter

    def intra_factored(r, prepared):
        _, _, kk, b, qs, vb, _ = prepared
        ks = (kk * jnp.exp2(-b)).astype(BF16)
        att = _dot_nt(stack(qs), stack(ks))
        att = jnp.where(same_head_causal, att, 0.0)
        oi_ref[r] = unstack(_dot(att.astype(BF16), stack(vb)))

    def intra_termwise(r, prepared):
        q, v, kk, b, _, _, _ = prepared
        for hs in heads:
            bh, qh, kh, vh = b[:, hs], q[:, hs], kk[:, hs], v[:, hs]
            acc = jnp.zeros((C, DH), F32)
            for s in range(C):
                es = jnp.exp2(jnp.minimum(bh - bh[s:s + 1], 0.0))
                col = jnp.sum(qh * es * kh[s:s + 1], axis=-1, keepdims=True)
                acc = acc + jnp.where(row_1 >= s, col, 0.0) * vh[s:s + 1]
            oi_ref[r, :, hs] = acc

    def finish(r, sl, prepared):
        _, _, kk, b, _, vb, o_inter = prepared
        bl = b[C - 1:C]
        o = oi_ref[r] + o_inter
        kd = (kk * jnp.exp2(bl - b)).astype(BF16)
        update = _dot_tn(vb, kd)
        decay = jnp.exp2(bl)
        normed = []
        for hs in heads:
            new = st_ref[r, hs, hs] * decay[:, hs] + update[hs, hs]
            st_ref[r, hs, hs] = new
            stb_ref[r, hs, hs] = new.astype(BF16)
            oh = o[:, hs]
            normed.append(oh * lax.rsqrt(jnp.mean(oh * oh, axis=-1, keepdims=True) + NORM_EPS))
        g = g_ref[r, sl, :]
        y = jnp.concatenate(normed, axis=1) * onorm * (g * _sigmoid(g))
        o_ref[r, sl, :] = y.astype(o_ref.dtype)

    def chunk(ci, carry):
        sl = pl.ds(pl.multiple_of(ci * C, C), C)
        rows = range(q_ref.shape[0])
        prepared = [prepare(r, sl) for r in rows]
        span = -prepared[0][3][C - 1:C]
        for p in prepared[1:]:
            span = jnp.maximum(span, -p[3][C - 1:C])
        factor_ok = jnp.max(span) < FACTOR_LIMIT

        @pl.when(factor_ok)
        def _():
            for r in rows:
                intra_factored(r, prepared[r])

        @pl.when(jnp.logical_not(factor_ok))
        def _():
            for r in rows:
                intra_termwise(r, prepared[r])

        for r in rows:
            finish(r, sl, prepared[r])
        return carry

    lax.fori_loop(0, n_chunks, chunk, 0)


def _hgrn(ph, hgrn_lb, onorm_g, layer, B, S, tc):
    L = hgrn_lb.shape[0]
    W = HGRN_HEADS * HGRN_DIM

    nb = math.gcd(B, HGRN_ROWS)

    def spec(off):
        return pl.BlockSpec((nb, tc, W), lambda b, t: (b, t, off))

    return pl.pallas_call(
        functools.partial(_hgrn_kernel, layer, tc // HGRN_CHUNK),
        grid=(B // nb, S // tc),
        in_specs=[spec(0), spec(1), spec(2), spec(3),
                  pl.BlockSpec((L, W), lambda b, t: (0, 0)),
                  pl.BlockSpec((1, HGRN_DIM), lambda b, t: (0, 0))],
        out_specs=pl.BlockSpec((nb, tc, W), lambda b, t: (b, t, 0)),
        out_shape=jax.ShapeDtypeStruct((B, S, W), BF16),
        scratch_shapes=[pltpu.VMEM((nb, W, W), F32), pltpu.VMEM((nb, W, W), BF16),
                        pltpu.VMEM((nb, HGRN_CHUNK, W), F32)],
        compiler_params=_cparams(("parallel", "arbitrary")),
        name="hgrn2",
    )(ph, ph, ph, ph, hgrn_lb, onorm_g)


def _rope_table_kernel(pos_ref, fr_ref, cos_ref, sin_ref):
    ang = pos_ref[...] * fr_ref[...]
    lane = lax.broadcasted_iota(jnp.int32, ang.shape, 1)
    sn = jnp.sin(ang)
    cos_ref[...] = jnp.cos(ang)
    sin_ref[...] = jnp.where(lane < MLA_NOPE + MLA_ROPE // 2, -sn, sn)


def _rope_tables(posf, freq_row, tr):
    N = posf.shape[0]
    out = jax.ShapeDtypeStruct((N, LANES), F32)
    return pl.pallas_call(
        _rope_table_kernel,
        grid=(N // tr,),
        in_specs=[pl.BlockSpec((tr, 1), lambda i: (i, 0)), pl.BlockSpec((1, LANES), lambda i: (0, 0))],
        out_specs=(pl.BlockSpec((tr, LANES), lambda i: (i, 0)), pl.BlockSpec((tr, LANES), lambda i: (i, 0))),
        out_shape=(out, out),
        compiler_params=_cparams(("parallel",)),
        name="rope_tables",
    )(posf, freq_row)


def _mla_prep_kernel(pm_ref, cos_ref, sin_ref, qg_ref, kvg_ref, wq_ref, wqr_ref, wkv_ref,
                     qn_ref, qnr_ref, kn_ref, knr_ref, q_ref, k_ref, v_ref):
    pm = pm_ref[...]
    o = 0
    ql = pm[:, o:o + MLA_Q_RANK]; o += MLA_Q_RANK
    kvl = pm[:, o:o + MLA_KV_RANK]; o += MLA_KV_RANK
    kr = pm[:, o:o + LANES]; o += LANES
    krr = pm[:, o:o + LANES]

    def rms(x, g):
        return x * lax.rsqrt(jnp.mean(x * x, axis=-1, keepdims=True) + NORM_EPS) * g

    qlb = rms(ql, qg_ref[...]).astype(BF16)
    qa = _dot(qlb, wq_ref[...])
    qr = _dot(qlb, wqr_ref[...])
    kva = _dot(rms(kvl, kvg_ref[...]).astype(BF16), wkv_ref[...])
    v_ref[...] = kva[:, MLA_HEADS * LANES:].astype(v_ref.dtype)

    cs = cos_ref[...]
    sn = sin_ref[...]
    scale = LOG2E / math.sqrt(MLA_QK)
    q_cos = qn_ref[...] * cs * scale
    q_sin = qnr_ref[...] * sn * scale
    k_cos = kn_ref[...] * cs
    k_sin = knr_ref[...] * sn
    k_rot = krr * k_sin

    def inv_rms(xh):
        return lax.rsqrt(jnp.sum(xh * xh, axis=-1, keepdims=True) * (1.0 / MLA_QK) + NORM_EPS)

    for h in range(MLA_HEADS):
        hs = slice(h * LANES, (h + 1) * LANES)
        qh = qa[:, hs]
        q_ref[0, h] = ((qh * q_cos + qr[:, hs] * q_sin) * inv_rms(qh)).astype(q_ref.dtype)
        kh = kva[:, hs] + kr
        k_ref[0, h] = ((kh * k_cos + k_rot) * inv_rms(kh)).astype(k_ref.dtype)


def _mla_prep(pm, cos_t, sin_t, qlat_g, kvlat_g, wq, wqr, wkv, qn, qnr, kn, knr, B, S, tr):
    N = pm.shape[0]
    nb = S // tr
    H = MLA_HEADS
    full = lambda a: pl.BlockSpec(a.shape, lambda i: (0,) * a.ndim)
    return pl.pallas_call(
        _mla_prep_kernel,
        grid=(N // tr,),
        in_specs=[pl.BlockSpec((tr, pm.shape[1]), lambda i: (i, 0)),
                  pl.BlockSpec((tr, LANES), lambda i: (i, 0)),
                  pl.BlockSpec((tr, LANES), lambda i: (i, 0)),
                  full(qlat_g), full(kvlat_g), full(wq), full(wqr), full(wkv),
                  full(qn), full(qnr), full(kn), full(knr)],
        out_specs=(pl.BlockSpec((1, H, tr, LANES), lambda i: (i // nb, 0, i % nb, 0)),
                   pl.BlockSpec((1, H, tr, LANES), lambda i: (i // nb, 0, i % nb, 0)),
                   pl.BlockSpec((tr, H * MLA_V), lambda i: (i, 0))),
        out_shape=(jax.ShapeDtypeStruct((B, H, S, LANES), BF16),
                   jax.ShapeDtypeStruct((B, H, S, LANES), BF16),
                   jax.ShapeDtypeStruct((N, H * MLA_V), BF16)),
        compiler_params=_cparams(("parallel",)),
        name="mla_prep",
    )(pm, cos_t, sin_t, qlat_g, kvlat_g, wq, wqr, wkv, qn, qnr, kn, knr)


def _attn_kernel(q_ref, k_ref, v_ref, o_ref, m_ref, l_ref, acc_ref):
    qi = pl.program_id(2)
    T = q_ref.shape[2]

    m_ref[...] = jnp.full_like(m_ref, NEG_BIG)
    l_ref[...] = jnp.zeros_like(l_ref)
    acc_ref[...] = jnp.zeros_like(acc_ref)

    def step(ki, masked):
        rows = pl.ds(pl.multiple_of(ki * T, T), T)
        vblk = v_ref[0, rows, :]
        lane = lax.broadcasted_iota(jnp.int32, vblk.shape, 1)
        if masked:
            row = lax.broadcasted_iota(jnp.int32, (T, T), 0)
            col = lax.broadcasted_iota(jnp.int32, (T, T), 1)
            keep = col <= row
        for h in range(2):
            s = _dot_nt(q_ref[0, h], k_ref[0, h, rows, :])
            if masked:
                s = jnp.where(keep, s, NEG_BIG)
            m_old = m_ref[h]
            m_new = jnp.maximum(m_old, jnp.max(s, axis=-1, keepdims=True))
            alpha = jnp.exp2(m_old - m_new)
            p = jnp.exp2(s - jnp.tile(m_new, (1, T // LANES)))
            l_ref[h] = alpha * l_ref[h] + jnp.sum(p, axis=-1, keepdims=True)
            vh = jnp.where((lane < MLA_V) if h == 0 else (lane >= MLA_V), vblk, jnp.zeros_like(vblk))
            acc_ref[h] = alpha * acc_ref[h] + _dot(p.astype(BF16), vh)
            m_ref[h] = m_new

    def below_diagonal(ki, c):
        step(ki, False)
        return c

    lax.fori_loop(0, qi, below_diagonal, 0)
    step(qi, True)
    o_ref[0] = (acc_ref[0] / l_ref[0] + acc_ref[1] / l_ref[1]).astype(o_ref.dtype)


def _attention(q, k, v3, B, S, T):
    H = MLA_HEADS
    n = S // T
    return pl.pallas_call(
        _attn_kernel,
        grid=(B, H // 2, n),
        in_specs=[pl.BlockSpec((1, 2, T, LANES), lambda b, hp, qi: (b, hp, qi, 0)),
                  pl.BlockSpec((1, 2, S, LANES), lambda b, hp, qi: (b, hp, 0, 0)),
                  pl.BlockSpec((1, S, LANES), lambda b, hp, qi: (b, 0, hp))],
        out_specs=pl.BlockSpec((1, T, LANES), lambda b, hp, qi: (b, qi, hp)),
        out_shape=jax.ShapeDtypeStruct((B, S, H * MLA_V), BF16),
        scratch_shapes=[pltpu.VMEM((2, T, LANES), F32), pltpu.VMEM((2, T, LANES), F32),
                        pltpu.VMEM((2, T, LANES), F32)],
        compiler_params=_cparams(("parallel", "parallel", "arbitrary")),
        name="mla_attention",
    )(q, k, v3)


def _pool_kernel(u_ref, w_ref, sc_ref, o_ref):
    S = u_ref.shape[1]
    row = lax.broadcasted_iota(jnp.int32, (S, LANES), 0)
    t1 = (row + 1).astype(F32)
    for gi, win in enumerate(POOL_WINDOWS):
        u = u_ref[0, :, gi * LANES:(gi + 1) * LANES]
        s = u
        k = 1
        while k < win:
            s = s + jnp.where(row >= k, pltpu.roll(s, k, 0), 0.0)
            k *= 2
        pooled = s / jnp.minimum(t1, float(win))
        mixed = _dot((pooled - u).astype(BF16), w_ref[gi].astype(BF16))
        o_ref[0, :, gi * LANES:(gi + 1) * LANES] = (
            mixed * sc_ref[:, gi * LANES:(gi + 1) * LANES]).astype(o_ref.dtype)


def _pool(pp3, w_pool, scale):
    B, S, W = pp3.shape
    return pl.pallas_call(
        _pool_kernel,
        grid=(B,),
        in_specs=[pl.BlockSpec((1, S, W), lambda b: (b, 0, 0)),
                  pl.BlockSpec(w_pool.shape, lambda b: (0, 0, 0)),
                  pl.BlockSpec((1, W), lambda b: (0, 0))],
        out_specs=pl.BlockSpec((1, S, W), lambda b: (b, 0, 0)),
        out_shape=jax.ShapeDtypeStruct((B, S, W), BF16),
        compiler_params=_cparams(("parallel",)),
        name="pool_mixer",
    )(pp3, w_pool, scale)


def _merge_router_kernel(x_ref, gt_ref, ya_ref, yb_ref, yc_ref, wa_ref, wb_ref, wc_ref, wo_ref, g1_ref,
                         g_ref, sh_ref, sc_ref, wr_ref, br_ref, xo_ref, hb_ref, ls_ref, wt_ref, cnt_ref):
    D = x_ref.shape[1]
    merged = None
    for j, (y_ref, w_ref) in enumerate(((ya_ref, wa_ref), (yb_ref, wb_ref), (yc_ref, wc_ref))):
        y = _dot(y_ref[...], w_ref[...])
        gate = _sigmoid(gt_ref[:, j * D:(j + 1) * D].astype(F32))
        merged = gate * y if merged is None else merged + gate * y
    x = x_ref[...] + g1_ref[0] * _dot(merged.astype(BF16), wo_ref[...])
    xo_ref[...] = x
    _route(x, g_ref, sh_ref, sc_ref, wr_ref, br_ref, hb_ref, ls_ref, wt_ref, cnt_ref)


def _route(x, g_ref, sh_ref, sc_ref, wr_ref, br_ref, hb_ref, ls_ref, wt_ref, cnt_ref):
    tr = x.shape[0]

    ms = jnp.mean(x * x, axis=-1, keepdims=True)
    h = x * lax.rsqrt(ms + NORM_EPS) * g_ref[...]
    h = h * (1.0 + sc_ref[0]) + sh_ref[0]
    hb_ref[...] = h.astype(hb_ref.dtype)
    h1, h2, _ = _split3(h)
    w1, w2, _ = _split3(wr_ref[...])
    logits = _dot(h1, w1) + (_dot(h1, w2) + _dot(h2, w1)) + br_ref[...]
    lane = lax.broadcasted_iota(jnp.int32, logits.shape, 1)
    work = logits
    vals, ids = [], []
    for _k in range(TOP_K):
        m = jnp.max(work, axis=-1, keepdims=True)
        sel = jnp.min(jnp.where(work == m, lane, LANES), axis=-1, keepdims=True)
        vals.append(m)
        ids.append(sel)
        work = jnp.where(lane == sel, -jnp.inf, work)
    es = [jnp.exp(v - vals[0]) for v in vals]
    den = es[0] + es[1] + es[2] + es[3]
    onehot = jnp.zeros(logits.shape, F32)
    wt_out = jnp.zeros(logits.shape, F32)
    for k in range(TOP_K):
        onehot = onehot + (lane == ids[k]).astype(F32)
        wt_out = jnp.where(lane == k, es[k] / den, wt_out)
    wt_ref[...] = wt_out

    r = lax.broadcasted_iota(jnp.int32, (tr, tr), 0)
    c = lax.broadcasted_iota(jnp.int32, (tr, tr), 1)
    strict = (c < r).astype(BF16)
    before = _dot(strict, onehot.astype(BF16))
    cnt = jnp.sum(onehot, axis=0, keepdims=True)
    cnt_ref[0] = jnp.broadcast_to(cnt, cnt_ref.shape[1:])
    units = jnp.floor((cnt + (SEG_ALIGN - 1.0)) * (1.0 / SEG_ALIGN))
    er = lax.broadcasted_iota(jnp.int32, (LANES, LANES), 0)
    ec = lax.broadcasted_iota(jnp.int32, (LANES, LANES), 1)
    upper = (er < ec).astype(BF16)
    seg_start = _dot(jnp.broadcast_to(units, (8, LANES)).astype(BF16), upper)[0:1] * float(SEG_ALIGN)
    pos = before + seg_start
    ls_out = jnp.full(logits.shape, -1.0, F32)
    for k in range(TOP_K):
        lk = jnp.sum(jnp.where(lane == ids[k], pos, 0.0), axis=-1, keepdims=True)
        ls_out = jnp.where(lane == k, lk, ls_out)
    ls_ref[...] = ls_out.astype(jnp.int32)


def _local_rows(tb):
    return TOP_K * tb + N_EXPERTS * SEG_ALIGN


def _sort_chunk(rl):
    return SORT_CHUNK if rl % SORT_CHUNK == 0 else SORT_CHUNK // 2


def _merge_router(x2, gates, ya, yb, yc, wa, wb, wc, wo, g, mod3, l, wr_p, br_p, B, S, tr):
    N, D = x2.shape
    nb = S // tr
    base = l * B * 6
    full = lambda a: pl.BlockSpec(a.shape, lambda i: (0,) * a.ndim)
    rows = lambda w: pl.BlockSpec((tr, w), lambda i: (i, 0))

    def modspec(j):
        return pl.BlockSpec((1, 1, D), lambda i: (base + (i // nb) * 6 + j, 0, 0))

    return pl.pallas_call(
        _merge_router_kernel,
        grid=(N // tr,),
        in_specs=[rows(D), rows(gates.shape[1]), rows(ya.shape[1]), rows(yb.shape[1]), rows(yc.shape[1]),
                  full(wa), full(wb), full(wc), full(wo), modspec(2),
                  full(g), modspec(3), modspec(4), full(wr_p), full(br_p)],
        out_specs=(rows(D), rows(D), rows(LANES), rows(LANES),
                   pl.BlockSpec((1, 8, LANES), lambda i: (i, 0, 0))),
        out_shape=(jax.ShapeDtypeStruct((N, D), F32),
                   jax.ShapeDtypeStruct((N, D), BF16),
                   jax.ShapeDtypeStruct((N, LANES), jnp.int32),
                   jax.ShapeDtypeStruct((N, LANES), F32),
                   jax.ShapeDtypeStruct((N // tr, 8, LANES), F32)),
        compiler_params=_cparams(("parallel",)),
        name="merge_router",
    )(x2, gates, ya, yb, yc, wa, wb, wc, wo, mod3, g, mod3, mod3, wr_p, br_p)


def _segment_copies(j, e, src_ref, dst_ref, len_ref, local_ref, global_ref, sem, to_global, n_bits):
    base = j * N_EXPERTS + e
    units = len_ref[base]
    s0 = src_ref[base]
    d0 = dst_ref[base]
    copies = []
    for bit in range(n_bits):
        rows = SEG_ALIGN << bit
        done = ((units >> (bit + 1)) << (bit + 1)) * SEG_ALIGN
        loc = local_ref.at[pl.ds(pl.multiple_of(s0 + done, SEG_ALIGN), rows)]
        glo = global_ref.at[pl.ds(pl.multiple_of(d0 + done, SEG_ALIGN), rows)]
        cp = pltpu.make_async_copy(loc, glo, sem) if to_global else pltpu.make_async_copy(glo, loc, sem)
        copies.append((((units >> bit) & 1) == 1, cp))
    return copies


def _each_segment_copy(j, src_ref, dst_ref, len_ref, local_ref, global_ref, sem, to_global, n_bits, action):
    def per_expert(e, c):
        for pred, cp in _segment_copies(j, e, src_ref, dst_ref, len_ref, local_ref, global_ref,
                                        sem, to_global, n_bits):
            @pl.when(pred)
            def _():
                action(cp)
        return c
    lax.fori_loop(0, N_EXPERTS, per_expert, 0)


def _sort_place_kernel(n_bits, src_ref, dst_ref, len_ref, zflag_ref, hb_ref, ls_ref, xs_hbm,
                       zbuf, xl_buf, zsem, sem):
    j = pl.program_id(0)
    last = pl.num_programs(0) - 1
    tr = hb_ref.shape[0]
    tm = zbuf.shape[0]
    rl = xl_buf.shape[1]
    n_tiles = xs_hbm.shape[0] // tm
    slot = j % 2

    def segments(block, buf_slot, action):
        _each_segment_copy(block, src_ref, dst_ref, len_ref, xl_buf.at[buf_slot], xs_hbm,
                           sem.at[buf_slot], True, n_bits, action)

    @pl.when(j == 0)
    def _():
        zbuf[...] = jnp.zeros_like(zbuf)

        def zero_copy(t):
            return pltpu.make_async_copy(zbuf, xs_hbm.at[pl.ds(pl.multiple_of(t * tm, tm), tm)], zsem)

        def zstart(t, c):
            @pl.when(zflag_ref[t] > 0)
            def _():
                zero_copy(t).start()
            return c

        def zwait(t, c):
            @pl.when(zflag_ref[t] > 0)
            def _():
                zero_copy(t).wait()
            return c

        lax.fori_loop(0, n_tiles, zstart, 0)
        lax.fori_loop(0, n_tiles, zwait, 0)

    @pl.when(j >= 2)
    def _():
        segments(j - 2, slot, lambda cp: cp.wait())

    ls_t = jnp.transpose(ls_ref[...].astype(F32))
    targets = [ls_t[k:k + 1, :].astype(jnp.int32) for k in range(TOP_K)]
    hb = hb_ref[...]
    ch = _sort_chunk(rl)
    for cidx in range(rl // ch):
        ri = lax.broadcasted_iota(jnp.int32, (ch, tr), 0) + cidx * ch
        hit = ri == targets[0]
        for k in range(1, TOP_K):
            hit = jnp.logical_or(hit, ri == targets[k])
        perm = jnp.where(hit, 1.0, 0.0).astype(BF16)
        xl_buf[slot, cidx * ch:(cidx + 1) * ch, :] = _dot(perm, hb).astype(xl_buf.dtype)

    segments(j, slot, lambda cp: cp.start())

    @pl.when(j == last)
    def _():
        @pl.when(j >= 1)
        def _():
            segments(j - 1, 1 - slot, lambda cp: cp.wait())
        segments(j, slot, lambda cp: cp.wait())


def _sort_place(seg_src, seg_dst, seg_len, zero_flags, hb, ls, n_rows, tr, tm, n_bits):
    N, D = hb.shape
    rl = _local_rows(tr)
    grid_spec = pltpu.PrefetchScalarGridSpec(
        num_scalar_prefetch=4,
        grid=(N // tr,),
        in_specs=[pl.BlockSpec((tr, D), lambda j, a, b, c, d: (j, 0)),
                  pl.BlockSpec((tr, LANES), lambda j, a, b, c, d: (j, 0))],
        out_specs=pl.BlockSpec(memory_space=pl.ANY),
        scratch_shapes=[pltpu.VMEM((tm, D), hb.dtype), pltpu.VMEM((2, rl, D), hb.dtype),
                        pltpu.SemaphoreType.DMA(()), pltpu.SemaphoreType.DMA((2,))],
    )
    return pl.pallas_call(
        functools.partial(_sort_place_kernel, n_bits),
        grid_spec=grid_spec,
        out_shape=jax.ShapeDtypeStruct((n_rows, D), hb.dtype),
        compiler_params=_cparams(("arbitrary",)),
        name="moe_sort_place",
    )(seg_src, seg_dst, seg_len, zero_flags, hb, ls)


def _expert_kernel(te_ref, rows_ref, nu_ref, x_ref, w1_ref, b1_ref, w2_ref, b2_ref, o_ref, w1b, w2b):
    t = pl.program_id(0)
    tm = x_ref.shape[0]
    F = w2_ref.shape[1]
    rows = rows_ref[t]

    @pl.when(jnp.logical_and(rows > 0, jnp.logical_or(t == 0, te_ref[t] != te_ref[jnp.maximum(t - 1, 0)])))
    def _():
        w1b[...] = w1_ref[0].astype(BF16)
        w2b[...] = w2_ref[0].astype(BF16)

    def mlp(m):
        a = _dot(x_ref[:m, :], w1b[...]) + b1_ref[0]
        glu = jnp.minimum(a[:, :F], SWIGLU_LIMIT)
        lin = jnp.clip(a[:, F:], -SWIGLU_LIMIT, SWIGLU_LIMIT)
        act = glu * _sigmoid(SWIGLU_ALPHA * glu) * (lin + 1.0)
        o_ref[:m, :] = (_dot(act.astype(BF16), w2b[...]) + b2_ref[0]).astype(o_ref.dtype)
        if m < tm:
            o_ref[m:, :] = jnp.zeros((tm - m, o_ref.shape[1]), o_ref.dtype)

    quarter = tm // EXPERT_ROW_STEPS
    for q in range(1, EXPERT_ROW_STEPS + 1):
        @pl.when(jnp.logical_and(rows > (q - 1) * quarter, rows <= q * quarter))
        def _():
            mlp(q * quarter)

    @pl.when(rows == 0)
    def _():
        o_ref[...] = jnp.zeros_like(o_ref)


def _experts(tile_expert, tile_rows, n_used, xs, w1_all, b1_all, w2_all, b2_all, l, tm):
    n_tiles = tile_expert.shape[0]
    D = xs.shape[1]
    L, E, _, F2 = w1_all.shape
    F = w2_all.shape[2]
    w1 = w1_all.reshape(L * E, D, F2)
    w2 = w2_all.reshape(L * E, F, D)
    b1 = b1_all.reshape(L * E, F2)
    b2 = b2_all.reshape(L * E, D)
    E0 = l * E
    E = L * E
    grid_spec = pltpu.PrefetchScalarGridSpec(
        num_scalar_prefetch=3,
        grid=(n_tiles,),
        in_specs=[pl.BlockSpec((tm, D), lambda t, te, tr, nu: (jnp.minimum(t, nu[0] - 1), 0)),
                  pl.BlockSpec((1, D, F2), lambda t, te, tr, nu: (E0 + te[t], 0, 0)),
                  pl.BlockSpec((1, 1, F2), lambda t, te, tr, nu: (E0 + te[t], 0, 0)),
                  pl.BlockSpec((1, F, D), lambda t, te, tr, nu: (E0 + te[t], 0, 0)),
                  pl.BlockSpec((1, 1, D), lambda t, te, tr, nu: (E0 + te[t], 0, 0))],
        out_specs=pl.BlockSpec((tm, D), lambda t, te, tr, nu: (t, 0)),
        scratch_shapes=[pltpu.VMEM((D, F2), BF16), pltpu.VMEM((F, D), BF16)],
    )
    return pl.pallas_call(
        _expert_kernel,
        grid_spec=grid_spec,
        out_shape=jax.ShapeDtypeStruct((n_tiles * tm, D), BF16),
        compiler_params=_cparams(("arbitrary",)),
        name="expert_mlp",
    )(tile_expert, tile_rows, n_used, xs, w1, b1.reshape(E, 1, F2), w2, b2.reshape(E, 1, D))


def _combine_kernel(n_bits, src_ref, dst_ref, len_ref, ys_hbm, x_ref, ls_ref, wt_ref, g2_ref, o_ref,
                    ybuf, sem):
    j = pl.program_id(0)
    tn = x_ref.shape[0]
    rl = ybuf.shape[1]
    slot = j % 2

    def fetch(block, buf_slot, action):
        _each_segment_copy(block, src_ref, dst_ref, len_ref, ybuf.at[buf_slot], ys_hbm,
                           sem.at[buf_slot], False, n_bits, action)

    def start_fetch(block, buf_slot):
        ybuf[buf_slot] = jnp.zeros((rl, ybuf.shape[2]), ybuf.dtype)
        fetch(block, buf_slot, lambda cp: cp.start())

    @pl.when(j == 0)
    def _():
        start_fetch(j, slot)

    @pl.when(j + 1 < pl.num_programs(0))
    def _():
        start_fetch(j + 1, 1 - slot)

    fetch(j, slot, lambda cp: cp.wait())
    ls = ls_ref[...]
    wt = wt_ref[...]
    ch = _sort_chunk(rl)
    y = jnp.zeros(x_ref.shape, F32)
    for cidx in range(rl // ch):
        li = lax.broadcasted_iota(jnp.int32, (tn, ch), 1) + cidx * ch
        pw = jnp.zeros((tn, ch), F32)
        for k in range(TOP_K):
            pw = jnp.where(li == ls[:, k:k + 1], wt[:, k:k + 1], pw)
        y = y + _dot(pw.astype(BF16), ybuf[slot, cidx * ch:(cidx + 1) * ch, :])
    o_ref[...] = x_ref[...] + g2_ref[0] * y


def _combine(seg_src, seg_dst, seg_len, ys, x2, ls, wts, mod3, l, B, S, tn, n_bits):
    N, D = x2.shape
    nb = S // tn
    base = l * B * 6
    rl = _local_rows(tn)
    grid_spec = pltpu.PrefetchScalarGridSpec(
        num_scalar_prefetch=3,
        grid=(N // tn,),
        in_specs=[pl.BlockSpec(memory_space=pl.ANY),
                  pl.BlockSpec((tn, D), lambda i, a, b, c: (i, 0)),
                  pl.BlockSpec((tn, LANES), lambda i, a, b, c: (i, 0)),
                  pl.BlockSpec((tn, LANES), lambda i, a, b, c: (i, 0)),
                  pl.BlockSpec((1, 1, D), lambda i, a, b, c: (base + (i // nb) * 6 + 5, 0, 0))],
        out_specs=pl.BlockSpec((tn, D), lambda i, a, b, c: (i, 0)),
        scratch_shapes=[pltpu.VMEM((2, rl, D), ys.dtype), pltpu.SemaphoreType.DMA((2,))],
    )
    return pl.pallas_call(
        functools.partial(_combine_kernel, n_bits),
        grid_spec=grid_spec,
        out_shape=jax.ShapeDtypeStruct((N, D), F32),
        compiler_params=_cparams(("arbitrary",)),
        name="moe_combine",
    )(seg_src, seg_dst, seg_len, ys, x2, ls, wts, mod3)


def _tile_rows(S, want):
    t = min(S, want)
    while S % t:
        t //= 2
    return t


def _prep_in_weights(w_in_l):
    HW = HGRN_HEADS * HGRN_DIM
    o = 0
    wh = w_in_l[:, o:o + 4 * HW]; o += 4 * HW
    wql = w_in_l[:, o:o + MLA_Q_RANK]; o += MLA_Q_RANK
    wkvl = w_in_l[:, o:o + MLA_KV_RANK]; o += MLA_KV_RANK
    wkr = w_in_l[:, o:o + MLA_ROPE]; o += MLA_ROPE
    wp = w_in_l[:, o:o + HW]; o += HW
    wg = w_in_l[:, o:]
    D = w_in_l.shape[0]
    wkr_p = jnp.zeros((D, LANES), w_in_l.dtype).at[:, MLA_NOPE:MLA_NOPE + MLA_ROPE].set(wkr)
    wm = jnp.concatenate([wql, wkvl, wkr_p, _rope_partner(wkr_p)], axis=1)
    return wg.astype(BF16), wh.astype(BF16), wm.astype(BF16), wp.astype(BF16)


def _rope_partner(w):
    half = MLA_ROPE // 2
    w3 = w.reshape(w.shape[:-1] + (-1, LANES))
    first = w3[..., MLA_NOPE:MLA_NOPE + half]
    second = w3[..., MLA_NOPE + half:MLA_NOPE + MLA_ROPE]
    out = jnp.concatenate([jnp.zeros_like(w3[..., :MLA_NOPE]), second, first,
                           jnp.zeros_like(w3[..., MLA_NOPE + MLA_ROPE:])], axis=-1)
    return out.reshape(w.shape)


def _pad_heads(w, n_heads, width):
    K = w.shape[0]
    w3 = w.reshape(K, n_heads, width)
    return jnp.pad(w3, ((0, 0), (0, 0), (0, LANES - width))).reshape(K, n_heads * LANES)


def kernel(x, c, positions, ada_w, ada_b, norm1_g, norm2_g, w_in, hgrn_lb, hgrn_onorm_g, mla_qlat_g, mla_kvlat_g, w_uq, w_ukv, q_norm_g, k_norm_g, w_pool, pool_scale, w_br_a, w_br_b, w_br_c, w_out, w_router, b_router, w_exp1, b_exp1, w_exp2, b_exp2):
    B, S, D = x.shape
    L = ada_w.shape[0]
    N = B * S
    H = MLA_HEADS
    tr = _tile_rows(S, 512)
    tc = _tile_rows(S, 128)
    ta = _tile_rows(S, 512)
    tm = 512

    mod = _ada(c, ada_w, ada_b)
    mod3 = mod.reshape(L * B * 6, 1, D)

    inv_freq = 1.0 / (ROPE_THETA ** (jnp.arange(0, MLA_ROPE, 2, dtype=F32) / MLA_ROPE))
    freq_row = jnp.zeros((1, LANES), F32).at[0, MLA_NOPE:MLA_NOPE + MLA_ROPE].set(
        jnp.concatenate([inv_freq, inv_freq]))
    posf = positions.astype(F32).reshape(N, 1)
    cos_t, sin_t = _rope_tables(posf, freq_row, tr)

    max_rows = N * TOP_K + (N // tr) * N_EXPERTS * (SEG_ALIGN - 1)
    n_tiles = -(-max_rows // tm) + N_EXPERTS
    n_bits = (tr // SEG_ALIGN).bit_length()
    x2 = x.reshape(N, D)
    for l in range(L):
        wg, wh, wm, wp = _prep_in_weights(w_in[l])
        pg, ph, pm, pp = _inproj(x2, norm1_g[l].reshape(1, D), mod3, l, wg, wh, wm, wp, B, S, tr)

        ya = _hgrn(ph.reshape(B, S, -1), hgrn_lb, hgrn_onorm_g[l].reshape(1, HGRN_DIM), l, B, S, tc)

        wq_f = _pad_heads(w_uq[l], H, MLA_QK)
        wq_p = wq_f.astype(BF16)
        wqr_p = _rope_partner(wq_f).astype(BF16)
        wkv3 = w_ukv[l].reshape(MLA_KV_RANK, H, MLA_NOPE + MLA_V)
        wkv_p = jnp.concatenate(
            [_pad_heads(wkv3[:, :, :MLA_NOPE].reshape(MLA_KV_RANK, H * MLA_NOPE), H, MLA_NOPE),
             wkv3[:, :, MLA_NOPE:].reshape(MLA_KV_RANK, H * MLA_V)], axis=1).astype(BF16)
        qn_p = jnp.pad(q_norm_g[l], (0, LANES - MLA_QK)).reshape(1, LANES)
        kn_p = jnp.pad(k_norm_g[l], (0, LANES - MLA_QK)).reshape(1, LANES)
        q, k, v = _mla_prep(pm, cos_t, sin_t, mla_qlat_g[l].reshape(1, -1), mla_kvlat_g[l].reshape(1, -1),
                            wq_p, wqr_p, wkv_p, qn_p, _rope_partner(qn_p), kn_p, _rope_partner(kn_p),
                            B, S, tr)
        yb = _attention(q, k, v.reshape(B, S, H * MLA_V), B, S, ta)

        yc = _pool(pp.reshape(B, S, -1), w_pool[l], pool_scale[l].reshape(1, -1))

        wr_p = jnp.pad(w_router[l], ((0, 0), (0, LANES - N_EXPERTS)))
        br_p = jnp.concatenate([b_router[l], jnp.full((LANES - N_EXPERTS,), NEG_BIG, F32)]).reshape(1, LANES)
        x2, hb, ls, wts, cnt = _merge_router(
            x2, pg, ya.reshape(N, -1), yb.reshape(N, -1), yc.reshape(N, -1),
            w_br_a[l].astype(BF16), w_br_b[l].astype(BF16), w_br_c[l].astype(BF16), w_out[l].astype(BF16),
            norm2_g[l].reshape(1, D), mod3, l, wr_p, br_p, B, S, tr)

        seg = ((cnt[:, 0, :N_EXPERTS].astype(jnp.int32) + SEG_ALIGN - 1) // SEG_ALIGN) * SEG_ALIGN
        seg_src = jnp.cumsum(seg, axis=1) - seg
        group = jnp.sum(seg, axis=0)
        tiles_per = (group + tm - 1) // tm
        tile_end = jnp.cumsum(tiles_per)
        offsets = (tile_end - tiles_per) * tm
        seg_dst = offsets[None, :] + jnp.cumsum(seg, axis=0) - seg
        n_used = tile_end[-1:]
        tile_ids = jnp.minimum(jnp.arange(n_tiles, dtype=jnp.int32), n_used[0] - 1)
        tile_expert = jnp.minimum(
            jnp.sum((tile_end[None, :] <= tile_ids[:, None]).astype(jnp.int32), axis=1), N_EXPERTS - 1)
        all_tiles = jnp.arange(n_tiles, dtype=jnp.int32)
        is_tail = jnp.any((tile_end[None, :] - 1 == all_tiles[:, None]) & (tiles_per[None, :] > 0), axis=1)
        zero_flags = (is_tail | (all_tiles >= n_used[0])).astype(jnp.int32)
        seg_src = seg_src.reshape(-1).astype(jnp.int32)
        seg_dst = seg_dst.reshape(-1).astype(jnp.int32)
        seg_len = (seg // SEG_ALIGN).reshape(-1).astype(jnp.int32)

        mine = tile_expert[:, None] == jnp.arange(N_EXPERTS, dtype=jnp.int32)[None, :]
        first_tile = jnp.sum(jnp.where(mine, (tile_end - tiles_per)[None, :], 0), axis=1)
        group_rows = jnp.sum(jnp.where(mine, group[None, :], 0), axis=1)
        tile_rows = jnp.where(all_tiles < n_used[0],
                              jnp.clip(group_rows - (all_tiles - first_tile) * tm, 0, tm), 0)

        xs = _sort_place(seg_src, seg_dst, seg_len, zero_flags, hb, ls, n_tiles * tm, tr, tm, n_bits)
        ys = _experts(tile_expert.astype(jnp.int32), tile_rows.astype(jnp.int32), n_used.astype(jnp.int32), xs,
                      w_exp1, b_exp1, w_exp2, b_exp2, l, tm)
        x2 = _combine(seg_src, seg_dst, seg_len, ys, x2, ls, wts, mod3, l, B, S, tr, n_bits)
    return x2.reshape(B, S, D)
```

```python
import functools
import math

import jax
import jax.numpy as jnp
from jax import lax
from jax.experimental import pallas as pl
from jax.experimental.pallas import tpu as pltpu

F32 = jnp.float32
BF16 = jnp.bfloat16

LANES = 128
NORM_EPS = 1e-6
HGRN_HEADS = 4
HGRN_DIM = 128
HGRN_CHUNK = 32
HGRN_ROWS = 8
MLA_HEADS = 8
MLA_NOPE = 64
MLA_ROPE = 32
MLA_QK = MLA_NOPE + MLA_ROPE
MLA_V = 64
MLA_Q_RANK = 384
MLA_KV_RANK = 256
ROPE_THETA = 10000.0
POOL_WINDOWS = (2, 4, 8, 16)
N_EXPERTS = 32
TOP_K = 4
SWIGLU_LIMIT = 7.0
SWIGLU_ALPHA = 1.702
NEG_BIG = -1e30
LOG2E = 1.4426950408889634
FACTOR_LIMIT = 100.0
SEG_ALIGN = 16
SORT_CHUNK = 256
EXPERT_ROW_STEPS = 4

VMEM_LIMIT = 56 * 1024 * 1024


def _cparams(sem):
    return pltpu.CompilerParams(dimension_semantics=sem, vmem_limit_bytes=VMEM_LIMIT)


def _dot(a, b):
    return jnp.dot(a, b, preferred_element_type=F32)


def _dot_nt(a, b):
    return lax.dot_general(a, b, (((1,), (1,)), ((), ())), preferred_element_type=F32)


def _dot_tn(a, b):
    return lax.dot_general(a, b, (((0,), (0,)), ((), ())), preferred_element_type=F32)


def _split3(x):
    hi = x.astype(BF16)
    r = x - hi.astype(F32)
    mid = r.astype(BF16)
    lo = (r - mid.astype(F32)).astype(BF16)
    return hi, mid, lo


def _sigmoid(x):
    return 1.0 / (1.0 + jnp.exp(-x))


def _ada_kernel(c_ref, w_ref, b_ref, o_ref):
    c = c_ref[...]
    cond = (c * _sigmoid(c)).astype(BF16)
    o_ref[0] = _dot(cond, w_ref[0].astype(BF16)) + b_ref[0]


def _ada(c, ada_w, ada_b):
    L, D, W = ada_w.shape
    B = c.shape[0]
    tn = D
    return pl.pallas_call(
        _ada_kernel,
        grid=(L, W // tn),
        in_specs=[
            pl.BlockSpec((B, D), lambda l, j: (0, 0)),
            pl.BlockSpec((1, D, tn), lambda l, j: (l, 0, j)),
            pl.BlockSpec((1, 1, tn), lambda l, j: (l, 0, j)),
        ],
        out_specs=pl.BlockSpec((1, B, tn), lambda l, j: (l, 0, j)),
        out_shape=jax.ShapeDtypeStruct((L, B, W), F32),
        compiler_params=_cparams(("parallel", "parallel")),
        name="ada_mod",
    )(c, ada_w, ada_b.reshape(L, 1, W))


def _inproj_kernel(x_ref, g_ref, sh_ref, sc_ref, wg_ref, wh_ref, wm_ref, wp_ref,
                   og_ref, oh_ref, om_ref, op_ref):
    x = x_ref[...]
    ms = jnp.mean(x * x, axis=-1, keepdims=True)
    h = x * lax.rsqrt(ms + NORM_EPS) * g_ref[...]
    h = h * (1.0 + sc_ref[0]) + sh_ref[0]
    hb = h.astype(BF16)
    cw = 512
    for w_ref, o_ref in ((wg_ref, og_ref), (wh_ref, oh_ref), (wm_ref, om_ref), (wp_ref, op_ref)):
        width = w_ref.shape[1]
        step = cw if width % cw == 0 else width
        for j in range(0, width, step):
            o_ref[:, j:j + step] = _dot(hb, w_ref[:, j:j + step]).astype(o_ref.dtype)


def _inproj(x2, g, mod3, l, wg, wh, wm, wp, B, S, tr):
    N, D = x2.shape
    nb = S // tr
    base = l * B * 6

    def modspec(j):
        return pl.BlockSpec((1, 1, D), lambda i: (base + (i // nb) * 6 + j, 0, 0))

    def wspec(w):
        return pl.BlockSpec(w.shape, lambda i: (0, 0), pipeline_mode=pl.Buffered(1))

    outs = (
        jax.ShapeDtypeStruct((N, wg.shape[1]), BF16),
        jax.ShapeDtypeStruct((N, wh.shape[1]), F32),
        jax.ShapeDtypeStruct((N, wm.shape[1]), F32),
        jax.ShapeDtypeStruct((N, wp.shape[1]), F32),
    )
    return pl.pallas_call(
        _inproj_kernel,
        grid=(N // tr,),
        in_specs=[
            pl.BlockSpec((tr, D), lambda i: (i, 0)),
            pl.BlockSpec((1, D), lambda i: (0, 0)),
            modspec(0), modspec(1),
            wspec(wg), wspec(wh), wspec(wm), wspec(wp),
        ],
        out_specs=tuple(pl.BlockSpec((tr, o.shape[1]), lambda i: (i, 0)) for o in outs),
        out_shape=outs,
        compiler_params=_cparams(("parallel",)),
        name="in_proj",
    )(x2, g, mod3, mod3, wg, wh, wm, wp)


def _hgrn_kernel(layer, n_chunks, q_ref, f_ref, i_ref, g_ref, lb_ref, on_ref, o_ref,
                 st_ref, stb_ref, oi_ref):
    C, DH, H = HGRN_CHUNK, HGRN_DIM, HGRN_HEADS

    @pl.when(pl.program_id(1) == 0)
    def _():
        st_ref[...] = jnp.zeros_like(st_ref)
        stb_ref[...] = jnp.zeros_like(stb_ref)

    lbr = lb_ref[...]
    e = jnp.exp(lbr - jnp.max(lbr, axis=0, keepdims=True))
    p = e / jnp.sum(e, axis=0, keepdims=True)
    lb = jnp.zeros((1, lbr.shape[1]), F32)
    for j in range(1, layer + 1):
        lb = lb + p[j:j + 1]
    log_lb = jnp.log(lb)
    log_1mlb = jnp.log1p(-lb)
    one_m_lb = 1.0 - lb
    onorm = jnp.tile(on_ref[...], (1, H))

    row_c = lax.broadcasted_iota(jnp.int32, (C, C), 0)
    col_c = lax.broadcasted_iota(jnp.int32, (C, C), 1)
    tri = (col_c <= row_c).astype(BF16)
    rs = lax.broadcasted_iota(jnp.int32, (H * C, H * C), 0)
    cs = lax.broadcasted_iota(jnp.int32, (H * C, H * C), 1)
    same_head_causal = jnp.logical_and(cs <= rs, cs >= rs - rs % C)
    row_1 = lax.broadcasted_iota(jnp.int32, (C, 1), 0)
    heads = [slice(h * DH, (h + 1) * DH) for h in range(H)]

    def stack(x):
        return jnp.concatenate([x[:, hs] for hs in heads], axis=0)

    def unstack(x):
        return jnp.concatenate([x[h * C:(h + 1) * C] for h in range(H)], axis=1)

    def prepare(r, sl):
        q = q_ref[r, sl, :]
        z = f_ref[r, sl, :]
        v = i_ref[r, sl, :]
        ez = jnp.exp(-jnp.abs(z))
        log_sig = jnp.minimum(z, 0.0) - jnp.log1p(ez)
        if layer == 0:
            lf = log_sig
            kk = jnp.where(z >= 0.0, ez, 1.0) / (1.0 + ez)
        else:
            a = log_1mlb + log_sig
            m = jnp.maximum(a, log_lb)
            lf = m + jnp.log(jnp.exp(a - m) + jnp.exp(log_lb - m))
            kk = one_m_lb * jnp.where(z >= 0.0, ez, 1.0) / (1.0 + ez)
        l1, l2, l3 = _split3(lf)
        b = (_dot(tri, l1) + _dot(tri, l2) + _dot(tri, l3)) * LOG2E
        qs = (q * jnp.exp2(b)).astype(BF16)
        o_inter = _dot_nt(qs, stb_ref[r])
        return q, v, kk, b, qs, v.astype(BF16), o_inter

    def intra_factored(r, prepared):
        _, _, kk, b, qs, vb, _ = prepared
        ks = (kk * jnp.exp2(-b)).astype(BF16)
        att = _dot_nt(stack(qs), stack(ks))
        att = jnp.where(same_head_causal, att, 0.0)
        oi_ref[r] = unstack(_dot(att.astype(BF16), stack(vb)))

    def intra_termwise(r, prepared):
        q, v, kk, b, _, _, _ = prepared
        for hs in heads:
            bh, qh, kh, vh = b[:, hs], q[:, hs], kk[:, hs], v[:, hs]
            acc = jnp.zeros((C, DH), F32)
            for s in range(C):
                es = jnp.exp2(jnp.minimum(bh - bh[s:s + 1], 0.0))
                col = jnp.sum(qh * es * kh[s:s + 1], axis=-1, keepdims=True)
                acc = acc + jnp.where(row_1 >= s, col, 0.0) * vh[s:s + 1]
            oi_ref[r, :, hs] = acc

    def finish(r, sl, prepared):
        _, _, kk, b, _, vb, o_inter = prepared
        bl = b[C - 1:C]
        o = oi_ref[r] + o_inter
        kd = (kk * jnp.exp2(bl - b)).astype(BF16)
        update = _dot_tn(vb, kd)
        decay = jnp.exp2(bl)
        normed = []
        for hs in heads:
            new = st_ref[r, hs, hs] * decay[:, hs] + update[hs, hs]
            st_ref[r, hs, hs] = new
            stb_ref[r, hs, hs] = new.astype(BF16)
            oh = o[:, hs]
            normed.append(oh * lax.rsqrt(jnp.mean(oh * oh, axis=-1, keepdims=True) + NORM_EPS))
        g = g_ref[r, sl, :]
        y = jnp.concatenate(normed, axis=1) * onorm * (g * _sigmoid(g))
        o_ref[r, sl, :] = y.astype(o_ref.dtype)

    def chunk(ci, carry):
        sl = pl.ds(pl.multiple_of(ci * C, C), C)
        rows = range(q_ref.shape[0])
        prepared = [prepare(r, sl) for r in rows]
        span = -prepared[0][3][C - 1:C]
        for p in prepared[1:]:
            span = jnp.maximum(span, -p[3][C - 1:C])
        factor_ok = jnp.max(span) < FACTOR_LIMIT

        @pl.when(factor_ok)
        def _():
            for r in rows:
                intra_factored(r, prepared[r])

        @pl.when(jnp.logical_not(factor_ok))
        def _():
            for r in rows:
                intra_termwise(r, prepared[r])

        for r in rows:
            finish(r, sl, prepared[r])
        return carry

    lax.fori_loop(0, n_chunks, chunk, 0)


def _hgrn(ph, hgrn_lb, onorm_g, layer, B, S, tc):
    L = hgrn_lb.shape[0]
    W = HGRN_HEADS * HGRN_DIM

    nb = math.gcd(B, HGRN_ROWS)

    def spec(off):
        return pl.BlockSpec((nb, tc, W), lambda b, t: (b, t, off))

    return pl.pallas_call(
        functools.partial(_hgrn_kernel, layer, tc // HGRN_CHUNK),
        grid=(B // nb, S // tc),
        in_specs=[spec(0), spec(1), spec(2), spec(3),
                  pl.BlockSpec((L, W), lambda b, t: (0, 0)),
                  pl.BlockSpec((1, HGRN_DIM), lambda b, t: (0, 0))],
        out_specs=pl.BlockSpec((nb, tc, W), lambda b, t: (b, t, 0)),
        out_shape=jax.ShapeDtypeStruct((B, S, W), BF16),
        scratch_shapes=[pltpu.VMEM((nb, W, W), F32), pltpu.VMEM((nb, W, W), BF16),
                        pltpu.VMEM((nb, HGRN_CHUNK, W), F32)],
        compiler_params=_cparams(("parallel", "arbitrary")),
        name="hgrn2",
    )(ph, ph, ph, ph, hgrn_lb, onorm_g)


def _rope_table_kernel(pos_ref, fr_ref, cos_ref, sin_ref):
    ang = pos_ref[...] * fr_ref[...]
    lane = lax.broadcasted_iota(jnp.int32, ang.shape, 1)
    sn = jnp.sin(ang)
    cos_ref[...] = jnp.cos(ang)
    sin_ref[...] = jnp.where(lane < MLA_NOPE + MLA_ROPE // 2, -sn, sn)


def _rope_tables(posf, freq_row, tr):
    N = posf.shape[0]
    out = jax.ShapeDtypeStruct((N, LANES), F32)
    return pl.pallas_call(
        _rope_table_kernel,
        grid=(N // tr,),
        in_specs=[pl.BlockSpec((tr, 1), lambda i: (i, 0)), pl.BlockSpec((1, LANES), lambda i: (0, 0))],
        out_specs=(pl.BlockSpec((tr, LANES), lambda i: (i, 0)), pl.BlockSpec((tr, LANES), lambda i: (i, 0))),
        out_shape=(out, out),
        compiler_params=_cparams(("parallel",)),
        name="rope_tables",
    )(posf, freq_row)


def _mla_prep_kernel(pm_ref, cos_ref, sin_ref, qg_ref, kvg_ref, wq_ref, wqr_ref, wkv_ref,
                     qn_ref, qnr_ref, kn_ref, knr_ref, q_ref, k_ref, v_ref):
    pm = pm_ref[...]
    o = 0
    ql = pm[:, o:o + MLA_Q_RANK]; o += MLA_Q_RANK
    kvl = pm[:, o:o + MLA_KV_RANK]; o += MLA_KV_RANK
    kr = pm[:, o:o + LANES]; o += LANES
    krr = pm[:, o:o + LANES]

    def rms(x, g):
        return x * lax.rsqrt(jnp.mean(x * x, axis=-1, keepdims=True) + NORM_EPS) * g

    qlb = rms(ql, qg_ref[...]).astype(BF16)
    qa = _dot(qlb, wq_ref[...])
    qr = _dot(qlb, wqr_ref[...])
    kva = _dot(rms(kvl, kvg_ref[...]).astype(BF16), wkv_ref[...])
    v_ref[...] = kva[:, MLA_HEADS * LANES:].astype(v_ref.dtype)

    cs = cos_ref[...]
    sn = sin_ref[...]
    scale = LOG2E / math.sqrt(MLA_QK)
    q_cos = qn_ref[...] * cs * scale
    q_sin = qnr_ref[...] * sn * scale
    k_cos = kn_ref[...] * cs
    k_sin = knr_ref[...] * sn
    k_rot = krr * k_sin

    def inv_rms(xh):
        return lax.rsqrt(jnp.sum(xh * xh, axis=-1, keepdims=True) * (1.0 / MLA_QK) + NORM_EPS)

    for h in range(MLA_HEADS):
        hs = slice(h * LANES, (h + 1) * LANES)
        qh = qa[:, hs]
        q_ref[0, h] = ((qh * q_cos + qr[:, hs] * q_sin) * inv_rms(qh)).astype(q_ref.dtype)
        kh = kva[:, hs] + kr
        k_ref[0, h] = ((kh * k_cos + k_rot) * inv_rms(kh)).astype(k_ref.dtype)


def _mla_prep(pm, cos_t, sin_t, qlat_g, kvlat_g, wq, wqr, wkv, qn, qnr, kn, knr, B, S, tr):
    N = pm.shape[0]
    nb = S // tr
    H = MLA_HEADS
    full = lambda a: pl.BlockSpec(a.shape, lambda i: (0,) * a.ndim)
    return pl.pallas_call(
        _mla_prep_kernel,
        grid=(N // tr,),
        in_specs=[pl.BlockSpec((tr, pm.shape[1]), lambda i: (i, 0)),
                  pl.BlockSpec((tr, LANES), lambda i: (i, 0)),
                  pl.BlockSpec((tr, LANES), lambda i: (i, 0)),
                  full(qlat_g), full(kvlat_g), full(wq), full(wqr), full(wkv),
                  full(qn), full(qnr), full(kn), full(knr)],
        out_specs=(pl.BlockSpec((1, H, tr, LANES), lambda i: (i // nb, 0, i % nb, 0)),
                   pl.BlockSpec((1, H, tr, LANES), lambda i: (i // nb, 0, i % nb, 0)),
                   pl.BlockSpec((tr, H * MLA_V), lambda i: (i, 0))),
        out_shape=(jax.ShapeDtypeStruct((B, H, S, LANES), BF16),
                   jax.ShapeDtypeStruct((B, H, S, LANES), BF16),
                   jax.ShapeDtypeStruct((N, H * MLA_V), BF16)),
        compiler_params=_cparams(("parallel",)),
        name="mla_prep",
    )(pm, cos_t, sin_t, qlat_g, kvlat_g, wq, wqr, wkv, qn, qnr, kn, knr)


def _attn_kernel(q_ref, k_ref, v_ref, o_ref, m_ref, l_ref, acc_ref):
    qi = pl.program_id(2)
    T = q_ref.shape[2]

    m_ref[...] = jnp.full_like(m_ref, NEG_BIG)
    l_ref[...] = jnp.zeros_like(l_ref)
    acc_ref[...] = jnp.zeros_like(acc_ref)

    def step(ki, masked):
        rows = pl.ds(pl.multiple_of(ki * T, T), T)
        vblk = v_ref[0, rows, :]
        lane = lax.broadcasted_iota(jnp.int32, vblk.shape, 1)
        if masked:
            row = lax.broadcasted_iota(jnp.int32, (T, T), 0)
            col = lax.broadcasted_iota(jnp.int32, (T, T), 1)
            keep = col <= row
        for h in range(2):
            s = _dot_nt(q_ref[0, h], k_ref[0, h, rows, :])
            if masked:
                s = jnp.where(keep, s, NEG_BIG)
            m_old = m_ref[h]
            m_new = jnp.maximum(m_old, jnp.max(s, axis=-1, keepdims=True))
            alpha = jnp.exp2(m_old - m_new)
            p = jnp.exp2(s - jnp.tile(m_new, (1, T // LANES)))
            l_ref[h] = alpha * l_ref[h] + jnp.sum(p, axis=-1, keepdims=True)
            vh = jnp.where((lane < MLA_V) if h == 0 else (lane >= MLA_V), vblk, jnp.zeros_like(vblk))
            acc_ref[h] = alpha * acc_ref[h] + _dot(p.astype(BF16), vh)
            m_ref[h] = m_new

    def below_diagonal(ki, c):
        step(ki, False)
        return c

    lax.fori_loop(0, qi, below_diagonal, 0)
    step(qi, True)
    o_ref[0] = (acc_ref[0] / l_ref[0] + acc_ref[1] / l_ref[1]).astype(o_ref.dtype)


def _attention(q, k, v3, B, S, T):
    H = MLA_HEADS
    n = S // T
    return pl.pallas_call(
        _attn_kernel,
        grid=(B, H // 2, n),
        in_specs=[pl.BlockSpec((1, 2, T, LANES), lambda b, hp, qi: (b, hp, qi, 0)),
                  pl.BlockSpec((1, 2, S, LANES), lambda b, hp, qi: (b, hp, 0, 0)),
                  pl.BlockSpec((1, S, LANES), lambda b, hp, qi: (b, 0, hp))],
        out_specs=pl.BlockSpec((1, T, LANES), lambda b, hp, qi: (b, qi, hp)),
        out_shape=jax.ShapeDtypeStruct((B, S, H * MLA_V), BF16),
        scratch_shapes=[pltpu.VMEM((2, T, LANES), F32), pltpu.VMEM((2, T, LANES), F32),
                        pltpu.VMEM((2, T, LANES), F32)],
        compiler_params=_cparams(("parallel", "parallel", "arbitrary")),
        name="mla_attention",
    )(q, k, v3)


def _pool_kernel(u_ref, w_ref, sc_ref, o_ref):
    S = u_ref.shape[1]
    row = lax.broadcasted_iota(jnp.int32, (S, LANES), 0)
    t1 = (row + 1).astype(F32)
    for gi, win in enumerate(POOL_WINDOWS):
        u = u_ref[0, :, gi * LANES:(gi + 1) * LANES]
        s = u
        k = 1
        while k < win:
            s = s + jnp.where(row >= k, pltpu.roll(s, k, 0), 0.0)
            k *= 2
        pooled = s / jnp.minimum(t1, float(win))
        mixed = _dot((pooled - u).astype(BF16), w_ref[gi].astype(BF16))
        o_ref[0, :, gi * LANES:(gi + 1) * LANES] = (
            mixed * sc_ref[:, gi * LANES:(gi + 1) * LANES]).astype(o_ref.dtype)


def _pool(pp3, w_pool, scale):
    B, S, W = pp3.shape
    return pl.pallas_call(
        _pool_kernel,
        grid=(B,),
        in_specs=[pl.BlockSpec((1, S, W), lambda b: (b, 0, 0)),
                  pl.BlockSpec(w_pool.shape, lambda b: (0, 0, 0)),
                  pl.BlockSpec((1, W), lambda b: (0, 0))],
        out_specs=pl.BlockSpec((1, S, W), lambda b: (b, 0, 0)),
        out_shape=jax.ShapeDtypeStruct((B, S, W), BF16),
        compiler_params=_cparams(("parallel",)),
        name="pool_mixer",
    )(pp3, w_pool, scale)


def _merge_router_kernel(x_ref, gt_ref, ya_ref, yb_ref, yc_ref, wa_ref, wb_ref, wc_ref, wo_ref, g1_ref,
                         g_ref, sh_ref, sc_ref, wr_ref, br_ref, xo_ref, hb_ref, ls_ref, wt_ref, cnt_ref):
    D = x_ref.shape[1]
    merged = None
    for j, (y_ref, w_ref) in enumerate(((ya_ref, wa_ref), (yb_ref, wb_ref), (yc_ref, wc_ref))):
        y = _dot(y_ref[...], w_ref[...])
        gate = _sigmoid(gt_ref[:, j * D:(j + 1) * D].astype(F32))
        merged = gate * y if merged is None else merged + gate * y
    x = x_ref[...] + g1_ref[0] * _dot(merged.astype(BF16), wo_ref[...])
    xo_ref[...] = x
    _route(x, g_ref, sh_ref, sc_ref, wr_ref, br_ref, hb_ref, ls_ref, wt_ref, cnt_ref)


def _route(x, g_ref, sh_ref, sc_ref, wr_ref, br_ref, hb_ref, ls_ref, wt_ref, cnt_ref):
    tr = x.shape[0]

    ms = jnp.mean(x * x, axis=-1, keepdims=True)
    h = x * lax.rsqrt(ms + NORM_EPS) * g_ref[...]
    h = h * (1.0 + sc_ref[0]) + sh_ref[0]
    hb_ref[...] = h.astype(hb_ref.dtype)
    h1, h2, _ = _split3(h)
    w1, w2, _ = _split3(wr_ref[...])
    logits = _dot(h1, w1) + (_dot(h1, w2) + _dot(h2, w1)) + br_ref[...]
    lane = lax.broadcasted_iota(jnp.int32, logits.shape, 1)
    work = logits
    vals, ids = [], []
    for _k in range(TOP_K):
        m = jnp.max(work, axis=-1, keepdims=True)
        sel = jnp.min(jnp.where(work == m, lane, LANES), axis=-1, keepdims=True)
        vals.append(m)
        ids.append(sel)
        work = jnp.where(lane == sel, -jnp.inf, work)
    es = [jnp.exp(v - vals[0]) for v in vals]
    den = es[0] + es[1] + es[2] + es[3]
    onehot = jnp.zeros(logits.shape, F32)
    wt_out = jnp.zeros(logits.shape, F32)
    for k in range(TOP_K):
        onehot = onehot + (lane == ids[k]).astype(F32)
        wt_out = jnp.where(lane == k, es[k] / den, wt_out)
    wt_ref[...] = wt_out

    r = lax.broadcasted_iota(jnp.int32, (tr, tr), 0)
    c = lax.broadcasted_iota(jnp.int32, (tr, tr), 1)
    strict = (c < r).astype(BF16)
    before = _dot(strict, onehot.astype(BF16))
    cnt = jnp.sum(onehot, axis=0, keepdims=True)
    cnt_ref[0] = jnp.broadcast_to(cnt, cnt_ref.shape[1:])
    units = jnp.floor((cnt + (SEG_ALIGN - 1.0)) * (1.0 / SEG_ALIGN))
    er = lax.broadcasted_iota(jnp.int32, (LANES, LANES), 0)
    ec = lax.broadcasted_iota(jnp.int32, (LANES, LANES), 1)
    upper = (er < ec).astype(BF16)
    seg_start = _dot(jnp.broadcast_to(units, (8, LANES)).astype(BF16), upper)[0:1] * float(SEG_ALIGN)
    pos = before + seg_start
    ls_out = jnp.full(logits.shape, -1.0, F32)
    for k in range(TOP_K):
        lk = jnp.sum(jnp.where(lane == ids[k], pos, 0.0), axis=-1, keepdims=True)
        ls_out = jnp.where(lane == k, lk, ls_out)
    ls_ref[...] = ls_out.astype(jnp.int32)


def _local_rows(tb):
    return TOP_K * tb + N_EXPERTS * SEG_ALIGN


def _sort_chunk(rl):
    return SORT_CHUNK if rl % SORT_CHUNK == 0 else SORT_CHUNK // 2


def _merge_router(x2, gates, ya, yb, yc, wa, wb, wc, wo, g, mod3, l, wr_p, br_p, B, S, tr):
    N, D = x2.shape
    nb = S // tr
    base = l * B * 6
    full = lambda a: pl.BlockSpec(a.shape, lambda i: (0,) * a.ndim)
    rows = lambda w: pl.BlockSpec((tr, w), lambda i: (i, 0))

    def modspec(j):
        return pl.BlockSpec((1, 1, D), lambda i: (base + (i // nb) * 6 + j, 0, 0))

    return pl.pallas_call(
        _merge_router_kernel,
        grid=(N // tr,),
        in_specs=[rows(D), rows(gates.shape[1]), rows(ya.shape[1]), rows(yb.shape[1]), rows(yc.shape[1]),
                  full(wa), full(wb), full(wc), full(wo), modspec(2),
                  full(g), modspec(3), modspec(4), full(wr_p), full(br_p)],
        out_specs=(rows(D), rows(D), rows(LANES), rows(LANES),
                   pl.BlockSpec((1, 8, LANES), lambda i: (i, 0, 0))),
        out_shape=(jax.ShapeDtypeStruct((N, D), F32),
                   jax.ShapeDtypeStruct((N, D), BF16),
                   jax.ShapeDtypeStruct((N, LANES), jnp.int32),
                   jax.ShapeDtypeStruct((N, LANES), F32),
                   jax.ShapeDtypeStruct((N // tr, 8, LANES), F32)),
        compiler_params=_cparams(("parallel",)),
        name="merge_router",
    )(x2, gates, ya, yb, yc, wa, wb, wc, wo, mod3, g, mod3, mod3, wr_p, br_p)


def _segment_copies(j, e, src_ref, dst_ref, len_ref, local_ref, global_ref, sem, to_global, n_bits):
    base = j * N_EXPERTS + e
    units = len_ref[base]
    s0 = src_ref[base]
    d0 = dst_ref[base]
    copies = []
    for bit in range(n_bits):
        rows = SEG_ALIGN << bit
        done = ((units >> (bit + 1)) << (bit + 1)) * SEG_ALIGN
        loc = local_ref.at[pl.ds(pl.multiple_of(s0 + done, SEG_ALIGN), rows)]
        glo = global_ref.at[pl.ds(pl.multiple_of(d0 + done, SEG_ALIGN), rows)]
        cp = pltpu.make_async_copy(loc, glo, sem) if to_global else pltpu.make_async_copy(glo, loc, sem)
        copies.append((((units >> bit) & 1) == 1, cp))
    return copies


def _each_segment_copy(j, src_ref, dst_ref, len_ref, local_ref, global_ref, sem, to_global, n_bits, action):
    def per_expert(e, c):
        for pred, cp in _segment_copies(j, e, src_ref, dst_ref, len_ref, local_ref, global_ref,
                                        sem, to_global, n_bits):
            @pl.when(pred)
            def _():
                action(cp)
        return c
    lax.fori_loop(0, N_EXPERTS, per_expert, 0)


def _sort_place_kernel(n_bits, src_ref, dst_ref, len_ref, zflag_ref, hb_ref, ls_ref, xs_hbm,
                       zbuf, xl_buf, zsem, sem):
    j = pl.program_id(0)
    last = pl.num_programs(0) - 1
    tr = hb_ref.shape[0]
    tm = zbuf.shape[0]
    rl = xl_buf.shape[1]
    n_tiles = xs_hbm.shape[0] // tm
    slot = j % 2

    def segments(block, buf_slot, action):
        _each_segment_copy(block, src_ref, dst_ref, len_ref, xl_buf.at[buf_slot], xs_hbm,
                           sem.at[buf_slot], True, n_bits, action)

    @pl.when(j == 0)
    def _():
        zbuf[...] = jnp.zeros_like(zbuf)

        def zero_copy(t):
            return pltpu.make_async_copy(zbuf, xs_hbm.at[pl.ds(pl.multiple_of(t * tm, tm), tm)], zsem)

        def zstart(t, c):
            @pl.when(zflag_ref[t] > 0)
            def _():
                zero_copy(t).start()
            return c

        def zwait(t, c):
            @pl.when(zflag_ref[t] > 0)
            def _():
                zero_copy(t).wait()
            return c

        lax.fori_loop(0, n_tiles, zstart, 0)
        lax.fori_loop(0, n_tiles, zwait, 0)

    @pl.when(j >= 2)
    def _():
        segments(j - 2, slot, lambda cp: cp.wait())

    ls_t = jnp.transpose(ls_ref[...].astype(F32))
    targets = [ls_t[k:k + 1, :].astype(jnp.int32) for k in range(TOP_K)]
    hb = hb_ref[...]
    ch = _sort_chunk(rl)
    for cidx in range(rl // ch):
        ri = lax.broadcasted_iota(jnp.int32, (ch, tr), 0) + cidx * ch
        hit = ri == targets[0]
        for k in range(1, TOP_K):
            hit = jnp.logical_or(hit, ri == targets[k])
        perm = jnp.where(hit, 1.0, 0.0).astype(BF16)
        xl_buf[slot, cidx * ch:(cidx + 1) * ch, :] = _dot(perm, hb).astype(xl_buf.dtype)

    segments(j, slot, lambda cp: cp.start())

    @pl.when(j == last)
    def _():
        @pl.when(j >= 1)
        def _():
            segments(j - 1, 1 - slot, lambda cp: cp.wait())
        segments(j, slot, lambda cp: cp.wait())


def _sort_place(seg_src, seg_dst, seg_len, zero_flags, hb, ls, n_rows, tr, tm, n_bits):
    N, D = hb.shape
    rl = _local_rows(tr)
    grid_spec = pltpu.PrefetchScalarGridSpec(
        num_scalar_prefetch=4,
        grid=(N // tr,),
        in_specs=[pl.BlockSpec((tr, D), lambda j, a, b, c, d: (j, 0)),
                  pl.BlockSpec((tr, LANES), lambda j, a, b, c, d: (j, 0))],
        out_specs=pl.BlockSpec(memory_space=pl.ANY),
        scratch_shapes=[pltpu.VMEM((tm, D), hb.dtype), pltpu.VMEM((2, rl, D), hb.dtype),
                        pltpu.SemaphoreType.DMA(()), pltpu.SemaphoreType.DMA((2,))],
    )
    return pl.pallas_call(
        functools.partial(_sort_place_kernel, n_bits),
        grid_spec=grid_spec,
        out_shape=jax.ShapeDtypeStruct((n_rows, D), hb.dtype),
        compiler_params=_cparams(("arbitrary",)),
        name="moe_sort_place",
    )(seg_src, seg_dst, seg_len, zero_flags, hb, ls)


def _expert_kernel(first_expert, te_ref, nxt_ref, rows_ref, nu_ref, x_ref, w1_hbm, b1_ref, w2_hbm, b2_ref,
                   o_ref, w1f, w2f, w1b, w2b, sem):
    t = pl.program_id(0)
    tm = x_ref.shape[0]
    F = w2b.shape[0]
    rows = rows_ref[t]

    def weight_copies(e):
        return (pltpu.make_async_copy(w1_hbm.at[first_expert + e], w1f, sem.at[0]),
                pltpu.make_async_copy(w2_hbm.at[first_expert + e], w2f, sem.at[1]))

    @pl.when(t == 0)
    def _():
        for cp in weight_copies(te_ref[0]):
            cp.start()

    @pl.when(jnp.logical_and(rows > 0, jnp.logical_or(t == 0, te_ref[t] != te_ref[jnp.maximum(t - 1, 0)])))
    def _():
        for cp in weight_copies(te_ref[t]):
            cp.wait()
        w1b[...] = w1f[...].astype(BF16)
        w2b[...] = w2f[...].astype(BF16)

        @pl.when(nxt_ref[t] >= 0)
        def _():
            for cp in weight_copies(nxt_ref[t]):
                cp.start()

    def mlp(m):
        a = _dot(x_ref[:m, :], w1b[...]) + b1_ref[0]
        glu = jnp.minimum(a[:, :F], SWIGLU_LIMIT)
        lin = jnp.clip(a[:, F:], -SWIGLU_LIMIT, SWIGLU_LIMIT)
        act = glu * _sigmoid(SWIGLU_ALPHA * glu) * (lin + 1.0)
        o_ref[:m, :] = (_dot(act.astype(BF16), w2b[...]) + b2_ref[0]).astype(o_ref.dtype)
        if m < tm:
            o_ref[m:, :] = jnp.zeros((tm - m, o_ref.shape[1]), o_ref.dtype)

    quarter = tm // EXPERT_ROW_STEPS
    for q in range(1, EXPERT_ROW_STEPS + 1):
        @pl.when(jnp.logical_and(rows > (q - 1) * quarter, rows <= q * quarter))
        def _():
            mlp(q * quarter)

    @pl.when(rows == 0)
    def _():
        o_ref[...] = jnp.zeros_like(o_ref)


def _experts(tile_expert, next_expert, tile_rows, n_used, xs, w1_all, b1_all, w2_all, b2_all, l, tm):
    n_tiles = tile_expert.shape[0]
    D = xs.shape[1]
    L, E, _, F2 = w1_all.shape
    F = w2_all.shape[2]
    w1 = w1_all.reshape(L * E, D, F2)
    w2 = w2_all.reshape(L * E, F, D)
    b1 = b1_all.reshape(L * E, F2)
    b2 = b2_all.reshape(L * E, D)
    E0 = l * E
    E = L * E
    grid_spec = pltpu.PrefetchScalarGridSpec(
        num_scalar_prefetch=4,
        grid=(n_tiles,),
        in_specs=[pl.BlockSpec((tm, D), lambda t, te, nx, tr, nu: (jnp.minimum(t, nu[0] - 1), 0)),
                  pl.BlockSpec(memory_space=pl.ANY),
                  pl.BlockSpec((1, 1, F2), lambda t, te, nx, tr, nu: (E0 + te[t], 0, 0)),
                  pl.BlockSpec(memory_space=pl.ANY),
                  pl.BlockSpec((1, 1, D), lambda t, te, nx, tr, nu: (E0 + te[t], 0, 0))],
        out_specs=pl.BlockSpec((tm, D), lambda t, te, nx, tr, nu: (t, 0)),
        scratch_shapes=[pltpu.VMEM((D, F2), F32), pltpu.VMEM((F, D), F32),
                        pltpu.VMEM((D, F2), BF16), pltpu.VMEM((F, D), BF16),
                        pltpu.SemaphoreType.DMA((2,))],
    )
    return pl.pallas_call(
        functools.partial(_expert_kernel, E0),
        grid_spec=grid_spec,
        out_shape=jax.ShapeDtypeStruct((n_tiles * tm, D), BF16),
        compiler_params=_cparams(("arbitrary",)),
        name="expert_mlp",
    )(tile_expert, next_expert, tile_rows, n_used, xs, w1, b1.reshape(E, 1, F2), w2, b2.reshape(E, 1, D))


def _combine_kernel(n_bits, src_ref, dst_ref, len_ref, ys_hbm, x_ref, ls_ref, wt_ref, g2_ref, o_ref,
                    ybuf, sem):
    j = pl.program_id(0)
    tn = x_ref.shape[0]
    rl = ybuf.shape[1]
    slot = j % 2

    def fetch(block, buf_slot, action):
        _each_segment_copy(block, src_ref, dst_ref, len_ref, ybuf.at[buf_slot], ys_hbm,
                           sem.at[buf_slot], False, n_bits, action)

    def start_fetch(block, buf_slot):
        ybuf[buf_slot] = jnp.zeros((rl, ybuf.shape[2]), ybuf.dtype)
        fetch(block, buf_slot, lambda cp: cp.start())

    @pl.when(j == 0)
    def _():
        start_fetch(j, slot)

    @pl.when(j + 1 < pl.num_programs(0))
    def _():
        start_fetch(j + 1, 1 - slot)

    fetch(j, slot, lambda cp: cp.wait())
    ls = ls_ref[...]
    wt = wt_ref[...]
    ch = _sort_chunk(rl)
    y = jnp.zeros(x_ref.shape, F32)
    for cidx in range(rl // ch):
        li = lax.broadcasted_iota(jnp.int32, (tn, ch), 1) + cidx * ch
        pw = jnp.zeros((tn, ch), F32)
        for k in range(TOP_K):
            pw = jnp.where(li == ls[:, k:k + 1], wt[:, k:k + 1], pw)
        y = y + _dot(pw.astype(BF16), ybuf[slot, cidx * ch:(cidx + 1) * ch, :])
    o_ref[...] = x_ref[...] + g2_ref[0] * y


def _combine(seg_src, seg_dst, seg_len, ys, x2, ls, wts, mod3, l, B, S, tn, n_bits):
    N, D = x2.shape
    nb = S // tn
    base = l * B * 6
    rl = _local_rows(tn)
    grid_spec = pltpu.PrefetchScalarGridSpec(
        num_scalar_prefetch=3,
        grid=(N // tn,),
        in_specs=[pl.BlockSpec(memory_space=pl.ANY),
                  pl.BlockSpec((tn, D), lambda i, a, b, c: (i, 0)),
                  pl.BlockSpec((tn, LANES), lambda i, a, b, c: (i, 0)),
                  pl.BlockSpec((tn, LANES), lambda i, a, b, c: (i, 0)),
                  pl.BlockSpec((1, 1, D), lambda i, a, b, c: (base + (i // nb) * 6 + 5, 0, 0))],
        out_specs=pl.BlockSpec((tn, D), lambda i, a, b, c: (i, 0)),
        scratch_shapes=[pltpu.VMEM((2, rl, D), ys.dtype), pltpu.SemaphoreType.DMA((2,))],
    )
    return pl.pallas_call(
        functools.partial(_combine_kernel, n_bits),
        grid_spec=grid_spec,
        out_shape=jax.ShapeDtypeStruct((N, D), F32),
        compiler_params=_cparams(("arbitrary",)),
        name="moe_combine",
    )(seg_src, seg_dst, seg_len, ys, x2, ls, wts, mod3)


def _tile_rows(S, want):
    t = min(S, want)
    while S % t:
        t //= 2
    return t


def _prep_in_weights(w_in_l):
    HW = HGRN_HEADS * HGRN_DIM
    o = 0
    wh = w_in_l[:, o:o + 4 * HW]; o += 4 * HW
    wql = w_in_l[:, o:o + MLA_Q_RANK]; o += MLA_Q_RANK
    wkvl = w_in_l[:, o:o + MLA_KV_RANK]; o += MLA_KV_RANK
    wkr = w_in_l[:, o:o + MLA_ROPE]; o += MLA_ROPE
    wp = w_in_l[:, o:o + HW]; o += HW
    wg = w_in_l[:, o:]
    D = w_in_l.shape[0]
    wkr_p = jnp.zeros((D, LANES), w_in_l.dtype).at[:, MLA_NOPE:MLA_NOPE + MLA_ROPE].set(wkr)
    wm = jnp.concatenate([wql, wkvl, wkr_p, _rope_partner(wkr_p)], axis=1)
    return wg.astype(BF16), wh.astype(BF16), wm.astype(BF16), wp.astype(BF16)


def _rope_partner(w):
    half = MLA_ROPE // 2
    w3 = w.reshape(w.shape[:-1] + (-1, LANES))
    first = w3[..., MLA_NOPE:MLA_NOPE + half]
    second = w3[..., MLA_NOPE + half:MLA_NOPE + MLA_ROPE]
    out = jnp.concatenate([jnp.zeros_like(w3[..., :MLA_NOPE]), second, first,
                           jnp.zeros_like(w3[..., MLA_NOPE + MLA_ROPE:])], axis=-1)
    return out.reshape(w.shape)


def _pad_heads(w, n_heads, width):
    K = w.shape[0]
    w3 = w.reshape(K, n_heads, width)
    return jnp.pad(w3, ((0, 0), (0, 0), (0, LANES - width))).reshape(K, n_heads * LANES)


def kernel(x, c, positions, ada_w, ada_b, norm1_g, norm2_g, w_in, hgrn_lb, hgrn_onorm_g, mla_qlat_g, mla_kvlat_g, w_uq, w_ukv, q_norm_g, k_norm_g, w_pool, pool_scale, w_br_a, w_br_b, w_br_c, w_out, w_router, b_router, w_exp1, b_exp1, w_exp2, b_exp2):
    B, S, D = x.shape
    L = ada_w.shape[0]
    N = B * S
    H = MLA_HEADS
    tr = _tile_rows(S, 512)
    tc = _tile_rows(S, 128)
    ta = _tile_rows(S, 512)
    tm = 512

    mod = _ada(c, ada_w, ada_b)
    mod3 = mod.reshape(L * B * 6, 1, D)

    inv_freq = 1.0 / (ROPE_THETA ** (jnp.arange(0, MLA_ROPE, 2, dtype=F32) / MLA_ROPE))
    freq_row = jnp.zeros((1, LANES), F32).at[0, MLA_NOPE:MLA_NOPE + MLA_ROPE].set(
        jnp.concatenate([inv_freq, inv_freq]))
    posf = positions.astype(F32).reshape(N, 1)
    cos_t, sin_t = _rope_tables(posf, freq_row, tr)

    max_rows = N * TOP_K + (N // tr) * N_EXPERTS * (SEG_ALIGN - 1)
    n_tiles = -(-max_rows // tm) + N_EXPERTS
    n_bits = (tr // SEG_ALIGN).bit_length()
    x2 = x.reshape(N, D)
    for l in range(L):
        wg, wh, wm, wp = _prep_in_weights(w_in[l])
        pg, ph, pm, pp = _inproj(x2, norm1_g[l].reshape(1, D), mod3, l, wg, wh, wm, wp, B, S, tr)

        ya = _hgrn(ph.reshape(B, S, -1), hgrn_lb, hgrn_onorm_g[l].reshape(1, HGRN_DIM), l, B, S, tc)

        wq_f = _pad_heads(w_uq[l], H, MLA_QK)
        wq_p = wq_f.astype(BF16)
        wqr_p = _rope_partner(wq_f).astype(BF16)
        wkv3 = w_ukv[l].reshape(MLA_KV_RANK, H, MLA_NOPE + MLA_V)
        wkv_p = jnp.concatenate(
            [_pad_heads(wkv3[:, :, :MLA_NOPE].reshape(MLA_KV_RANK, H * MLA_NOPE), H, MLA_NOPE),
             wkv3[:, :, MLA_NOPE:].reshape(MLA_KV_RANK, H * MLA_V)], axis=1).astype(BF16)
        qn_p = jnp.pad(q_norm_g[l], (0, LANES - MLA_QK)).reshape(1, LANES)
        kn_p = jnp.pad(k_norm_g[l], (0, LANES - MLA_QK)).reshape(1, LANES)
        q, k, v = _mla_prep(pm, cos_t, sin_t, mla_qlat_g[l].reshape(1, -1), mla_kvlat_g[l].reshape(1, -1),
                            wq_p, wqr_p, wkv_p, qn_p, _rope_partner(qn_p), kn_p, _rope_partner(kn_p),
                            B, S, tr)
        yb = _attention(q, k, v.reshape(B, S, H * MLA_V), B, S, ta)

        yc = _pool(pp.reshape(B, S, -1), w_pool[l], pool_scale[l].reshape(1, -1))

        wr_p = jnp.pad(w_router[l], ((0, 0), (0, LANES - N_EXPERTS)))
        br_p = jnp.concatenate([b_router[l], jnp.full((LANES - N_EXPERTS,), NEG_BIG, F32)]).reshape(1, LANES)
        x2, hb, ls, wts, cnt = _merge_router(
            x2, pg, ya.reshape(N, -1), yb.reshape(N, -1), yc.reshape(N, -1),
            w_br_a[l].astype(BF16), w_br_b[l].astype(BF16), w_br_c[l].astype(BF16), w_out[l].astype(BF16),
            norm2_g[l].reshape(1, D), mod3, l, wr_p, br_p, B, S, tr)

        seg = ((cnt[:, 0, :N_EXPERTS].astype(jnp.int32) + SEG_ALIGN - 1) // SEG_ALIGN) * SEG_ALIGN
        seg_src = jnp.cumsum(seg, axis=1) - seg
        group = jnp.sum(seg, axis=0)
        tiles_per = (group + tm - 1) // tm
        tile_end = jnp.cumsum(tiles_per)
        offsets = (tile_end - tiles_per) * tm
        seg_dst = offsets[None, :] + jnp.cumsum(seg, axis=0) - seg
        n_used = tile_end[-1:]
        tile_ids = jnp.minimum(jnp.arange(n_tiles, dtype=jnp.int32), n_used[0] - 1)
        tile_expert = jnp.minimum(
            jnp.sum((tile_end[None, :] <= tile_ids[:, None]).astype(jnp.int32), axis=1), N_EXPERTS - 1)
        all_tiles = jnp.arange(n_tiles, dtype=jnp.int32)
        is_tail = jnp.any((tile_end[None, :] - 1 == all_tiles[:, None]) & (tiles_per[None, :] > 0), axis=1)
        zero_flags = (is_tail | (all_tiles >= n_used[0])).astype(jnp.int32)
        seg_src = seg_src.reshape(-1).astype(jnp.int32)
        seg_dst = seg_dst.reshape(-1).astype(jnp.int32)
        seg_len = (seg // SEG_ALIGN).reshape(-1).astype(jnp.int32)

        mine = tile_expert[:, None] == jnp.arange(N_EXPERTS, dtype=jnp.int32)[None, :]
        first_tile = jnp.sum(jnp.where(mine, (tile_end - tiles_per)[None, :], 0), axis=1)
        group_rows = jnp.sum(jnp.where(mine, group[None, :], 0), axis=1)
        tile_rows = jnp.where(all_tiles < n_used[0],
                              jnp.clip(group_rows - (all_tiles - first_tile) * tm, 0, tm), 0)

        xs = _sort_place(seg_src, seg_dst, seg_len, zero_flags, hb, ls, n_tiles * tm, tr, tm, n_bits)
        group_end = jnp.sum(jnp.where(mine, tile_end[None, :], 0), axis=1)
        following = jnp.minimum(
            jnp.sum((tile_end[None, :] <= group_end[:, None]).astype(jnp.int32), axis=1), N_EXPERTS - 1)
        next_expert = jnp.where(group_end < n_used[0], following, -1)

        ys = _experts(tile_expert.astype(jnp.int32), next_expert.astype(jnp.int32), tile_rows.astype(jnp.int32),
                      n_used.astype(jnp.int32), xs, w_exp1, b_exp1, w_exp2, b_exp2, l, tm)
        x2 = _combine(seg_src, seg_dst, seg_len, ys, x2, ls, wts, mod3, l, B, S, tr, n_bits)
    return x2.reshape(B, S, D)
```

```python
import functools
import math

import jax
import jax.numpy as jnp
from jax import lax
from jax.experimental import pallas as pl
from jax.experimental.pallas import tpu as pltpu

F32 = jnp.float32
BF16 = jnp.bfloat16

LANES = 128
NORM_EPS = 1e-6
HGRN_HEADS = 4
HGRN_DIM = 128
HGRN_CHUNK = 32
HGRN_ROWS = 8
MLA_HEADS = 8
MLA_NOPE = 64
MLA_ROPE = 32
MLA_QK = MLA_NOPE + MLA_ROPE
MLA_V = 64
MLA_Q_RANK = 384
MLA_KV_RANK = 256
ROPE_THETA = 10000.0
POOL_WINDOWS = (2, 4, 8, 16)
N_EXPERTS = 32
TOP_K = 4
SWIGLU_LIMIT = 7.0
SWIGLU_ALPHA = 1.702
NEG_BIG = -1e30
LOG2E = 1.4426950408889634
FACTOR_LIMIT = 100.0
SEG_ALIGN = 16
SORT_CHUNK = 256
EXPERT_ROW_STEPS = 4
ATTN_HEADS = 4

VMEM_LIMIT = 56 * 1024 * 1024


def _cparams(sem):
    return pltpu.CompilerParams(dimension_semantics=sem, vmem_limit_bytes=VMEM_LIMIT)


def _dot(a, b):
    return jnp.dot(a, b, preferred_element_type=F32)


def _dot_nt(a, b):
    return lax.dot_general(a, b, (((1,), (1,)), ((), ())), preferred_element_type=F32)


def _dot_tn(a, b):
    return lax.dot_general(a, b, (((0,), (0,)), ((), ())), preferred_element_type=F32)


def _split3(x):
    hi = x.astype(BF16)
    r = x - hi.astype(F32)
    mid = r.astype(BF16)
    lo = (r - mid.astype(F32)).astype(BF16)
    return hi, mid, lo


def _sigmoid(x):
    return 1.0 / (1.0 + jnp.exp(-x))


def _ada_kernel(c_ref, w_ref, b_ref, o_ref):
    c = c_ref[...]
    cond = (c * _sigmoid(c)).astype(BF16)
    o_ref[0] = _dot(cond, w_ref[0].astype(BF16)) + b_ref[0]


def _ada(c, ada_w, ada_b):
    L, D, W = ada_w.shape
    B = c.shape[0]
    tn = D
    return pl.pallas_call(
        _ada_kernel,
        grid=(L, W // tn),
        in_specs=[
            pl.BlockSpec((B, D), lambda l, j: (0, 0)),
            pl.BlockSpec((1, D, tn), lambda l, j: (l, 0, j)),
            pl.BlockSpec((1, 1, tn), lambda l, j: (l, 0, j)),
        ],
        out_specs=pl.BlockSpec((1, B, tn), lambda l, j: (l, 0, j)),
        out_shape=jax.ShapeDtypeStruct((L, B, W), F32),
        compiler_params=_cparams(("parallel", "parallel")),
        name="ada_mod",
    )(c, ada_w, ada_b.reshape(L, 1, W))


def _inproj_kernel(x_ref, g_ref, sh_ref, sc_ref, wg_ref, wh_ref, wm_ref, wp_ref,
                   og_ref, oh_ref, om_ref, op_ref):
    x = x_ref[...]
    ms = jnp.mean(x * x, axis=-1, keepdims=True)
    h = x * lax.rsqrt(ms + NORM_EPS) * g_ref[...]
    h = h * (1.0 + sc_ref[0]) + sh_ref[0]
    hb = h.astype(BF16)
    cw = 512
    for w_ref, o_ref in ((wg_ref, og_ref), (wh_ref, oh_ref), (wm_ref, om_ref), (wp_ref, op_ref)):
        width = w_ref.shape[1]
        step = cw if width % cw == 0 else width
        for j in range(0, width, step):
            o_ref[:, j:j + step] = _dot(hb, w_ref[:, j:j + step]).astype(o_ref.dtype)


def _inproj(x2, g, mod3, l, wg, wh, wm, wp, B, S, tr):
    N, D = x2.shape
    nb = S // tr
    base = l * B * 6

    def modspec(j):
        return pl.BlockSpec((1, 1, D), lambda i: (base + (i // nb) * 6 + j, 0, 0))

    def wspec(w):
        return pl.BlockSpec(w.shape, lambda i: (0, 0), pipeline_mode=pl.Buffered(1))

    outs = (
        jax.ShapeDtypeStruct((N, wg.shape[1]), BF16),
        jax.ShapeDtypeStruct((N, wh.shape[1]), F32),
        jax.ShapeDtypeStruct((N, wm.shape[1]), F32),
        jax.ShapeDtypeStruct((N, wp.shape[1]), F32),
    )
    return pl.pallas_call(
        _inproj_kernel,
        grid=(N // tr,),
        in_specs=[
            pl.BlockSpec((tr, D), lambda i: (i, 0)),
            pl.BlockSpec((1, D), lambda i: (0, 0)),
            modspec(0), modspec(1),
            wspec(wg), wspec(wh), wspec(wm), wspec(wp),
        ],
        out_specs=tuple(pl.BlockSpec((tr, o.shape[1]), lambda i: (i, 0)) for o in outs),
        out_shape=outs,
        compiler_params=_cparams(("parallel",)),
        name="in_proj",
    )(x2, g, mod3, mod3, wg, wh, wm, wp)


def _hgrn_kernel(layer, n_chunks, q_ref, f_ref, i_ref, g_ref, lb_ref, on_ref, o_ref,
                 st_ref, stb_ref, oi_ref):
    C, DH, H = HGRN_CHUNK, HGRN_DIM, HGRN_HEADS

    @pl.when(pl.program_id(1) == 0)
    def _():
        st_ref[...] = jnp.zeros_like(st_ref)
        stb_ref[...] = jnp.zeros_like(stb_ref)

    lbr = lb_ref[...]
    e = jnp.exp(lbr - jnp.max(lbr, axis=0, keepdims=True))
    p = e / jnp.sum(e, axis=0, keepdims=True)
    lb = jnp.zeros((1, lbr.shape[1]), F32)
    for j in range(1, layer + 1):
        lb = lb + p[j:j + 1]
    log_lb = jnp.log(lb)
    log_1mlb = jnp.log1p(-lb)
    one_m_lb = 1.0 - lb
    onorm = jnp.tile(on_ref[...], (1, H))

    row_c = lax.broadcasted_iota(jnp.int32, (C, C), 0)
    col_c = lax.broadcasted_iota(jnp.int32, (C, C), 1)
    tri = (col_c <= row_c).astype(BF16)
    rs = lax.broadcasted_iota(jnp.int32, (H * C, H * C), 0)
    cs = lax.broadcasted_iota(jnp.int32, (H * C, H * C), 1)
    same_head_causal = jnp.logical_and(cs <= rs, cs >= rs - rs % C)
    row_1 = lax.broadcasted_iota(jnp.int32, (C, 1), 0)
    heads = [slice(h * DH, (h + 1) * DH) for h in range(H)]

    def stack(x):
        return jnp.concatenate([x[:, hs] for hs in heads], axis=0)

    def unstack(x):
        return jnp.concatenate([x[h * C:(h + 1) * C] for h in range(H)], axis=1)

    def prepare(r, sl):
        q = q_ref[r, sl, :]
        z = f_ref[r, sl, :]
        v = i_ref[r, sl, :]
        ez = jnp.exp(-jnp.abs(z))
        log_sig = jnp.minimum(z, 0.0) - jnp.log1p(ez)
        if layer == 0:
            lf = log_sig
            kk = jnp.where(z >= 0.0, ez, 1.0) / (1.0 + ez)
        else:
            a = log_1mlb + log_sig
            m = jnp.maximum(a, log_lb)
            lf = m + jnp.log(jnp.exp(a - m) + jnp.exp(log_lb - m))
            kk = one_m_lb * jnp.where(z >= 0.0, ez, 1.0) / (1.0 + ez)
        l1, l2, l3 = _split3(lf)
        b = (_dot(tri, l1) + _dot(tri, l2) + _dot(tri, l3)) * LOG2E
        qs = (q * jnp.exp2(b)).astype(BF16)
        o_inter = jnp.concatenate(
            [_dot_nt(qs[:, hs], stb_ref[r * H + h]) for h, hs in enumerate(heads)], axis=1)
        return q, v, kk, b, qs, v.astype(BF16), o_inter

    def intra_factored(r, prepared):
        _, _, kk, b, qs, vb, _ = prepared
        ks = (kk * jnp.exp2(-b)).astype(BF16)
        att = _dot_nt(stack(qs), stack(ks))
        att = jnp.where(same_head_causal, att, 0.0)
        oi_ref[r] = unstack(_dot(att.astype(BF16), stack(vb)))

    def intra_termwise(r, prepared):
        q, v, kk, b, _, _, _ = prepared
        for hs in heads:
            bh, qh, kh, vh = b[:, hs], q[:, hs], kk[:, hs], v[:, hs]
            acc = jnp.zeros((C, DH), F32)
            for s in range(C):
                es = jnp.exp2(jnp.minimum(bh - bh[s:s + 1], 0.0))
                col = jnp.sum(qh * es * kh[s:s + 1], axis=-1, keepdims=True)
                acc = acc + jnp.where(row_1 >= s, col, 0.0) * vh[s:s + 1]
            oi_ref[r, :, hs] = acc

    def finish(r, sl, prepared):
        _, _, kk, b, _, vb, o_inter = prepared
        bl = b[C - 1:C]
        o = oi_ref[r] + o_inter
        kd = (kk * jnp.exp2(bl - b)).astype(BF16)
        decay = jnp.exp2(bl)
        normed = []
        for h, hs in enumerate(heads):
            new = st_ref[r * H + h] * decay[:, hs] + _dot_tn(vb[:, hs], kd[:, hs])
            st_ref[r * H + h] = new
            stb_ref[r * H + h] = new.astype(BF16)
            oh = o[:, hs]
            normed.append(oh * lax.rsqrt(jnp.mean(oh * oh, axis=-1, keepdims=True) + NORM_EPS))
        g = g_ref[r, sl, :]
        y = jnp.concatenate(normed, axis=1) * onorm * (g * _sigmoid(g))
        o_ref[r, sl, :] = y.astype(o_ref.dtype)

    def chunk(ci, carry):
        sl = pl.ds(pl.multiple_of(ci * C, C), C)
        rows = range(q_ref.shape[0])
        prepared = [prepare(r, sl) for r in rows]
        span = -prepared[0][3][C - 1:C]
        for p in prepared[1:]:
            span = jnp.maximum(span, -p[3][C - 1:C])
        factor_ok = jnp.max(span) < FACTOR_LIMIT

        @pl.when(factor_ok)
        def _():
            for r in rows:
                intra_factored(r, prepared[r])

        @pl.when(jnp.logical_not(factor_ok))
        def _():
            for r in rows:
                intra_termwise(r, prepared[r])

        for r in rows:
            finish(r, sl, prepared[r])
        return carry

    lax.fori_loop(0, n_chunks, chunk, 0)


def _hgrn(ph, hgrn_lb, onorm_g, layer, B, S, tc):
    L = hgrn_lb.shape[0]
    W = HGRN_HEADS * HGRN_DIM

    nb = math.gcd(B, HGRN_ROWS)

    def spec(off):
        return pl.BlockSpec((nb, tc, W), lambda b, t: (b, t, off))

    return pl.pallas_call(
        functools.partial(_hgrn_kernel, layer, tc // HGRN_CHUNK),
        grid=(B // nb, S // tc),
        in_specs=[spec(0), spec(1), spec(2), spec(3),
                  pl.BlockSpec((L, W), lambda b, t: (0, 0)),
                  pl.BlockSpec((1, HGRN_DIM), lambda b, t: (0, 0))],
        out_specs=pl.BlockSpec((nb, tc, W), lambda b, t: (b, t, 0)),
        out_shape=jax.ShapeDtypeStruct((B, S, W), BF16),
        scratch_shapes=[pltpu.VMEM((nb * HGRN_HEADS, HGRN_DIM, HGRN_DIM), F32),
                        pltpu.VMEM((nb * HGRN_HEADS, HGRN_DIM, HGRN_DIM), BF16),
                        pltpu.VMEM((nb, HGRN_CHUNK, W), F32)],
        compiler_params=_cparams(("parallel", "arbitrary")),
        name="hgrn2",
    )(ph, ph, ph, ph, hgrn_lb, onorm_g)


def _rope_table_kernel(pos_ref, fr_ref, cos_ref, sin_ref):
    ang = pos_ref[...] * fr_ref[...]
    lane = lax.broadcasted_iota(jnp.int32, ang.shape, 1)
    sn = jnp.sin(ang)
    cos_ref[...] = jnp.cos(ang)
    sin_ref[...] = jnp.where(lane < MLA_NOPE + MLA_ROPE // 2, -sn, sn)


def _rope_tables(posf, freq_row, tr):
    N = posf.shape[0]
    out = jax.ShapeDtypeStruct((N, LANES), F32)
    return pl.pallas_call(
        _rope_table_kernel,
        grid=(N // tr,),
        in_specs=[pl.BlockSpec((tr, 1), lambda i: (i, 0)), pl.BlockSpec((1, LANES), lambda i: (0, 0))],
        out_specs=(pl.BlockSpec((tr, LANES), lambda i: (i, 0)), pl.BlockSpec((tr, LANES), lambda i: (i, 0))),
        out_shape=(out, out),
        compiler_params=_cparams(("parallel",)),
        name="rope_tables",
    )(posf, freq_row)


def _mla_prep_kernel(pm_ref, cos_ref, sin_ref, qg_ref, kvg_ref, wq_ref, wqr_ref, wkv_ref,
                     qn_ref, qnr_ref, kn_ref, knr_ref, q_ref, k_ref, v_ref):
    pm = pm_ref[...]
    o = 0
    ql = pm[:, o:o + MLA_Q_RANK]; o += MLA_Q_RANK
    kvl = pm[:, o:o + MLA_KV_RANK]; o += MLA_KV_RANK
    kr = pm[:, o:o + LANES]; o += LANES
    krr = pm[:, o:o + LANES]

    def rms(x, g):
        return x * lax.rsqrt(jnp.mean(x * x, axis=-1, keepdims=True) + NORM_EPS) * g

    qlb = rms(ql, qg_ref[...]).astype(BF16)
    qa = _dot(qlb, wq_ref[...])
    qr = _dot(qlb, wqr_ref[...])
    kva = _dot(rms(kvl, kvg_ref[...]).astype(BF16), wkv_ref[...])
    v_ref[...] = kva[:, MLA_HEADS * LANES:].astype(v_ref.dtype)

    cs = cos_ref[...]
    sn = sin_ref[...]
    scale = LOG2E / math.sqrt(MLA_QK)
    q_cos = qn_ref[...] * cs * scale
    q_sin = qnr_ref[...] * sn * scale
    k_cos = kn_ref[...] * cs
    k_sin = knr_ref[...] * sn
    k_rot = krr * k_sin

    def inv_rms(xh):
        return lax.rsqrt(jnp.sum(xh * xh, axis=-1, keepdims=True) * (1.0 / MLA_QK) + NORM_EPS)

    for h in range(MLA_HEADS):
        hs = slice(h * LANES, (h + 1) * LANES)
        qh = qa[:, hs]
        q_ref[0, h] = ((qh * q_cos + qr[:, hs] * q_sin) * inv_rms(qh)).astype(q_ref.dtype)
        kh = kva[:, hs] + kr
        k_ref[0, h] = ((kh * k_cos + k_rot) * inv_rms(kh)).astype(k_ref.dtype)


def _mla_prep(pm, cos_t, sin_t, qlat_g, kvlat_g, wq, wqr, wkv, qn, qnr, kn, knr, B, S, tr):
    N = pm.shape[0]
    nb = S // tr
    H = MLA_HEADS
    full = lambda a: pl.BlockSpec(a.shape, lambda i: (0,) * a.ndim)
    return pl.pallas_call(
        _mla_prep_kernel,
        grid=(N // tr,),
        in_specs=[pl.BlockSpec((tr, pm.shape[1]), lambda i: (i, 0)),
                  pl.BlockSpec((tr, LANES), lambda i: (i, 0)),
                  pl.BlockSpec((tr, LANES), lambda i: (i, 0)),
                  full(qlat_g), full(kvlat_g), full(wq), full(wqr), full(wkv),
                  full(qn), full(qnr), full(kn), full(knr)],
        out_specs=(pl.BlockSpec((1, H, tr, LANES), lambda i: (i // nb, 0, i % nb, 0)),
                   pl.BlockSpec((1, H, tr, LANES), lambda i: (i // nb, 0, i % nb, 0)),
                   pl.BlockSpec((tr, H * MLA_V), lambda i: (i, 0))),
        out_shape=(jax.ShapeDtypeStruct((B, H, S, LANES), BF16),
                   jax.ShapeDtypeStruct((B, H, S, LANES), BF16),
                   jax.ShapeDtypeStruct((N, H * MLA_V), BF16)),
        compiler_params=_cparams(("parallel",)),
        name="mla_prep",
    )(pm, cos_t, sin_t, qlat_g, kvlat_g, wq, wqr, wkv, qn, qnr, kn, knr)


def _attn_kernel(q_ref, k_ref, v_ref, o_ref, m_ref, l_ref, acc_ref):
    qi = pl.program_id(2)
    T = q_ref.shape[2]

    m_ref[...] = jnp.full_like(m_ref, NEG_BIG)
    l_ref[...] = jnp.zeros_like(l_ref)
    acc_ref[...] = jnp.zeros_like(acc_ref)

    n_heads = q_ref.shape[1]

    def step(ki, masked):
        rows = pl.ds(pl.multiple_of(ki * T, T), T)
        lane = lax.broadcasted_iota(jnp.int32, (T, LANES), 1)
        if masked:
            row = lax.broadcasted_iota(jnp.int32, (T, T), 0)
            col = lax.broadcasted_iota(jnp.int32, (T, T), 1)
            keep = col <= row
        for h in range(n_heads):
            vblk = v_ref[0, rows, (h // 2) * LANES:(h // 2 + 1) * LANES]
            s = _dot_nt(q_ref[0, h], k_ref[0, h, rows, :])
            if masked:
                s = jnp.where(keep, s, NEG_BIG)
            m_old = m_ref[h]
            m_new = jnp.maximum(m_old, jnp.max(s, axis=-1, keepdims=True))
            alpha = jnp.exp2(m_old - m_new)
            p = jnp.exp2(s - jnp.tile(m_new, (1, T // LANES)))
            l_ref[h] = alpha * l_ref[h] + jnp.sum(p, axis=-1, keepdims=True)
            vh = jnp.where((lane < MLA_V) if h % 2 == 0 else (lane >= MLA_V), vblk, jnp.zeros_like(vblk))
            acc_ref[h] = alpha * acc_ref[h] + _dot(p.astype(BF16), vh)
            m_ref[h] = m_new

    def below_diagonal(ki, c):
        step(ki, False)
        return c

    lax.fori_loop(0, qi, below_diagonal, 0)
    step(qi, True)
    for pair in range(n_heads // 2):
        h0, h1 = 2 * pair, 2 * pair + 1
        o_ref[0, :, pair * LANES:(pair + 1) * LANES] = (
            acc_ref[h0] / l_ref[h0] + acc_ref[h1] / l_ref[h1]).astype(o_ref.dtype)


def _attention(q, k, v3, B, S, T):
    H = MLA_HEADS
    hs = ATTN_HEADS
    w = hs // 2 * LANES
    n = S // T
    return pl.pallas_call(
        _attn_kernel,
        grid=(B, H // hs, n),
        in_specs=[pl.BlockSpec((1, hs, T, LANES), lambda b, hg, qi: (b, hg, qi, 0)),
                  pl.BlockSpec((1, hs, S, LANES), lambda b, hg, qi: (b, hg, 0, 0)),
                  pl.BlockSpec((1, S, w), lambda b, hg, qi: (b, 0, hg))],
        out_specs=pl.BlockSpec((1, T, w), lambda b, hg, qi: (b, qi, hg)),
        out_shape=jax.ShapeDtypeStruct((B, S, H * MLA_V), BF16),
        scratch_shapes=[pltpu.VMEM((hs, T, LANES), F32), pltpu.VMEM((hs, T, LANES), F32),
                        pltpu.VMEM((hs, T, LANES), F32)],
        compiler_params=_cparams(("parallel", "parallel", "arbitrary")),
        name="mla_attention",
    )(q, k, v3)


def _pool_kernel(u_ref, w_ref, sc_ref, o_ref):
    S = u_ref.shape[1]
    row = lax.broadcasted_iota(jnp.int32, (S, LANES), 0)
    t1 = (row + 1).astype(F32)
    for gi, win in enumerate(POOL_WINDOWS):
        u = u_ref[0, :, gi * LANES:(gi + 1) * LANES]
        s = u
        k = 1
        while k < win:
            s = s + jnp.where(row >= k, pltpu.roll(s, k, 0), 0.0)
            k *= 2
        pooled = s / jnp.minimum(t1, float(win))
        mixed = _dot((pooled - u).astype(BF16), w_ref[gi].astype(BF16))
        o_ref[0, :, gi * LANES:(gi + 1) * LANES] = (
            mixed * sc_ref[:, gi * LANES:(gi + 1) * LANES]).astype(o_ref.dtype)


def _pool(pp3, w_pool, scale):
    B, S, W = pp3.shape
    return pl.pallas_call(
        _pool_kernel,
        grid=(B,),
        in_specs=[pl.BlockSpec((1, S, W), lambda b: (b, 0, 0)),
                  pl.BlockSpec(w_pool.shape, lambda b: (0, 0, 0)),
                  pl.BlockSpec((1, W), lambda b: (0, 0))],
        out_specs=pl.BlockSpec((1, S, W), lambda b: (b, 0, 0)),
        out_shape=jax.ShapeDtypeStruct((B, S, W), BF16),
        compiler_params=_cparams(("parallel",)),
        name="pool_mixer",
    )(pp3, w_pool, scale)


def _merge_router_kernel(x_ref, gt_ref, ya_ref, yb_ref, yc_ref, wa_ref, wb_ref, wc_ref, wo_ref, g1_ref,
                         g_ref, sh_ref, sc_ref, wr_ref, br_ref, xo_ref, hb_ref, ls_ref, wt_ref, cnt_ref):
    D = x_ref.shape[1]
    merged = None
    for j, (y_ref, w_ref) in enumerate(((ya_ref, wa_ref), (yb_ref, wb_ref), (yc_ref, wc_ref))):
        y = _dot(y_ref[...], w_ref[...])
        gate = _sigmoid(gt_ref[:, j * D:(j + 1) * D].astype(F32))
        merged = gate * y if merged is None else merged + gate * y
    x = x_ref[...] + g1_ref[0] * _dot(merged.astype(BF16), wo_ref[...])
    xo_ref[...] = x
    _route(x, g_ref, sh_ref, sc_ref, wr_ref, br_ref, hb_ref, ls_ref, wt_ref, cnt_ref)


def _route(x, g_ref, sh_ref, sc_ref, wr_ref, br_ref, hb_ref, ls_ref, wt_ref, cnt_ref):
    tr = x.shape[0]

    ms = jnp.mean(x * x, axis=-1, keepdims=True)
    h = x * lax.rsqrt(ms + NORM_EPS) * g_ref[...]
    h = h * (1.0 + sc_ref[0]) + sh_ref[0]
    hb_ref[...] = h.astype(hb_ref.dtype)
    h1, h2, _ = _split3(h)
    w1, w2, _ = _split3(wr_ref[...])
    logits = _dot(h1, w1) + (_dot(h1, w2) + _dot(h2, w1)) + br_ref[...]
    lane = lax.broadcasted_iota(jnp.int32, logits.shape, 1)
    work = logits
    vals, ids = [], []
    for _k in range(TOP_K):
        m = jnp.max(work, axis=-1, keepdims=True)
        sel = jnp.min(jnp.where(work == m, lane, LANES), axis=-1, keepdims=True)
        vals.append(m)
        ids.append(sel)
        work = jnp.where(lane == sel, -jnp.inf, work)
    es = [jnp.exp(v - vals[0]) for v in vals]
    den = es[0] + es[1] + es[2] + es[3]
    onehot = jnp.zeros(logits.shape, F32)
    wt_out = jnp.zeros(logits.shape, F32)
    for k in range(TOP_K):
        onehot = onehot + (lane == ids[k]).astype(F32)
        wt_out = jnp.where(lane == k, es[k] / den, wt_out)
    wt_ref[...] = wt_out

    r = lax.broadcasted_iota(jnp.int32, (tr, tr), 0)
    c = lax.broadcasted_iota(jnp.int32, (tr, tr), 1)
    strict = (c < r).astype(BF16)
    before = _dot(strict, onehot.astype(BF16))
    cnt = jnp.sum(onehot, axis=0, keepdims=True)
    cnt_ref[0] = jnp.broadcast_to(cnt, cnt_ref.shape[1:])
    units = jnp.floor((cnt + (SEG_ALIGN - 1.0)) * (1.0 / SEG_ALIGN))
    er = lax.broadcasted_iota(jnp.int32, (LANES, LANES), 0)
    ec = lax.broadcasted_iota(jnp.int32, (LANES, LANES), 1)
    upper = (er < ec).astype(BF16)
    seg_start = _dot(jnp.broadcast_to(units, (8, LANES)).astype(BF16), upper)[0:1] * float(SEG_ALIGN)
    pos = before + seg_start
    ls_out = jnp.full(logits.shape, -1.0, F32)
    for k in range(TOP_K):
        lk = jnp.sum(jnp.where(lane == ids[k], pos, 0.0), axis=-1, keepdims=True)
        ls_out = jnp.where(lane == k, lk, ls_out)
    ls_ref[...] = ls_out.astype(jnp.int32)


def _local_rows(tb):
    return TOP_K * tb + N_EXPERTS * SEG_ALIGN


def _sort_chunk(rl):
    return SORT_CHUNK if rl % SORT_CHUNK == 0 else SORT_CHUNK // 2


def _merge_router(x2, gates, ya, yb, yc, wa, wb, wc, wo, g, mod3, l, wr_p, br_p, B, S, tr):
    N, D = x2.shape
    nb = S // tr
    base = l * B * 6
    full = lambda a: pl.BlockSpec(a.shape, lambda i: (0,) * a.ndim)
    rows = lambda w: pl.BlockSpec((tr, w), lambda i: (i, 0))

    def modspec(j):
        return pl.BlockSpec((1, 1, D), lambda i: (base + (i // nb) * 6 + j, 0, 0))

    return pl.pallas_call(
        _merge_router_kernel,
        grid=(N // tr,),
        in_specs=[rows(D), rows(gates.shape[1]), rows(ya.shape[1]), rows(yb.shape[1]), rows(yc.shape[1]),
                  full(wa), full(wb), full(wc), full(wo), modspec(2),
                  full(g), modspec(3), modspec(4), full(wr_p), full(br_p)],
        out_specs=(rows(D), rows(D), rows(LANES), rows(LANES),
                   pl.BlockSpec((1, 8, LANES), lambda i: (i, 0, 0))),
        out_shape=(jax.ShapeDtypeStruct((N, D), F32),
                   jax.ShapeDtypeStruct((N, D), BF16),
                   jax.ShapeDtypeStruct((N, LANES), jnp.int32),
                   jax.ShapeDtypeStruct((N, LANES), F32),
                   jax.ShapeDtypeStruct((N // tr, 8, LANES), F32)),
        compiler_params=_cparams(("parallel",)),
        name="merge_router",
    )(x2, gates, ya, yb, yc, wa, wb, wc, wo, mod3, g, mod3, mod3, wr_p, br_p)


def _segment_copies(j, e, src_ref, dst_ref, len_ref, local_ref, global_ref, sem, to_global, n_bits):
    base = j * N_EXPERTS + e
    units = len_ref[base]
    s0 = src_ref[base]
    d0 = dst_ref[base]
    copies = []
    for bit in range(n_bits):
        rows = SEG_ALIGN << bit
        done = ((units >> (bit + 1)) << (bit + 1)) * SEG_ALIGN
        loc = local_ref.at[pl.ds(pl.multiple_of(s0 + done, SEG_ALIGN), rows)]
        glo = global_ref.at[pl.ds(pl.multiple_of(d0 + done, SEG_ALIGN), rows)]
        cp = pltpu.make_async_copy(loc, glo, sem) if to_global else pltpu.make_async_copy(glo, loc, sem)
        copies.append((((units >> bit) & 1) == 1, cp))
    return copies


def _each_segment_copy(j, src_ref, dst_ref, len_ref, local_ref, global_ref, sem, to_global, n_bits, action):
    def per_expert(e, c):
        for pred, cp in _segment_copies(j, e, src_ref, dst_ref, len_ref, local_ref, global_ref,
                                        sem, to_global, n_bits):
            @pl.when(pred)
            def _():
                action(cp)
        return c
    lax.fori_loop(0, N_EXPERTS, per_expert, 0)


def _sort_place_kernel(n_bits, src_ref, dst_ref, len_ref, zflag_ref, hb_ref, ls_ref, xs_hbm,
                       zbuf, xl_buf, zsem, sem):
    j = pl.program_id(0)
    last = pl.num_programs(0) - 1
    tr = hb_ref.shape[0]
    tm = zbuf.shape[0]
    rl = xl_buf.shape[1]
    n_tiles = xs_hbm.shape[0] // tm
    slot = j % 2

    def segments(block, buf_slot, action):
        _each_segment_copy(block, src_ref, dst_ref, len_ref, xl_buf.at[buf_slot], xs_hbm,
                           sem.at[buf_slot], True, n_bits, action)

    @pl.when(j == 0)
    def _():
        zbuf[...] = jnp.zeros_like(zbuf)

        def zero_copy(t):
            return pltpu.make_async_copy(zbuf, xs_hbm.at[pl.ds(pl.multiple_of(t * tm, tm), tm)], zsem)

        def zstart(t, c):
            @pl.when(zflag_ref[t] > 0)
            def _():
                zero_copy(t).start()
            return c

        def zwait(t, c):
            @pl.when(zflag_ref[t] > 0)
            def _():
                zero_copy(t).wait()
            return c

        lax.fori_loop(0, n_tiles, zstart, 0)
        lax.fori_loop(0, n_tiles, zwait, 0)

    @pl.when(j >= 2)
    def _():
        segments(j - 2, slot, lambda cp: cp.wait())

    ls_t = jnp.transpose(ls_ref[...].astype(F32))
    targets = [ls_t[k:k + 1, :].astype(jnp.int32) for k in range(TOP_K)]
    hb = hb_ref[...]
    ch = _sort_chunk(rl)
    for cidx in range(rl // ch):
        ri = lax.broadcasted_iota(jnp.int32, (ch, tr), 0) + cidx * ch
        hit = ri == targets[0]
        for k in range(1, TOP_K):
            hit = jnp.logical_or(hit, ri == targets[k])
        perm = jnp.where(hit, 1.0, 0.0).astype(BF16)
        xl_buf[slot, cidx * ch:(cidx + 1) * ch, :] = _dot(perm, hb).astype(xl_buf.dtype)

    segments(j, slot, lambda cp: cp.start())

    @pl.when(j == last)
    def _():
        @pl.when(j >= 1)
        def _():
            segments(j - 1, 1 - slot, lambda cp: cp.wait())
        segments(j, slot, lambda cp: cp.wait())


def _sort_place(seg_src, seg_dst, seg_len, zero_flags, hb, ls, n_rows, tr, tm, n_bits):
    N, D = hb.shape
    rl = _local_rows(tr)
    grid_spec = pltpu.PrefetchScalarGridSpec(
        num_scalar_prefetch=4,
        grid=(N // tr,),
        in_specs=[pl.BlockSpec((tr, D), lambda j, a, b, c, d: (j, 0)),
                  pl.BlockSpec((tr, LANES), lambda j, a, b, c, d: (j, 0))],
        out_specs=pl.BlockSpec(memory_space=pl.ANY),
        scratch_shapes=[pltpu.VMEM((tm, D), hb.dtype), pltpu.VMEM((2, rl, D), hb.dtype),
                        pltpu.SemaphoreType.DMA(()), pltpu.SemaphoreType.DMA((2,))],
    )
    return pl.pallas_call(
        functools.partial(_sort_place_kernel, n_bits),
        grid_spec=grid_spec,
        out_shape=jax.ShapeDtypeStruct((n_rows, D), hb.dtype),
        compiler_params=_cparams(("arbitrary",)),
        name="moe_sort_place",
    )(seg_src, seg_dst, seg_len, zero_flags, hb, ls)


def _expert_kernel(first_expert, te_ref, nxt_ref, rows_ref, nu_ref, x_ref, w1_hbm, b1_ref, w2_hbm, b2_ref,
                   o_ref, w1f, w2f, w1b, w2b, sem):
    t = pl.program_id(0)
    tm = x_ref.shape[0]
    F = w2b.shape[0]
    rows = rows_ref[t]

    def weight_copies(e):
        return (pltpu.make_async_copy(w1_hbm.at[first_expert + e], w1f, sem.at[0]),
                pltpu.make_async_copy(w2_hbm.at[first_expert + e], w2f, sem.at[1]))

    @pl.when(t == 0)
    def _():
        for cp in weight_copies(te_ref[0]):
            cp.start()

    @pl.when(jnp.logical_and(rows > 0, jnp.logical_or(t == 0, te_ref[t] != te_ref[jnp.maximum(t - 1, 0)])))
    def _():
        for cp in weight_copies(te_ref[t]):
            cp.wait()
        w1b[...] = w1f[...].astype(BF16)
        w2b[...] = w2f[...].astype(BF16)

        @pl.when(nxt_ref[t] >= 0)
        def _():
            for cp in weight_copies(nxt_ref[t]):
                cp.start()

    def mlp(m):
        a = _dot(x_ref[:m, :], w1b[...]) + b1_ref[0]
        glu = jnp.minimum(a[:, :F], SWIGLU_LIMIT)
        lin = jnp.clip(a[:, F:], -SWIGLU_LIMIT, SWIGLU_LIMIT)
        act = glu * _sigmoid(SWIGLU_ALPHA * glu) * (lin + 1.0)
        o_ref[:m, :] = (_dot(act.astype(BF16), w2b[...]) + b2_ref[0]).astype(o_ref.dtype)
        if m < tm:
            o_ref[m:, :] = jnp.zeros((tm - m, o_ref.shape[1]), o_ref.dtype)

    quarter = tm // EXPERT_ROW_STEPS
    for q in range(1, EXPERT_ROW_STEPS + 1):
        @pl.when(jnp.logical_and(rows > (q - 1) * quarter, rows <= q * quarter))
        def _():
            mlp(q * quarter)

    @pl.when(rows == 0)
    def _():
        o_ref[...] = jnp.zeros_like(o_ref)


def _experts(tile_expert, next_expert, tile_rows, n_used, xs, w1_all, b1_all, w2_all, b2_all, l, tm):
    n_tiles = tile_expert.shape[0]
    D = xs.shape[1]
    L, E, _, F2 = w1_all.shape
    F = w2_all.shape[2]
    w1 = w1_all.reshape(L * E, D, F2)
    w2 = w2_all.reshape(L * E, F, D)
    b1 = b1_all.reshape(L * E, F2)
    b2 = b2_all.reshape(L * E, D)
    E0 = l * E
    E = L * E
    grid_spec = pltpu.PrefetchScalarGridSpec(
        num_scalar_prefetch=4,
        grid=(n_tiles,),
        in_specs=[pl.BlockSpec((tm, D), lambda t, te, nx, tr, nu: (jnp.minimum(t, nu[0] - 1), 0)),
                  pl.BlockSpec(memory_space=pl.ANY),
                  pl.BlockSpec((1, 1, F2), lambda t, te, nx, tr, nu: (E0 + te[t], 0, 0)),
                  pl.BlockSpec(memory_space=pl.ANY),
                  pl.BlockSpec((1, 1, D), lambda t, te, nx, tr, nu: (E0 + te[t], 0, 0))],
        out_specs=pl.BlockSpec((tm, D), lambda t, te, nx, tr, nu: (t, 0)),
        scratch_shapes=[pltpu.VMEM((D, F2), F32), pltpu.VMEM((F, D), F32),
                        pltpu.VMEM((D, F2), BF16), pltpu.VMEM((F, D), BF16),
                        pltpu.SemaphoreType.DMA((2,))],
    )
    return pl.pallas_call(
        functools.partial(_expert_kernel, E0),
        grid_spec=grid_spec,
        out_shape=jax.ShapeDtypeStruct((n_tiles * tm, D), BF16),
        compiler_params=_cparams(("arbitrary",)),
        name="expert_mlp",
    )(tile_expert, next_expert, tile_rows, n_used, xs, w1, b1.reshape(E, 1, F2), w2, b2.reshape(E, 1, D))


def _combine_kernel(n_bits, src_ref, dst_ref, len_ref, ys_hbm, x_ref, ls_ref, wt_ref, g2_ref, o_ref,
                    ybuf, sem):
    j = pl.program_id(0)
    tn = x_ref.shape[0]
    rl = ybuf.shape[1]
    slot = j % 2

    def fetch(block, buf_slot, action):
        _each_segment_copy(block, src_ref, dst_ref, len_ref, ybuf.at[buf_slot], ys_hbm,
                           sem.at[buf_slot], False, n_bits, action)

    def start_fetch(block, buf_slot):
        ybuf[buf_slot] = jnp.zeros((rl, ybuf.shape[2]), ybuf.dtype)
        fetch(block, buf_slot, lambda cp: cp.start())

    @pl.when(j == 0)
    def _():
        start_fetch(j, slot)

    @pl.when(j + 1 < pl.num_programs(0))
    def _():
        start_fetch(j + 1, 1 - slot)

    fetch(j, slot, lambda cp: cp.wait())
    ls = ls_ref[...]
    wt = wt_ref[...]
    ch = _sort_chunk(rl)
    y = jnp.zeros(x_ref.shape, F32)
    for cidx in range(rl // ch):
        li = lax.broadcasted_iota(jnp.int32, (tn, ch), 1) + cidx * ch
        pw = jnp.zeros((tn, ch), F32)
        for k in range(TOP_K):
            pw = jnp.where(li == ls[:, k:k + 1], wt[:, k:k + 1], pw)
        y = y + _dot(pw.astype(BF16), ybuf[slot, cidx * ch:(cidx + 1) * ch, :])
    o_ref[...] = x_ref[...] + g2_ref[0] * y


def _combine(seg_src, seg_dst, seg_len, ys, x2, ls, wts, mod3, l, B, S, tn, n_bits):
    N, D = x2.shape
    nb = S // tn
    base = l * B * 6
    rl = _local_rows(tn)
    grid_spec = pltpu.PrefetchScalarGridSpec(
        num_scalar_prefetch=3,
        grid=(N // tn,),
        in_specs=[pl.BlockSpec(memory_space=pl.ANY),
                  pl.BlockSpec((tn, D), lambda i, a, b, c: (i, 0)),
                  pl.BlockSpec((tn, LANES), lambda i, a, b, c: (i, 0)),
                  pl.BlockSpec((tn, LANES), lambda i, a, b, c: (i, 0)),
                  pl.BlockSpec((1, 1, D), lambda i, a, b, c: (base + (i // nb) * 6 + 5, 0, 0))],
        out_specs=pl.BlockSpec((tn, D), lambda i, a, b, c: (i, 0)),
        scratch_shapes=[pltpu.VMEM((2, rl, D), ys.dtype), pltpu.SemaphoreType.DMA((2,))],
    )
    return pl.pallas_call(
        functools.partial(_combine_kernel, n_bits),
        grid_spec=grid_spec,
        out_shape=jax.ShapeDtypeStruct((N, D), F32),
        compiler_params=_cparams(("arbitrary",)),
        name="moe_combine",
    )(seg_src, seg_dst, seg_len, ys, x2, ls, wts, mod3)


def _tile_rows(S, want):
    t = min(S, want)
    while S % t:
        t //= 2
    return t


def _prep_in_weights(w_in_l):
    HW = HGRN_HEADS * HGRN_DIM
    o = 0
    wh = w_in_l[:, o:o + 4 * HW]; o += 4 * HW
    wql = w_in_l[:, o:o + MLA_Q_RANK]; o += MLA_Q_RANK
    wkvl = w_in_l[:, o:o + MLA_KV_RANK]; o += MLA_KV_RANK
    wkr = w_in_l[:, o:o + MLA_ROPE]; o += MLA_ROPE
    wp = w_in_l[:, o:o + HW]; o += HW
    wg = w_in_l[:, o:]
    D = w_in_l.shape[0]
    wkr_p = jnp.zeros((D, LANES), w_in_l.dtype).at[:, MLA_NOPE:MLA_NOPE + MLA_ROPE].set(wkr)
    wm = jnp.concatenate([wql, wkvl, wkr_p, _rope_partner(wkr_p)], axis=1)
    return wg.astype(BF16), wh.astype(BF16), wm.astype(BF16), wp.astype(BF16)


def _rope_partner(w):
    half = MLA_ROPE // 2
    w3 = w.reshape(w.shape[:-1] + (-1, LANES))
    first = w3[..., MLA_NOPE:MLA_NOPE + half]
    second = w3[..., MLA_NOPE + half:MLA_NOPE + MLA_ROPE]
    out = jnp.concatenate([jnp.zeros_like(w3[..., :MLA_NOPE]), second, first,
                           jnp.zeros_like(w3[..., MLA_NOPE + MLA_ROPE:])], axis=-1)
    return out.reshape(w.shape)


def _pad_heads(w, n_heads, width):
    K = w.shape[0]
    w3 = w.reshape(K, n_heads, width)
    return jnp.pad(w3, ((0, 0), (0, 0), (0, LANES - width))).reshape(K, n_heads * LANES)


def kernel(x, c, positions, ada_w, ada_b, norm1_g, norm2_g, w_in, hgrn_lb, hgrn_onorm_g, mla_qlat_g, mla_kvlat_g, w_uq, w_ukv, q_norm_g, k_norm_g, w_pool, pool_scale, w_br_a, w_br_b, w_br_c, w_out, w_router, b_router, w_exp1, b_exp1, w_exp2, b_exp2):
    B, S, D = x.shape
    L = ada_w.shape[0]
    N = B * S
    H = MLA_HEADS
    tr = _tile_rows(S, 512)
    tc = _tile_rows(S, 128)
    ta = _tile_rows(S, 512)
    tm = 512

    mod = _ada(c, ada_w, ada_b)
    mod3 = mod.reshape(L * B * 6, 1, D)

    inv_freq = 1.0 / (ROPE_THETA ** (jnp.arange(0, MLA_ROPE, 2, dtype=F32) / MLA_ROPE))
    freq_row = jnp.zeros((1, LANES), F32).at[0, MLA_NOPE:MLA_NOPE + MLA_ROPE].set(
        jnp.concatenate([inv_freq, inv_freq]))
    posf = positions.astype(F32).reshape(N, 1)
    cos_t, sin_t = _rope_tables(posf, freq_row, tr)

    max_rows = N * TOP_K + (N // tr) * N_EXPERTS * (SEG_ALIGN - 1)
    n_tiles = -(-max_rows // tm) + N_EXPERTS
    n_bits = (tr // SEG_ALIGN).bit_length()
    x2 = x.reshape(N, D)
    for l in range(L):
        wg, wh, wm, wp = _prep_in_weights(w_in[l])
        pg, ph, pm, pp = _inproj(x2, norm1_g[l].reshape(1, D), mod3, l, wg, wh, wm, wp, B, S, tr)

        ya = _hgrn(ph.reshape(B, S, -1), hgrn_lb, hgrn_onorm_g[l].reshape(1, HGRN_DIM), l, B, S, tc)

        wq_f = _pad_heads(w_uq[l], H, MLA_QK)
        wq_p = wq_f.astype(BF16)
        wqr_p = _rope_partner(wq_f).astype(BF16)
        wkv3 = w_ukv[l].reshape(MLA_KV_RANK, H, MLA_NOPE + MLA_V)
        wkv_p = jnp.concatenate(
            [_pad_heads(wkv3[:, :, :MLA_NOPE].reshape(MLA_KV_RANK, H * MLA_NOPE), H, MLA_NOPE),
             wkv3[:, :, MLA_NOPE:].reshape(MLA_KV_RANK, H * MLA_V)], axis=1).astype(BF16)
        qn_p = jnp.pad(q_norm_g[l], (0, LANES - MLA_QK)).reshape(1, LANES)
        kn_p = jnp.pad(k_norm_g[l], (0, LANES - MLA_QK)).reshape(1, LANES)
        q, k, v = _mla_prep(pm, cos_t, sin_t, mla_qlat_g[l].reshape(1, -1), mla_kvlat_g[l].reshape(1, -1),
                            wq_p, wqr_p, wkv_p, qn_p, _rope_partner(qn_p), kn_p, _rope_partner(kn_p),
                            B, S, tr)
        yb = _attention(q, k, v.reshape(B, S, H * MLA_V), B, S, ta)

        yc = _pool(pp.reshape(B, S, -1), w_pool[l], pool_scale[l].reshape(1, -1))

        wr_p = jnp.pad(w_router[l], ((0, 0), (0, LANES - N_EXPERTS)))
        br_p = jnp.concatenate([b_router[l], jnp.full((LANES - N_EXPERTS,), NEG_BIG, F32)]).reshape(1, LANES)
        x2, hb, ls, wts, cnt = _merge_router(
            x2, pg, ya.reshape(N, -1), yb.reshape(N, -1), yc.reshape(N, -1),
            w_br_a[l].astype(BF16), w_br_b[l].astype(BF16), w_br_c[l].astype(BF16), w_out[l].astype(BF16),
            norm2_g[l].reshape(1, D), mod3, l, wr_p, br_p, B, S, tr)

        seg = ((cnt[:, 0, :N_EXPERTS].astype(jnp.int32) + SEG_ALIGN - 1) // SEG_ALIGN) * SEG_ALIGN
        seg_src = jnp.cumsum(seg, axis=1) - seg
        group = jnp.sum(seg, axis=0)
        tiles_per = (group + tm - 1) // tm
        tile_end = jnp.cumsum(tiles_per)
        offsets = (tile_end - tiles_per) * tm
        seg_dst = offsets[None, :] + jnp.cumsum(seg, axis=0) - seg
        n_used = tile_end[-1:]
        tile_ids = jnp.minimum(jnp.arange(n_tiles, dtype=jnp.int32), n_used[0] - 1)
        tile_expert = jnp.minimum(
            jnp.sum((tile_end[None, :] <= tile_ids[:, None]).astype(jnp.int32), axis=1), N_EXPERTS - 1)
        all_tiles = jnp.arange(n_tiles, dtype=jnp.int32)
        is_tail = jnp.any((tile_end[None, :] - 1 == all_tiles[:, None]) & (tiles_per[None, :] > 0), axis=1)
        zero_flags = (is_tail | (all_tiles >= n_used[0])).astype(jnp.int32)
        seg_src = seg_src.reshape(-1).astype(jnp.int32)
        seg_dst = seg_dst.reshape(-1).astype(jnp.int32)
        seg_len = (seg // SEG_ALIGN).reshape(-1).astype(jnp.int32)

        mine = tile_expert[:, None] == jnp.arange(N_EXPERTS, dtype=jnp.int32)[None, :]
        first_tile = jnp.sum(jnp.where(mine, (tile_end - tiles_per)[None, :], 0), axis=1)
        group_rows = jnp.sum(jnp.where(mine, group[None, :], 0), axis=1)
        tile_rows = jnp.where(all_tiles < n_used[0],
                              jnp.clip(group_rows - (all_tiles - first_tile) * tm, 0, tm), 0)

        xs = _sort_place(seg_src, seg_dst, seg_len, zero_flags, hb, ls, n_tiles * tm, tr, tm, n_bits)
        group_end = jnp.sum(jnp.where(mine, tile_end[None, :], 0), axis=1)
        following = jnp.minimum(
            jnp.sum((tile_end[None, :] <= group_end[:, None]).astype(jnp.int32), axis=1), N_EXPERTS - 1)
        next_expert = jnp.where(group_end < n_used[0], following, -1)

        ys = _experts(tile_expert.astype(jnp.int32), next_expert.astype(jnp.int32), tile_rows.astype(jnp.int32),
                      n_used.astype(jnp.int32), xs, w_exp1, b_exp1, w_exp2, b_exp2, l, tm)
        x2 = _combine(seg_src, seg_dst, seg_len, ys, x2, ls, wts, mod3, l, B, S, tr, n_bits)
    return x2.reshape(B, S, D)
```

```python
import functools
import math

import jax
import jax.numpy as jnp
from jax import lax
from jax.experimental import pallas as pl
from jax.experimental.pallas import tpu as pltpu

F32 = jnp.float32
BF16 = jnp.bfloat16

LANES = 128
NORM_EPS = 1e-6
HGRN_HEADS = 4
HGRN_DIM = 128
HGRN_CHUNK = 32
HGRN_ROWS = 8
MLA_HEADS = 8
MLA_NOPE = 64
MLA_ROPE = 32
MLA_QK = MLA_NOPE + MLA_ROPE
MLA_V = 64
MLA_Q_RANK = 384
MLA_KV_RANK = 256
ROPE_THETA = 10000.0
POOL_WINDOWS = (2, 4, 8, 16)
N_EXPERTS = 32
TOP_K = 4
SWIGLU_LIMIT = 7.0
SWIGLU_ALPHA = 1.702
NEG_BIG = -1e30
LOG2E = 1.4426950408889634
FACTOR_LIMIT = 100.0
SEG_ALIGN = 16
SORT_CHUNK = 256
EXPERT_ROW_STEPS = 4
ATTN_HEADS = 4

VMEM_LIMIT = 56 * 1024 * 1024


def _cparams(sem):
    return pltpu.CompilerParams(dimension_semantics=sem, vmem_limit_bytes=VMEM_LIMIT)


def _dot(a, b):
    return jnp.dot(a, b, preferred_element_type=F32)


def _dot_nt(a, b):
    return lax.dot_general(a, b, (((1,), (1,)), ((), ())), preferred_element_type=F32)


def _dot_tn(a, b):
    return lax.dot_general(a, b, (((0,), (0,)), ((), ())), preferred_element_type=F32)


def _split3(x):
    hi = x.astype(BF16)
    r = x - hi.astype(F32)
    mid = r.astype(BF16)
    lo = (r - mid.astype(F32)).astype(BF16)
    return hi, mid, lo


def _sigmoid(x):
    return 1.0 / (1.0 + jnp.exp(-x))


def _ada_kernel(c_ref, w_ref, b_ref, o_ref):
    c = c_ref[...]
    cond = (c * _sigmoid(c)).astype(BF16)
    o_ref[0] = _dot(cond, w_ref[0].astype(BF16)) + b_ref[0]


def _ada(c, ada_w, ada_b):
    L, D, W = ada_w.shape
    B = c.shape[0]
    tn = D
    return pl.pallas_call(
        _ada_kernel,
        grid=(L, W // tn),
        in_specs=[
            pl.BlockSpec((B, D), lambda l, j: (0, 0)),
            pl.BlockSpec((1, D, tn), lambda l, j: (l, 0, j)),
            pl.BlockSpec((1, 1, tn), lambda l, j: (l, 0, j)),
        ],
        out_specs=pl.BlockSpec((1, B, tn), lambda l, j: (l, 0, j)),
        out_shape=jax.ShapeDtypeStruct((L, B, W), F32),
        compiler_params=_cparams(("parallel", "parallel")),
        name="ada_mod",
    )(c, ada_w, ada_b.reshape(L, 1, W))


def _inproj_kernel(x_ref, g_ref, sh_ref, sc_ref, wg_ref, wh_ref, wm_ref, wp_ref,
                   og_ref, oh_ref, om_ref, op_ref):
    x = x_ref[...]
    ms = jnp.mean(x * x, axis=-1, keepdims=True)
    h = x * lax.rsqrt(ms + NORM_EPS) * g_ref[...]
    h = h * (1.0 + sc_ref[0]) + sh_ref[0]
    hb = h.astype(BF16)
    cw = 512
    for w_ref, o_ref in ((wg_ref, og_ref), (wh_ref, oh_ref), (wm_ref, om_ref), (wp_ref, op_ref)):
        width = w_ref.shape[1]
        step = cw if width % cw == 0 else width
        for j in range(0, width, step):
            o_ref[:, j:j + step] = _dot(hb, w_ref[:, j:j + step]).astype(o_ref.dtype)


def _inproj(x2, g, mod3, l, wg, wh, wm, wp, B, S, tr):
    N, D = x2.shape
    nb = S // tr
    base = l * B * 6

    def modspec(j):
        return pl.BlockSpec((1, 1, D), lambda i: (base + (i // nb) * 6 + j, 0, 0))

    def wspec(w):
        return pl.BlockSpec(w.shape, lambda i: (0, 0), pipeline_mode=pl.Buffered(1))

    outs = (
        jax.ShapeDtypeStruct((N, wg.shape[1]), BF16),
        jax.ShapeDtypeStruct((N, wh.shape[1]), F32),
        jax.ShapeDtypeStruct((N, wm.shape[1]), F32),
        jax.ShapeDtypeStruct((N, wp.shape[1]), F32),
    )
    return pl.pallas_call(
        _inproj_kernel,
        grid=(N // tr,),
        in_specs=[
            pl.BlockSpec((tr, D), lambda i: (i, 0)),
            pl.BlockSpec((1, D), lambda i: (0, 0)),
            modspec(0), modspec(1),
            wspec(wg), wspec(wh), wspec(wm), wspec(wp),
        ],
        out_specs=tuple(pl.BlockSpec((tr, o.shape[1]), lambda i: (i, 0)) for o in outs),
        out_shape=outs,
        compiler_params=_cparams(("parallel",)),
        name="in_proj",
    )(x2, g, mod3, mod3, wg, wh, wm, wp)


def _hgrn_kernel(layer, n_chunks, q_ref, f_ref, i_ref, g_ref, lb_ref, on_ref, o_ref,
                 st_ref, stb_ref, oi_ref):
    C, DH, H = HGRN_CHUNK, HGRN_DIM, HGRN_HEADS

    @pl.when(pl.program_id(1) == 0)
    def _():
        st_ref[...] = jnp.zeros_like(st_ref)
        stb_ref[...] = jnp.zeros_like(stb_ref)

    lbr = lb_ref[...]
    e = jnp.exp(lbr - jnp.max(lbr, axis=0, keepdims=True))
    p = e / jnp.sum(e, axis=0, keepdims=True)
    lb = jnp.zeros((1, lbr.shape[1]), F32)
    for j in range(1, layer + 1):
        lb = lb + p[j:j + 1]
    log_lb = jnp.log(lb)
    log_1mlb = jnp.log1p(-lb)
    one_m_lb = 1.0 - lb
    onorm = jnp.tile(on_ref[...], (1, H))

    row_c = lax.broadcasted_iota(jnp.int32, (C, C), 0)
    col_c = lax.broadcasted_iota(jnp.int32, (C, C), 1)
    tri = (col_c <= row_c).astype(BF16)
    rs = lax.broadcasted_iota(jnp.int32, (H * C, H * C), 0)
    cs = lax.broadcasted_iota(jnp.int32, (H * C, H * C), 1)
    same_head_causal = jnp.logical_and(cs <= rs, cs >= rs - rs % C)
    row_1 = lax.broadcasted_iota(jnp.int32, (C, 1), 0)
    heads = [slice(h * DH, (h + 1) * DH) for h in range(H)]

    def stack(x):
        return jnp.concatenate([x[:, hs] for hs in heads], axis=0)

    def unstack(x):
        return jnp.concatenate([x[h * C:(h + 1) * C] for h in range(H)], axis=1)

    def prepare(r, sl):
        q = q_ref[r, sl, :]
        z = f_ref[r, sl, :]
        v = i_ref[r, sl, :]
        ez = jnp.exp(-jnp.abs(z))
        log_sig = jnp.minimum(z, 0.0) - jnp.log1p(ez)
        if layer == 0:
            lf = log_sig
            kk = jnp.where(z >= 0.0, ez, 1.0) / (1.0 + ez)
        else:
            a = log_1mlb + log_sig
            m = jnp.maximum(a, log_lb)
            lf = m + jnp.log(jnp.exp(a - m) + jnp.exp(log_lb - m))
            kk = one_m_lb * jnp.where(z >= 0.0, ez, 1.0) / (1.0 + ez)
        l1, l2, l3 = _split3(lf)
        b = (_dot(tri, l1) + _dot(tri, l2) + _dot(tri, l3)) * LOG2E
        qs = (q * jnp.exp2(b)).astype(BF16)
        o_inter = jnp.concatenate(
            [_dot_nt(qs[:, hs], stb_ref[r * H + h]) for h, hs in enumerate(heads)], axis=1)
        return q, v, kk, b, qs, v.astype(BF16), o_inter

    def intra_factored(r, prepared):
        _, _, kk, b, qs, vb, _ = prepared
        ks = (kk * jnp.exp2(-b)).astype(BF16)
        att = _dot_nt(stack(qs), stack(ks))
        att = jnp.where(same_head_causal, att, 0.0)
        oi_ref[r] = unstack(_dot(att.astype(BF16), stack(vb)))

    def intra_termwise(r, prepared):
        q, v, kk, b, _, _, _ = prepared
        for hs in heads:
            bh, qh, kh, vh = b[:, hs], q[:, hs], kk[:, hs], v[:, hs]
            acc = jnp.zeros((C, DH), F32)
            for s in range(C):
                es = jnp.exp2(jnp.minimum(bh - bh[s:s + 1], 0.0))
                col = jnp.sum(qh * es * kh[s:s + 1], axis=-1, keepdims=True)
                acc = acc + jnp.where(row_1 >= s, col, 0.0) * vh[s:s + 1]
            oi_ref[r, :, hs] = acc

    def finish(r, sl, prepared):
        _, _, kk, b, _, vb, o_inter = prepared
        bl = b[C - 1:C]
        o = oi_ref[r] + o_inter
        kd = (kk * jnp.exp2(bl - b)).astype(BF16)
        decay = jnp.exp2(bl)
        normed = []
        for h, hs in enumerate(heads):
            new = st_ref[r * H + h] * decay[:, hs] + _dot_tn(vb[:, hs], kd[:, hs])
            st_ref[r * H + h] = new
            stb_ref[r * H + h] = new.astype(BF16)
            oh = o[:, hs]
            normed.append(oh * lax.rsqrt(jnp.mean(oh * oh, axis=-1, keepdims=True) + NORM_EPS))
        g = g_ref[r, sl, :]
        y = jnp.concatenate(normed, axis=1) * onorm * (g * _sigmoid(g))
        o_ref[r, sl, :] = y.astype(o_ref.dtype)

    def chunk(ci, carry):
        sl = pl.ds(pl.multiple_of(ci * C, C), C)
        rows = range(q_ref.shape[0])
        prepared = [prepare(r, sl) for r in rows]
        span = -prepared[0][3][C - 1:C]
        for p in prepared[1:]:
            span = jnp.maximum(span, -p[3][C - 1:C])
        factor_ok = jnp.max(span) < FACTOR_LIMIT

        @pl.when(factor_ok)
        def _():
            for r in rows:
                intra_factored(r, prepared[r])

        @pl.when(jnp.logical_not(factor_ok))
        def _():
            for r in rows:
                intra_termwise(r, prepared[r])

        for r in rows:
            finish(r, sl, prepared[r])
        return carry

    lax.fori_loop(0, n_chunks, chunk, 0)


def _hgrn(ph, hgrn_lb, onorm_g, layer, B, S, tc):
    L = hgrn_lb.shape[0]
    W = HGRN_HEADS * HGRN_DIM

    nb = math.gcd(B, HGRN_ROWS)

    def spec(off):
        return pl.BlockSpec((nb, tc, W), lambda b, t: (b, t, off))

    return pl.pallas_call(
        functools.partial(_hgrn_kernel, layer, tc // HGRN_CHUNK),
        grid=(B // nb, S // tc),
        in_specs=[spec(0), spec(1), spec(2), spec(3),
                  pl.BlockSpec((L, W), lambda b, t: (0, 0)),
                  pl.BlockSpec((1, HGRN_DIM), lambda b, t: (0, 0))],
        out_specs=pl.BlockSpec((nb, tc, W), lambda b, t: (b, t, 0)),
        out_shape=jax.ShapeDtypeStruct((B, S, W), BF16),
        scratch_shapes=[pltpu.VMEM((nb * HGRN_HEADS, HGRN_DIM, HGRN_DIM), F32),
                        pltpu.VMEM((nb * HGRN_HEADS, HGRN_DIM, HGRN_DIM), BF16),
                        pltpu.VMEM((nb, HGRN_CHUNK, W), F32)],
        compiler_params=_cparams(("parallel", "arbitrary")),
        name="hgrn2",
    )(ph, ph, ph, ph, hgrn_lb, onorm_g)


def _rope_table_kernel(pos_ref, fr_ref, cos_ref, sin_ref):
    ang = pos_ref[...] * fr_ref[...]
    lane = lax.broadcasted_iota(jnp.int32, ang.shape, 1)
    sn = jnp.sin(ang)
    cos_ref[...] = jnp.cos(ang)
    sin_ref[...] = jnp.where(lane < MLA_NOPE + MLA_ROPE // 2, -sn, sn)


def _rope_tables(posf, freq_row, tr):
    N = posf.shape[0]
    out = jax.ShapeDtypeStruct((N, LANES), F32)
    return pl.pallas_call(
        _rope_table_kernel,
        grid=(N // tr,),
        in_specs=[pl.BlockSpec((tr, 1), lambda i: (i, 0)), pl.BlockSpec((1, LANES), lambda i: (0, 0))],
        out_specs=(pl.BlockSpec((tr, LANES), lambda i: (i, 0)), pl.BlockSpec((tr, LANES), lambda i: (i, 0))),
        out_shape=(out, out),
        compiler_params=_cparams(("parallel",)),
        name="rope_tables",
    )(posf, freq_row)


def _mla_prep_kernel(pm_ref, cos_ref, sin_ref, qg_ref, kvg_ref, wq_ref, wqr_ref, wkv_ref,
                     qn_ref, qnr_ref, kn_ref, knr_ref, q_ref, k_ref, v_ref):
    pm = pm_ref[...]
    o = 0
    ql = pm[:, o:o + MLA_Q_RANK]; o += MLA_Q_RANK
    kvl = pm[:, o:o + MLA_KV_RANK]; o += MLA_KV_RANK
    kr = pm[:, o:o + LANES]; o += LANES
    krr = pm[:, o:o + LANES]

    def rms(x, g):
        return x * lax.rsqrt(jnp.mean(x * x, axis=-1, keepdims=True) + NORM_EPS) * g

    qlb = rms(ql, qg_ref[...]).astype(BF16)
    qa = _dot(qlb, wq_ref[...])
    qr = _dot(qlb, wqr_ref[...])
    kva = _dot(rms(kvl, kvg_ref[...]).astype(BF16), wkv_ref[...])
    v_ref[...] = kva[:, MLA_HEADS * LANES:].astype(v_ref.dtype)

    cs = cos_ref[...]
    sn = sin_ref[...]
    scale = LOG2E / math.sqrt(MLA_QK)
    q_cos = qn_ref[...] * cs * scale
    q_sin = qnr_ref[...] * sn * scale
    k_cos = kn_ref[...] * cs
    k_sin = knr_ref[...] * sn
    k_rot = krr * k_sin

    def inv_rms(xh):
        return lax.rsqrt(jnp.sum(xh * xh, axis=-1, keepdims=True) * (1.0 / MLA_QK) + NORM_EPS)

    for h in range(MLA_HEADS):
        hs = slice(h * LANES, (h + 1) * LANES)
        qh = qa[:, hs]
        q_ref[0, h] = ((qh * q_cos + qr[:, hs] * q_sin) * inv_rms(qh)).astype(q_ref.dtype)
        kh = kva[:, hs] + kr
        k_ref[0, h] = ((kh * k_cos + k_rot) * inv_rms(kh)).astype(k_ref.dtype)


def _mla_prep(pm, cos_t, sin_t, qlat_g, kvlat_g, wq, wqr, wkv, qn, qnr, kn, knr, B, S, tr):
    N = pm.shape[0]
    nb = S // tr
    H = MLA_HEADS
    full = lambda a: pl.BlockSpec(a.shape, lambda i: (0,) * a.ndim)
    return pl.pallas_call(
        _mla_prep_kernel,
        grid=(N // tr,),
        in_specs=[pl.BlockSpec((tr, pm.shape[1]), lambda i: (i, 0)),
                  pl.BlockSpec((tr, LANES), lambda i: (i, 0)),
                  pl.BlockSpec((tr, LANES), lambda i: (i, 0)),
                  full(qlat_g), full(kvlat_g), full(wq), full(wqr), full(wkv),
                  full(qn), full(qnr), full(kn), full(knr)],
        out_specs=(pl.BlockSpec((1, H, tr, LANES), lambda i: (i // nb, 0, i % nb, 0)),
                   pl.BlockSpec((1, H, tr, LANES), lambda i: (i // nb, 0, i % nb, 0)),
                   pl.BlockSpec((tr, H * MLA_V), lambda i: (i, 0))),
        out_shape=(jax.ShapeDtypeStruct((B, H, S, LANES), BF16),
                   jax.ShapeDtypeStruct((B, H, S, LANES), BF16),
                   jax.ShapeDtypeStruct((N, H * MLA_V), BF16)),
        compiler_params=_cparams(("parallel",)),
        name="mla_prep",
    )(pm, cos_t, sin_t, qlat_g, kvlat_g, wq, wqr, wkv, qn, qnr, kn, knr)


def _attn_kernel(q_ref, k_ref, v_ref, o_ref, m_ref, l_ref, acc_ref):
    qi = pl.program_id(2)
    T = q_ref.shape[2]

    m_ref[...] = jnp.full_like(m_ref, NEG_BIG)
    l_ref[...] = jnp.zeros_like(l_ref)
    acc_ref[...] = jnp.zeros_like(acc_ref)

    n_heads = q_ref.shape[1]

    def step(ki, masked, nk=None):
        nk = nk or T
        rows = pl.ds(pl.multiple_of(ki * T, T), nk)
        lane = lax.broadcasted_iota(jnp.int32, (nk, LANES), 1)
        if masked:
            row = lax.broadcasted_iota(jnp.int32, (T, T), 0)
            col = lax.broadcasted_iota(jnp.int32, (T, T), 1)
            keep = col <= row
        for h in range(n_heads):
            vblk = v_ref[0, rows, (h // 2) * LANES:(h // 2 + 1) * LANES]
            s = _dot_nt(q_ref[0, h], k_ref[0, h, rows, :])
            if masked:
                s = jnp.where(keep, s, NEG_BIG)
            m_old = m_ref[h]
            m_new = jnp.maximum(m_old, jnp.max(s, axis=-1, keepdims=True))
            alpha = jnp.exp2(m_old - m_new)
            p = jnp.exp2(s - jnp.tile(m_new, (1, nk // LANES)))
            l_ref[h] = alpha * l_ref[h] + jnp.sum(p, axis=-1, keepdims=True)
            vh = jnp.where((lane < MLA_V) if h % 2 == 0 else (lane >= MLA_V), vblk, jnp.zeros_like(vblk))
            acc_ref[h] = alpha * acc_ref[h] + _dot(p.astype(BF16), vh)
            m_ref[h] = m_new

    def below_diagonal(kp, c):
        step(2 * kp, False, 2 * T)
        return c

    lax.fori_loop(0, qi >> 1, below_diagonal, 0)

    @pl.when((qi & 1) == 1)
    def _():
        step(qi - 1, False)

    step(qi, True)
    for pair in range(n_heads // 2):
        h0, h1 = 2 * pair, 2 * pair + 1
        o_ref[0, :, pair * LANES:(pair + 1) * LANES] = (
            acc_ref[h0] / l_ref[h0] + acc_ref[h1] / l_ref[h1]).astype(o_ref.dtype)


def _attention(q, k, v3, B, S, T):
    H = MLA_HEADS
    hs = ATTN_HEADS
    w = hs // 2 * LANES
    n = S // T
    return pl.pallas_call(
        _attn_kernel,
        grid=(B, H // hs, n),
        in_specs=[pl.BlockSpec((1, hs, T, LANES), lambda b, hg, qi: (b, hg, qi, 0)),
                  pl.BlockSpec((1, hs, S, LANES), lambda b, hg, qi: (b, hg, 0, 0)),
                  pl.BlockSpec((1, S, w), lambda b, hg, qi: (b, 0, hg))],
        out_specs=pl.BlockSpec((1, T, w), lambda b, hg, qi: (b, qi, hg)),
        out_shape=jax.ShapeDtypeStruct((B, S, H * MLA_V), BF16),
        scratch_shapes=[pltpu.VMEM((hs, T, LANES), F32), pltpu.VMEM((hs, T, LANES), F32),
                        pltpu.VMEM((hs, T, LANES), F32)],
        compiler_params=_cparams(("parallel", "parallel", "arbitrary")),
        name="mla_attention",
    )(q, k, v3)


def _pool_kernel(u_ref, w_ref, sc_ref, o_ref):
    S = u_ref.shape[1]
    row = lax.broadcasted_iota(jnp.int32, (S, LANES), 0)
    t1 = (row + 1).astype(F32)
    for gi, win in enumerate(POOL_WINDOWS):
        u = u_ref[0, :, gi * LANES:(gi + 1) * LANES]
        s = u
        k = 1
        while k < win:
            s = s + jnp.where(row >= k, pltpu.roll(s, k, 0), 0.0)
            k *= 2
        pooled = s / jnp.minimum(t1, float(win))
        mixed = _dot((pooled - u).astype(BF16), w_ref[gi].astype(BF16))
        o_ref[0, :, gi * LANES:(gi + 1) * LANES] = (
            mixed * sc_ref[:, gi * LANES:(gi + 1) * LANES]).astype(o_ref.dtype)


def _pool(pp3, w_pool, scale):
    B, S, W = pp3.shape
    return pl.pallas_call(
        _pool_kernel,
        grid=(B,),
        in_specs=[pl.BlockSpec((1, S, W), lambda b: (b, 0, 0)),
                  pl.BlockSpec(w_pool.shape, lambda b: (0, 0, 0)),
                  pl.BlockSpec((1, W), lambda b: (0, 0))],
        out_specs=pl.BlockSpec((1, S, W), lambda b: (b, 0, 0)),
        out_shape=jax.ShapeDtypeStruct((B, S, W), BF16),
        compiler_params=_cparams(("parallel",)),
        name="pool_mixer",
    )(pp3, w_pool, scale)


def _merge_router_kernel(x_ref, gt_ref, ya_ref, yb_ref, yc_ref, wa_ref, wb_ref, wc_ref, wo_ref, g1_ref,
                         g_ref, sh_ref, sc_ref, wr_ref, br_ref, xo_ref, hb_ref, ls_ref, wt_ref, cnt_ref):
    D = x_ref.shape[1]
    merged = None
    for j, (y_ref, w_ref) in enumerate(((ya_ref, wa_ref), (yb_ref, wb_ref), (yc_ref, wc_ref))):
        y = _dot(y_ref[...], w_ref[...])
        gate = _sigmoid(gt_ref[:, j * D:(j + 1) * D].astype(F32))
        merged = gate * y if merged is None else merged + gate * y
    x = x_ref[...] + g1_ref[0] * _dot(merged.astype(BF16), wo_ref[...])
    xo_ref[...] = x
    _route(x, g_ref, sh_ref, sc_ref, wr_ref, br_ref, hb_ref, ls_ref, wt_ref, cnt_ref)


def _route(x, g_ref, sh_ref, sc_ref, wr_ref, br_ref, hb_ref, ls_ref, wt_ref, cnt_ref):
    tr = x.shape[0]

    ms = jnp.mean(x * x, axis=-1, keepdims=True)
    h = x * lax.rsqrt(ms + NORM_EPS) * g_ref[...]
    h = h * (1.0 + sc_ref[0]) + sh_ref[0]
    hb_ref[...] = h.astype(hb_ref.dtype)
    h1, h2, _ = _split3(h)
    w1, w2, _ = _split3(wr_ref[...])
    logits = _dot(h1, w1) + (_dot(h1, w2) + _dot(h2, w1)) + br_ref[...]
    lane = lax.broadcasted_iota(jnp.int32, logits.shape, 1)
    work = logits
    vals, ids = [], []
    for _k in range(TOP_K):
        m = jnp.max(work, axis=-1, keepdims=True)
        sel = jnp.min(jnp.where(work == m, lane, LANES), axis=-1, keepdims=True)
        vals.append(m)
        ids.append(sel)
        work = jnp.where(lane == sel, -jnp.inf, work)
    es = [jnp.exp(v - vals[0]) for v in vals]
    den = es[0] + es[1] + es[2] + es[3]
    onehot = jnp.zeros(logits.shape, F32)
    wt_out = jnp.zeros(logits.shape, F32)
    for k in range(TOP_K):
        onehot = onehot + (lane == ids[k]).astype(F32)
        wt_out = jnp.where(lane == k, es[k] / den, wt_out)
    wt_ref[...] = wt_out

    r = lax.broadcasted_iota(jnp.int32, (tr, tr), 0)
    c = lax.broadcasted_iota(jnp.int32, (tr, tr), 1)
    strict = (c < r).astype(BF16)
    before = _dot(strict, onehot.astype(BF16))
    cnt = jnp.sum(onehot, axis=0, keepdims=True)
    cnt_ref[0] = jnp.broadcast_to(cnt, cnt_ref.shape[1:])
    units = jnp.floor((cnt + (SEG_ALIGN - 1.0)) * (1.0 / SEG_ALIGN))
    er = lax.broadcasted_iota(jnp.int32, (LANES, LANES), 0)
    ec = lax.broadcasted_iota(jnp.int32, (LANES, LANES), 1)
    upper = (er < ec).astype(BF16)
    seg_start = _dot(jnp.broadcast_to(units, (8, LANES)).astype(BF16), upper)[0:1] * float(SEG_ALIGN)
    pos = before + seg_start
    ls_out = jnp.full(logits.shape, -1.0, F32)
    for k in range(TOP_K):
        lk = jnp.sum(jnp.where(lane == ids[k], pos, 0.0), axis=-1, keepdims=True)
        ls_out = jnp.where(lane == k, lk, ls_out)
    ls_ref[...] = ls_out.astype(jnp.int32)


def _local_rows(tb):
    return TOP_K * tb + N_EXPERTS * SEG_ALIGN


def _sort_chunk(rl):
    return SORT_CHUNK if rl % SORT_CHUNK == 0 else SORT_CHUNK // 2


def _merge_router(x2, gates, ya, yb, yc, wa, wb, wc, wo, g, mod3, l, wr_p, br_p, B, S, tr):
    N, D = x2.shape
    nb = S // tr
    base = l * B * 6
    full = lambda a: pl.BlockSpec(a.shape, lambda i: (0,) * a.ndim)
    rows = lambda w: pl.BlockSpec((tr, w), lambda i: (i, 0))

    def modspec(j):
        return pl.BlockSpec((1, 1, D), lambda i: (base + (i // nb) * 6 + j, 0, 0))

    return pl.pallas_call(
        _merge_router_kernel,
        grid=(N // tr,),
        in_specs=[rows(D), rows(gates.shape[1]), rows(ya.shape[1]), rows(yb.shape[1]), rows(yc.shape[1]),
                  full(wa), full(wb), full(wc), full(wo), modspec(2),
                  full(g), modspec(3), modspec(4), full(wr_p), full(br_p)],
        out_specs=(rows(D), rows(D), rows(LANES), rows(LANES),
                   pl.BlockSpec((1, 8, LANES), lambda i: (i, 0, 0))),
        out_shape=(jax.ShapeDtypeStruct((N, D), F32),
                   jax.ShapeDtypeStruct((N, D), BF16),
                   jax.ShapeDtypeStruct((N, LANES), jnp.int32),
                   jax.ShapeDtypeStruct((N, LANES), F32),
                   jax.ShapeDtypeStruct((N // tr, 8, LANES), F32)),
        compiler_params=_cparams(("parallel",)),
        name="merge_router",
    )(x2, gates, ya, yb, yc, wa, wb, wc, wo, mod3, g, mod3, mod3, wr_p, br_p)


def _segment_copies(j, e, src_ref, dst_ref, len_ref, local_ref, global_ref, sem, to_global, n_bits):
    base = j * N_EXPERTS + e
    units = len_ref[base]
    s0 = src_ref[base]
    d0 = dst_ref[base]
    copies = []
    for bit in range(n_bits):
        rows = SEG_ALIGN << bit
        done = ((units >> (bit + 1)) << (bit + 1)) * SEG_ALIGN
        loc = local_ref.at[pl.ds(pl.multiple_of(s0 + done, SEG_ALIGN), rows)]
        glo = global_ref.at[pl.ds(pl.multiple_of(d0 + done, SEG_ALIGN), rows)]
        cp = pltpu.make_async_copy(loc, glo, sem) if to_global else pltpu.make_async_copy(glo, loc, sem)
        copies.append((((units >> bit) & 1) == 1, cp))
    return copies


def _each_segment_copy(j, src_ref, dst_ref, len_ref, local_ref, global_ref, sem, to_global, n_bits, action):
    def per_expert(e, c):
        for pred, cp in _segment_copies(j, e, src_ref, dst_ref, len_ref, local_ref, global_ref,
                                        sem, to_global, n_bits):
            @pl.when(pred)
            def _():
                action(cp)
        return c
    lax.fori_loop(0, N_EXPERTS, per_expert, 0)


def _sort_place_kernel(n_bits, src_ref, dst_ref, len_ref, zflag_ref, hb_ref, ls_ref, xs_hbm,
                       zbuf, xl_buf, zsem, sem):
    j = pl.program_id(0)
    last = pl.num_programs(0) - 1
    tr = hb_ref.shape[0]
    tm = zbuf.shape[0]
    rl = xl_buf.shape[1]
    n_tiles = xs_hbm.shape[0] // tm
    slot = j % 2

    def segments(block, buf_slot, action):
        _each_segment_copy(block, src_ref, dst_ref, len_ref, xl_buf.at[buf_slot], xs_hbm,
                           sem.at[buf_slot], True, n_bits, action)

    def zero_fill(action):
        def per_tile(t, c):
            @pl.when(zflag_ref[t] > 0)
            def _():
                action(pltpu.make_async_copy(zbuf, xs_hbm.at[pl.ds(pl.multiple_of(t * tm, tm), tm)], zsem))
            return c
        lax.fori_loop(0, n_tiles, per_tile, 0)

    @pl.when(j == 0)
    def _():
        zbuf[...] = jnp.zeros_like(zbuf)
        zero_fill(lambda cp: cp.start())

    @pl.when(j >= 2)
    def _():
        segments(j - 2, slot, lambda cp: cp.wait())

    ls_t = jnp.transpose(ls_ref[...].astype(F32))
    targets = [ls_t[k:k + 1, :].astype(jnp.int32) for k in range(TOP_K)]
    hb = hb_ref[...]
    ch = _sort_chunk(rl)
    for cidx in range(rl // ch):
        ri = lax.broadcasted_iota(jnp.int32, (ch, tr), 0) + cidx * ch
        hit = ri == targets[0]
        for k in range(1, TOP_K):
            hit = jnp.logical_or(hit, ri == targets[k])
        perm = jnp.where(hit, 1.0, 0.0).astype(BF16)
        xl_buf[slot, cidx * ch:(cidx + 1) * ch, :] = _dot(perm, hb).astype(xl_buf.dtype)

    @pl.when(j == 0)
    def _():
        zero_fill(lambda cp: cp.wait())

    segments(j, slot, lambda cp: cp.start())

    @pl.when(j == last)
    def _():
        @pl.when(j >= 1)
        def _():
            segments(j - 1, 1 - slot, lambda cp: cp.wait())
        segments(j, slot, lambda cp: cp.wait())


def _sort_place(seg_src, seg_dst, seg_len, zero_flags, hb, ls, n_rows, tr, tm, n_bits):
    N, D = hb.shape
    rl = _local_rows(tr)
    grid_spec = pltpu.PrefetchScalarGridSpec(
        num_scalar_prefetch=4,
        grid=(N // tr,),
        in_specs=[pl.BlockSpec((tr, D), lambda j, a, b, c, d: (j, 0)),
                  pl.BlockSpec((tr, LANES), lambda j, a, b, c, d: (j, 0))],
        out_specs=pl.BlockSpec(memory_space=pl.ANY),
        scratch_shapes=[pltpu.VMEM((tm, D), hb.dtype), pltpu.VMEM((2, rl, D), hb.dtype),
                        pltpu.SemaphoreType.DMA(()), pltpu.SemaphoreType.DMA((2,))],
    )
    return pl.pallas_call(
        functools.partial(_sort_place_kernel, n_bits),
        grid_spec=grid_spec,
        out_shape=jax.ShapeDtypeStruct((n_rows, D), hb.dtype),
        compiler_params=_cparams(("arbitrary",)),
        name="moe_sort_place",
    )(seg_src, seg_dst, seg_len, zero_flags, hb, ls)


def _expert_kernel(first_expert, te_ref, nxt_ref, rows_ref, nu_ref, x_ref, w1_hbm, b1_ref, w2_hbm, b2_ref,
                   o_ref, w1f, w2f, w1b, w2b, sem):
    t = pl.program_id(0)
    tm = x_ref.shape[0]
    F = w2b.shape[0]
    rows = rows_ref[t]

    def weight_copies(e):
        return (pltpu.make_async_copy(w1_hbm.at[first_expert + e], w1f, sem.at[0]),
                pltpu.make_async_copy(w2_hbm.at[first_expert + e], w2f, sem.at[1]))

    @pl.when(t == 0)
    def _():
        for cp in weight_copies(te_ref[0]):
            cp.start()

    @pl.when(jnp.logical_and(rows > 0, jnp.logical_or(t == 0, te_ref[t] != te_ref[jnp.maximum(t - 1, 0)])))
    def _():
        for cp in weight_copies(te_ref[t]):
            cp.wait()
        w1b[...] = w1f[...].astype(BF16)
        w2b[...] = w2f[...].astype(BF16)

        @pl.when(nxt_ref[t] >= 0)
        def _():
            for cp in weight_copies(nxt_ref[t]):
                cp.start()

    def mlp(m):
        a = _dot(x_ref[:m, :], w1b[...]) + b1_ref[0]
        glu = jnp.minimum(a[:, :F], SWIGLU_LIMIT)
        lin = jnp.clip(a[:, F:], -SWIGLU_LIMIT, SWIGLU_LIMIT)
        act = glu * _sigmoid(SWIGLU_ALPHA * glu) * (lin + 1.0)
        o_ref[:m, :] = (_dot(act.astype(BF16), w2b[...]) + b2_ref[0]).astype(o_ref.dtype)
        if m < tm:
            o_ref[m:, :] = jnp.zeros((tm - m, o_ref.shape[1]), o_ref.dtype)

    quarter = tm // EXPERT_ROW_STEPS
    for q in range(1, EXPERT_ROW_STEPS + 1):
        @pl.when(jnp.logical_and(rows > (q - 1) * quarter, rows <= q * quarter))
        def _():
            mlp(q * quarter)

    @pl.when(rows == 0)
    def _():
        o_ref[...] = jnp.zeros_like(o_ref)


def _experts(tile_expert, next_expert, tile_rows, n_used, xs, w1_all, b1_all, w2_all, b2_all, l, tm):
    n_tiles = tile_expert.shape[0]
    D = xs.shape[1]
    L, E, _, F2 = w1_all.shape
    F = w2_all.shape[2]
    w1 = w1_all.reshape(L * E, D, F2)
    w2 = w2_all.reshape(L * E, F, D)
    b1 = b1_all.reshape(L * E, F2)
    b2 = b2_all.reshape(L * E, D)
    E0 = l * E
    E = L * E
    grid_spec = pltpu.PrefetchScalarGridSpec(
        num_scalar_prefetch=4,
        grid=(n_tiles,),
        in_specs=[pl.BlockSpec((tm, D), lambda t, te, nx, tr, nu: (jnp.minimum(t, nu[0] - 1), 0)),
                  pl.BlockSpec(memory_space=pl.ANY),
                  pl.BlockSpec((1, 1, F2), lambda t, te, nx, tr, nu: (E0 + te[t], 0, 0)),
                  pl.BlockSpec(memory_space=pl.ANY),
                  pl.BlockSpec((1, 1, D), lambda t, te, nx, tr, nu: (E0 + te[t], 0, 0))],
        out_specs=pl.BlockSpec((tm, D), lambda t, te, nx, tr, nu: (t, 0)),
        scratch_shapes=[pltpu.VMEM((D, F2), F32), pltpu.VMEM((F, D), F32),
                        pltpu.VMEM((D, F2), BF16), pltpu.VMEM((F, D), BF16),
                        pltpu.SemaphoreType.DMA((2,))],
    )
    return pl.pallas_call(
        functools.partial(_expert_kernel, E0),
        grid_spec=grid_spec,
        out_shape=jax.ShapeDtypeStruct((n_tiles * tm, D), BF16),
        compiler_params=_cparams(("arbitrary",)),
        name="expert_mlp",
    )(tile_expert, next_expert, tile_rows, n_used, xs, w1, b1.reshape(E, 1, F2), w2, b2.reshape(E, 1, D))


def _combine_kernel(n_bits, src_ref, dst_ref, len_ref, ys_hbm, x_ref, ls_ref, wt_ref, g2_ref, o_ref,
                    ybuf, sem):
    j = pl.program_id(0)
    tn = x_ref.shape[0]
    rl = ybuf.shape[1]
    slot = j % 2

    def fetch(block, buf_slot, action):
        _each_segment_copy(block, src_ref, dst_ref, len_ref, ybuf.at[buf_slot], ys_hbm,
                           sem.at[buf_slot], False, n_bits, action)

    def start_fetch(block, buf_slot):
        ybuf[buf_slot] = jnp.zeros((rl, ybuf.shape[2]), ybuf.dtype)
        fetch(block, buf_slot, lambda cp: cp.start())

    @pl.when(j == 0)
    def _():
        start_fetch(j, slot)

    @pl.when(j + 1 < pl.num_programs(0))
    def _():
        start_fetch(j + 1, 1 - slot)

    fetch(j, slot, lambda cp: cp.wait())
    ls = ls_ref[...]
    wt = wt_ref[...]
    ch = _sort_chunk(rl)
    y = jnp.zeros(x_ref.shape, F32)
    for cidx in range(rl // ch):
        li = lax.broadcasted_iota(jnp.int32, (tn, ch), 1) + cidx * ch
        pw = jnp.zeros((tn, ch), F32)
        for k in range(TOP_K):
            pw = jnp.where(li == ls[:, k:k + 1], wt[:, k:k + 1], pw)
        y = y + _dot(pw.astype(BF16), ybuf[slot, cidx * ch:(cidx + 1) * ch, :])
    o_ref[...] = x_ref[...] + g2_ref[0] * y


def _combine(seg_src, seg_dst, seg_len, ys, x2, ls, wts, mod3, l, B, S, tn, n_bits):
    N, D = x2.shape
    nb = S // tn
    base = l * B * 6
    rl = _local_rows(tn)
    grid_spec = pltpu.PrefetchScalarGridSpec(
        num_scalar_prefetch=3,
        grid=(N // tn,),
        in_specs=[pl.BlockSpec(memory_space=pl.ANY),
                  pl.BlockSpec((tn, D), lambda i, a, b, c: (i, 0)),
                  pl.BlockSpec((tn, LANES), lambda i, a, b, c: (i, 0)),
                  pl.BlockSpec((tn, LANES), lambda i, a, b, c: (i, 0)),
                  pl.BlockSpec((1, 1, D), lambda i, a, b, c: (base + (i // nb) * 6 + 5, 0, 0))],
        out_specs=pl.BlockSpec((tn, D), lambda i, a, b, c: (i, 0)),
        scratch_shapes=[pltpu.VMEM((2, rl, D), ys.dtype), pltpu.SemaphoreType.DMA((2,))],
    )
    return pl.pallas_call(
        functools.partial(_combine_kernel, n_bits),
        grid_spec=grid_spec,
        out_shape=jax.ShapeDtypeStruct((N, D), F32),
        compiler_params=_cparams(("arbitrary",)),
        name="moe_combine",
    )(seg_src, seg_dst, seg_len, ys, x2, ls, wts, mod3)


def _tile_rows(S, want):
    t = min(S, want)
    while S % t:
        t //= 2
    return t


def _prep_in_weights(w_in_l):
    HW = HGRN_HEADS * HGRN_DIM
    o = 0
    wh = w_in_l[:, o:o + 4 * HW]; o += 4 * HW
    wql = w_in_l[:, o:o + MLA_Q_RANK]; o += MLA_Q_RANK
    wkvl = w_in_l[:, o:o + MLA_KV_RANK]; o += MLA_KV_RANK
    wkr = w_in_l[:, o:o + MLA_ROPE]; o += MLA_ROPE
    wp = w_in_l[:, o:o + HW]; o += HW
    wg = w_in_l[:, o:]
    D = w_in_l.shape[0]
    wkr_p = jnp.zeros((D, LANES), w_in_l.dtype).at[:, MLA_NOPE:MLA_NOPE + MLA_ROPE].set(wkr)
    wm = jnp.concatenate([wql, wkvl, wkr_p, _rope_partner(wkr_p)], axis=1)
    return wg.astype(BF16), wh.astype(BF16), wm.astype(BF16), wp.astype(BF16)


def _rope_partner(w):
    half = MLA_ROPE // 2
    w3 = w.reshape(w.shape[:-1] + (-1, LANES))
    first = w3[..., MLA_NOPE:MLA_NOPE + half]
    second = w3[..., MLA_NOPE + half:MLA_NOPE + MLA_ROPE]
    out = jnp.concatenate([jnp.zeros_like(w3[..., :MLA_NOPE]), second, first,
                           jnp.zeros_like(w3[..., MLA_NOPE + MLA_ROPE:])], axis=-1)
    return out.reshape(w.shape)


def _pad_heads(w, n_heads, width):
    K = w.shape[0]
    w3 = w.reshape(K, n_heads, width)
    return jnp.pad(w3, ((0, 0), (0, 0), (0, LANES - width))).reshape(K, n_heads * LANES)


def kernel(x, c, positions, ada_w, ada_b, norm1_g, norm2_g, w_in, hgrn_lb, hgrn_onorm_g, mla_qlat_g, mla_kvlat_g, w_uq, w_ukv, q_norm_g, k_norm_g, w_pool, pool_scale, w_br_a, w_br_b, w_br_c, w_out, w_router, b_router, w_exp1, b_exp1, w_exp2, b_exp2):
    B, S, D = x.shape
    L = ada_w.shape[0]
    N = B * S
    H = MLA_HEADS
    tr = _tile_rows(S, 512)
    tc = _tile_rows(S, 128)
    ta = _tile_rows(S, 512)
    tm = 512

    mod = _ada(c, ada_w, ada_b)
    mod3 = mod.reshape(L * B * 6, 1, D)

    inv_freq = 1.0 / (ROPE_THETA ** (jnp.arange(0, MLA_ROPE, 2, dtype=F32) / MLA_ROPE))
    freq_row = jnp.zeros((1, LANES), F32).at[0, MLA_NOPE:MLA_NOPE + MLA_ROPE].set(
        jnp.concatenate([inv_freq, inv_freq]))
    posf = positions.astype(F32).reshape(N, 1)
    cos_t, sin_t = _rope_tables(posf, freq_row, tr)

    max_rows = N * TOP_K + (N // tr) * N_EXPERTS * (SEG_ALIGN - 1)
    n_tiles = -(-max_rows // tm) + N_EXPERTS
    n_bits = (tr // SEG_ALIGN).bit_length()
    x2 = x.reshape(N, D)
    for l in range(L):
        wg, wh, wm, wp = _prep_in_weights(w_in[l])
        pg, ph, pm, pp = _inproj(x2, norm1_g[l].reshape(1, D), mod3, l, wg, wh, wm, wp, B, S, tr)

        ya = _hgrn(ph.reshape(B, S, -1), hgrn_lb, hgrn_onorm_g[l].reshape(1, HGRN_DIM), l, B, S, tc)

        wq_f = _pad_heads(w_uq[l], H, MLA_QK)
        wq_p = wq_f.astype(BF16)
        wqr_p = _rope_partner(wq_f).astype(BF16)
        wkv3 = w_ukv[l].reshape(MLA_KV_RANK, H, MLA_NOPE + MLA_V)
        wkv_p = jnp.concatenate(
            [_pad_heads(wkv3[:, :, :MLA_NOPE].reshape(MLA_KV_RANK, H * MLA_NOPE), H, MLA_NOPE),
             wkv3[:, :, MLA_NOPE:].reshape(MLA_KV_RANK, H * MLA_V)], axis=1).astype(BF16)
        qn_p = jnp.pad(q_norm_g[l], (0, LANES - MLA_QK)).reshape(1, LANES)
        kn_p = jnp.pad(k_norm_g[l], (0, LANES - MLA_QK)).reshape(1, LANES)
        q, k, v = _mla_prep(pm, cos_t, sin_t, mla_qlat_g[l].reshape(1, -1), mla_kvlat_g[l].reshape(1, -1),
                            wq_p, wqr_p, wkv_p, qn_p, _rope_partner(qn_p), kn_p, _rope_partner(kn_p),
                            B, S, tr)
        yb = _attention(q, k, v.reshape(B, S, H * MLA_V), B, S, ta)

        yc = _pool(pp.reshape(B, S, -1), w_pool[l], pool_scale[l].reshape(1, -1))

        wr_p = jnp.pad(w_router[l], ((0, 0), (0, LANES - N_EXPERTS)))
        br_p = jnp.concatenate([b_router[l], jnp.full((LANES - N_EXPERTS,), NEG_BIG, F32)]).reshape(1, LANES)
        x2, hb, ls, wts, cnt = _merge_router(
            x2, pg, ya.reshape(N, -1), yb.reshape(N, -1), yc.reshape(N, -1),
            w_br_a[l].astype(BF16), w_br_b[l].astype(BF16), w_br_c[l].astype(BF16), w_out[l].astype(BF16),
            norm2_g[l].reshape(1, D), mod3, l, wr_p, br_p, B, S, tr)

        seg = ((cnt[:, 0, :N_EXPERTS].astype(jnp.int32) + SEG_ALIGN - 1) // SEG_ALIGN) * SEG_ALIGN
        seg_src = jnp.cumsum(seg, axis=1) - seg
        group = jnp.sum(seg, axis=0)
        tiles_per = (group + tm - 1) // tm
        tile_end = jnp.cumsum(tiles_per)
        offsets = (tile_end - tiles_per) * tm
        seg_dst = offsets[None, :] + jnp.cumsum(seg, axis=0) - seg
        n_used = tile_end[-1:]
        tile_ids = jnp.minimum(jnp.arange(n_tiles, dtype=jnp.int32), n_used[0] - 1)
        tile_expert = jnp.minimum(
            jnp.sum((tile_end[None, :] <= tile_ids[:, None]).astype(jnp.int32), axis=1), N_EXPERTS - 1)
        all_tiles = jnp.arange(n_tiles, dtype=jnp.int32)
        is_tail = jnp.any((tile_end[None, :] - 1 == all_tiles[:, None]) & (tiles_per[None, :] > 0), axis=1)
        zero_flags = (is_tail | (all_tiles >= n_used[0])).astype(jnp.int32)
        seg_src = seg_src.reshape(-1).astype(jnp.int32)
        seg_dst = seg_dst.reshape(-1).astype(jnp.int32)
        seg_len = (seg // SEG_ALIGN).reshape(-1).astype(jnp.int32)

        mine = tile_expert[:, None] == jnp.arange(N_EXPERTS, dtype=jnp.int32)[None, :]
        first_tile = jnp.sum(jnp.where(mine, (tile_end - tiles_per)[None, :], 0), axis=1)
        group_rows = jnp.sum(jnp.where(mine, group[None, :], 0), axis=1)
        tile_rows = jnp.where(all_tiles < n_used[0],
                              jnp.clip(group_rows - (all_tiles - first_tile) * tm, 0, tm), 0)

        xs = _sort_place(seg_src, seg_dst, seg_len, zero_flags, hb, ls, n_tiles * tm, tr, tm, n_bits)
        group_end = jnp.sum(jnp.where(mine, tile_end[None, :], 0), axis=1)
        following = jnp.minimum(
            jnp.sum((tile_end[None, :] <= group_end[:, None]).astype(jnp.int32), axis=1), N_EXPERTS - 1)
        next_expert = jnp.where(group_end < n_used[0], following, -1)

        ys = _experts(tile_expert.astype(jnp.int32), next_expert.astype(jnp.int32), tile_rows.astype(jnp.int32),
                      n_used.astype(jnp.int32), xs, w_exp1, b_exp1, w_exp2, b_exp2, l, tm)
        x2 = _combine(seg_src, seg_dst, seg_len, ys, x2, ls, wts, mod3, l, B, S, tr, n_bits)
    return x2.reshape(B, S, D)
```

```python
import functools
import math

import jax
import jax.numpy as jnp
from jax import lax
from jax.experimental import pallas as pl
from jax.experimental.pallas import tpu as pltpu

F32 = jnp.float32
BF16 = jnp.bfloat16

LANES = 128
SUBLANES = 8
NORM_EPS = 1e-6
HGRN_HEADS = 4
HGRN_DIM = 128
HGRN_CHUNK = 32
HGRN_ROWS = 8
MLA_HEADS = 8
MLA_NOPE = 64
MLA_ROPE = 32
MLA_QK = MLA_NOPE + MLA_ROPE
MLA_V = 64
MLA_Q_RANK = 384
MLA_KV_RANK = 256
ROPE_THETA = 10000.0
POOL_WINDOWS = (2, 4, 8, 16)
N_EXPERTS = 32
TOP_K = 4
SWIGLU_LIMIT = 7.0
SWIGLU_ALPHA = 1.702
NEG_BIG = -1e30
LOG2E = 1.4426950408889634
FACTOR_LIMIT = 100.0
SEG_ALIGN = 16
SORT_CHUNK = 256
EXPERT_ROW_STEPS = 4
ATTN_HEADS = 4

VMEM_LIMIT = 56 * 1024 * 1024


def _cparams(sem):
    return pltpu.CompilerParams(dimension_semantics=sem, vmem_limit_bytes=VMEM_LIMIT)


def _dot(a, b):
    return jnp.dot(a, b, preferred_element_type=F32)


def _dot_nt(a, b):
    return lax.dot_general(a, b, (((1,), (1,)), ((), ())), preferred_element_type=F32)


def _dot_tn(a, b):
    return lax.dot_general(a, b, (((0,), (0,)), ((), ())), preferred_element_type=F32)


def _split3(x):
    hi = x.astype(BF16)
    r = x - hi.astype(F32)
    mid = r.astype(BF16)
    lo = (r - mid.astype(F32)).astype(BF16)
    return hi, mid, lo


def _sigmoid(x):
    return 1.0 / (1.0 + jnp.exp(-x))


def _ada_kernel(c_ref, w_ref, b_ref, o_ref):
    c = c_ref[...]
    cond = (c * _sigmoid(c)).astype(BF16)
    o_ref[0] = _dot(cond, w_ref[0].astype(BF16)) + b_ref[0]


def _ada(c, ada_w, ada_b):
    L, D, W = ada_w.shape
    B = c.shape[0]
    tn = D
    return pl.pallas_call(
        _ada_kernel,
        grid=(L, W // tn),
        in_specs=[
            pl.BlockSpec((B, D), lambda l, j: (0, 0)),
            pl.BlockSpec((1, D, tn), lambda l, j: (l, 0, j)),
            pl.BlockSpec((1, 1, tn), lambda l, j: (l, 0, j)),
        ],
        out_specs=pl.BlockSpec((1, B, tn), lambda l, j: (l, 0, j)),
        out_shape=jax.ShapeDtypeStruct((L, B, W), F32),
        compiler_params=_cparams(("parallel", "parallel")),
        name="ada_mod",
    )(c, ada_w, ada_b.reshape(L, 1, W))


def _inproj_kernel(x_ref, g_ref, sh_ref, sc_ref, wg_ref, wh_ref, wm_ref, wp_ref,
                   og_ref, oh_ref, om_ref, op_ref):
    x = x_ref[...]
    ms = jnp.mean(x * x, axis=-1, keepdims=True)
    h = x * lax.rsqrt(ms + NORM_EPS) * g_ref[...]
    h = h * (1.0 + sc_ref[0]) + sh_ref[0]
    hb = h.astype(BF16)
    cw = 512
    for w_ref, o_ref in ((wg_ref, og_ref), (wh_ref, oh_ref), (wm_ref, om_ref), (wp_ref, op_ref)):
        width = w_ref.shape[1]
        step = cw if width % cw == 0 else width
        for j in range(0, width, step):
            o_ref[:, j:j + step] = _dot(hb, w_ref[:, j:j + step]).astype(o_ref.dtype)


def _inproj(x2, g, mod3, l, wg, wh, wm, wp, B, S, tr):
    N, D = x2.shape
    nb = S // tr
    base = l * B * 6

    def modspec(j):
        return pl.BlockSpec((1, 1, D), lambda i: (base + (i // nb) * 6 + j, 0, 0))

    def wspec(w):
        return pl.BlockSpec(w.shape, lambda i: (0, 0), pipeline_mode=pl.Buffered(1))

    outs = (
        jax.ShapeDtypeStruct((N, wg.shape[1]), BF16),
        jax.ShapeDtypeStruct((N, wh.shape[1]), F32),
        jax.ShapeDtypeStruct((N, wm.shape[1]), F32),
        jax.ShapeDtypeStruct((N, wp.shape[1]), F32),
    )
    return pl.pallas_call(
        _inproj_kernel,
        grid=(N // tr,),
        in_specs=[
            pl.BlockSpec((tr, D), lambda i: (i, 0)),
            pl.BlockSpec((1, D), lambda i: (0, 0)),
            modspec(0), modspec(1),
            wspec(wg), wspec(wh), wspec(wm), wspec(wp),
        ],
        out_specs=tuple(pl.BlockSpec((tr, o.shape[1]), lambda i: (i, 0)) for o in outs),
        out_shape=outs,
        compiler_params=_cparams(("parallel",)),
        name="in_proj",
    )(x2, g, mod3, mod3, wg, wh, wm, wp)


def _hgrn_kernel(layer, n_chunks, q_ref, f_ref, i_ref, g_ref, lb_ref, on_ref, o_ref,
                 st_ref, stb_ref, oi_ref):
    C, DH, H = HGRN_CHUNK, HGRN_DIM, HGRN_HEADS

    @pl.when(pl.program_id(1) == 0)
    def _():
        st_ref[...] = jnp.zeros_like(st_ref)
        stb_ref[...] = jnp.zeros_like(stb_ref)

    lbr = lb_ref[...]
    e = jnp.exp(lbr - jnp.max(lbr, axis=0, keepdims=True))
    p = e / jnp.sum(e, axis=0, keepdims=True)
    lb = jnp.zeros((1, lbr.shape[1]), F32)
    for j in range(1, layer + 1):
        lb = lb + p[j:j + 1]
    log_lb = jnp.log(lb)
    log_1mlb = jnp.log1p(-lb)
    one_m_lb = 1.0 - lb
    onorm = jnp.tile(on_ref[...], (1, H))

    row_c = lax.broadcasted_iota(jnp.int32, (C, C), 0)
    col_c = lax.broadcasted_iota(jnp.int32, (C, C), 1)
    tri = (col_c <= row_c).astype(BF16)
    rs = lax.broadcasted_iota(jnp.int32, (H * C, H * C), 0)
    cs = lax.broadcasted_iota(jnp.int32, (H * C, H * C), 1)
    same_head_causal = jnp.logical_and(cs <= rs, cs >= rs - rs % C)
    row_1 = lax.broadcasted_iota(jnp.int32, (C, 1), 0)
    heads = [slice(h * DH, (h + 1) * DH) for h in range(H)]

    def stack(x):
        return jnp.concatenate([x[:, hs] for hs in heads], axis=0)

    def unstack(x):
        return jnp.concatenate([x[h * C:(h + 1) * C] for h in range(H)], axis=1)

    def prepare(r, sl):
        q = q_ref[r, sl, :]
        z = f_ref[r, sl, :]
        v = i_ref[r, sl, :]
        ez = jnp.exp(-jnp.abs(z))
        log_sig = jnp.minimum(z, 0.0) - jnp.log1p(ez)
        if layer == 0:
            lf = log_sig
            kk = jnp.where(z >= 0.0, ez, 1.0) / (1.0 + ez)
        else:
            a = log_1mlb + log_sig
            m = jnp.maximum(a, log_lb)
            lf = m + jnp.log(jnp.exp(a - m) + jnp.exp(log_lb - m))
            kk = one_m_lb * jnp.where(z >= 0.0, ez, 1.0) / (1.0 + ez)
        l1, l2, l3 = _split3(lf)
        b = (_dot(tri, l1) + _dot(tri, l2) + _dot(tri, l3)) * LOG2E
        qs = (q * jnp.exp2(b)).astype(BF16)
        o_inter = jnp.concatenate(
            [_dot_nt(qs[:, hs], stb_ref[r * H + h]) for h, hs in enumerate(heads)], axis=1)
        return q, v, kk, b, qs, v.astype(BF16), o_inter

    def intra_factored(r, prepared):
        _, _, kk, b, qs, vb, _ = prepared
        ks = (kk * jnp.exp2(-b)).astype(BF16)
        att = _dot_nt(stack(qs), stack(ks))
        att = jnp.where(same_head_causal, att, 0.0)
        oi_ref[r] = unstack(_dot(att.astype(BF16), stack(vb)))

    def intra_termwise(r, prepared):
        q, v, kk, b, _, _, _ = prepared
        for hs in heads:
            bh, qh, kh, vh = b[:, hs], q[:, hs], kk[:, hs], v[:, hs]
            acc = jnp.zeros((C, DH), F32)
            for s in range(C):
                es = jnp.exp2(jnp.minimum(bh - bh[s:s + 1], 0.0))
                col = jnp.sum(qh * es * kh[s:s + 1], axis=-1, keepdims=True)
                acc = acc + jnp.where(row_1 >= s, col, 0.0) * vh[s:s + 1]
            oi_ref[r, :, hs] = acc

    def finish(r, sl, prepared):
        _, _, kk, b, _, vb, o_inter = prepared
        bl = b[C - 1:C]
        o = oi_ref[r] + o_inter
        kd = (kk * jnp.exp2(bl - b)).astype(BF16)
        decay = jnp.exp2(bl)
        normed = []
        for h, hs in enumerate(heads):
            new = st_ref[r * H + h] * decay[:, hs] + _dot_tn(vb[:, hs], kd[:, hs])
            st_ref[r * H + h] = new
            stb_ref[r * H + h] = new.astype(BF16)
            oh = o[:, hs]
            normed.append(oh * lax.rsqrt(jnp.mean(oh * oh, axis=-1, keepdims=True) + NORM_EPS))
        g = g_ref[r, sl, :]
        y = jnp.concatenate(normed, axis=1) * onorm * (g * _sigmoid(g))
        o_ref[r, sl, :] = y.astype(o_ref.dtype)

    def chunk(ci, carry):
        sl = pl.ds(pl.multiple_of(ci * C, C), C)
        rows = range(q_ref.shape[0])
        prepared = [prepare(r, sl) for r in rows]
        span = -prepared[0][3][C - 1:C]
        for p in prepared[1:]:
            span = jnp.maximum(span, -p[3][C - 1:C])
        factor_ok = jnp.max(span) < FACTOR_LIMIT

        @pl.when(factor_ok)
        def _():
            for r in rows:
                intra_factored(r, prepared[r])

        @pl.when(jnp.logical_not(factor_ok))
        def _():
            for r in rows:
                intra_termwise(r, prepared[r])

        for r in rows:
            finish(r, sl, prepared[r])
        return carry

    lax.fori_loop(0, n_chunks, chunk, 0)


def _hgrn(ph, hgrn_lb, onorm_g, layer, B, S, tc):
    L = hgrn_lb.shape[0]
    W = HGRN_HEADS * HGRN_DIM

    nb = math.gcd(B, HGRN_ROWS)

    def spec(off):
        return pl.BlockSpec((nb, tc, W), lambda b, t: (b, t, off))

    return pl.pallas_call(
        functools.partial(_hgrn_kernel, layer, tc // HGRN_CHUNK),
        grid=(B // nb, S // tc),
        in_specs=[spec(0), spec(1), spec(2), spec(3),
                  pl.BlockSpec((L, W), lambda b, t: (0, 0)),
                  pl.BlockSpec((1, HGRN_DIM), lambda b, t: (0, 0))],
        out_specs=pl.BlockSpec((nb, tc, W), lambda b, t: (b, t, 0)),
        out_shape=jax.ShapeDtypeStruct((B, S, W), BF16),
        scratch_shapes=[pltpu.VMEM((nb * HGRN_HEADS, HGRN_DIM, HGRN_DIM), F32),
                        pltpu.VMEM((nb * HGRN_HEADS, HGRN_DIM, HGRN_DIM), BF16),
                        pltpu.VMEM((nb, HGRN_CHUNK, W), F32)],
        compiler_params=_cparams(("parallel", "arbitrary")),
        name="hgrn2",
    )(ph, ph, ph, ph, hgrn_lb, onorm_g)


def _rope_table_kernel(pos_ref, fr_ref, cos_ref, sin_ref):
    ang = pos_ref[...] * fr_ref[...]
    lane = lax.broadcasted_iota(jnp.int32, ang.shape, 1)
    sn = jnp.sin(ang)
    cos_ref[...] = jnp.cos(ang)
    sin_ref[...] = jnp.where(lane < MLA_NOPE + MLA_ROPE // 2, -sn, sn)


def _rope_tables(posf, freq_row, tr):
    N = posf.shape[0]
    out = jax.ShapeDtypeStruct((N, LANES), F32)
    return pl.pallas_call(
        _rope_table_kernel,
        grid=(N // tr,),
        in_specs=[pl.BlockSpec((tr, 1), lambda i: (i, 0)), pl.BlockSpec((1, LANES), lambda i: (0, 0))],
        out_specs=(pl.BlockSpec((tr, LANES), lambda i: (i, 0)), pl.BlockSpec((tr, LANES), lambda i: (i, 0))),
        out_shape=(out, out),
        compiler_params=_cparams(("parallel",)),
        name="rope_tables",
    )(posf, freq_row)


def _mla_prep_kernel(pm_ref, cos_ref, sin_ref, qg_ref, kvg_ref, wq_ref, wqr_ref, wkv_ref,
                     qn_ref, qnr_ref, kn_ref, knr_ref, q_ref, k_ref, v_ref):
    pm = pm_ref[...]
    o = 0
    ql = pm[:, o:o + MLA_Q_RANK]; o += MLA_Q_RANK
    kvl = pm[:, o:o + MLA_KV_RANK]; o += MLA_KV_RANK
    kr = pm[:, o:o + LANES]; o += LANES
    krr = pm[:, o:o + LANES]

    def rms(x, g):
        return x * lax.rsqrt(jnp.mean(x * x, axis=-1, keepdims=True) + NORM_EPS) * g

    qlb = rms(ql, qg_ref[...]).astype(BF16)
    qa = _dot(qlb, wq_ref[...])
    qr = _dot(qlb, wqr_ref[...])
    kva = _dot(rms(kvl, kvg_ref[...]).astype(BF16), wkv_ref[...])
    v_ref[...] = kva[:, MLA_HEADS * LANES:].astype(v_ref.dtype)

    cs = cos_ref[...]
    sn = sin_ref[...]
    scale = LOG2E / math.sqrt(MLA_QK)
    q_cos = qn_ref[...] * cs * scale
    q_sin = qnr_ref[...] * sn * scale
    k_cos = kn_ref[...] * cs
    k_sin = knr_ref[...] * sn
    k_rot = krr * k_sin

    def inv_rms(xh):
        return lax.rsqrt(jnp.sum(xh * xh, axis=-1, keepdims=True) * (1.0 / MLA_QK) + NORM_EPS)

    for h in range(MLA_HEADS):
        hs = slice(h * LANES, (h + 1) * LANES)
        qh = qa[:, hs]
        q_ref[0, h] = ((qh * q_cos + qr[:, hs] * q_sin) * inv_rms(qh)).astype(q_ref.dtype)
        kh = kva[:, hs] + kr
        k_ref[0, h] = ((kh * k_cos + k_rot) * inv_rms(kh)).astype(k_ref.dtype)


def _mla_prep(pm, cos_t, sin_t, qlat_g, kvlat_g, wq, wqr, wkv, qn, qnr, kn, knr, B, S, tr):
    N = pm.shape[0]
    nb = S // tr
    H = MLA_HEADS
    full = lambda a: pl.BlockSpec(a.shape, lambda i: (0,) * a.ndim)
    return pl.pallas_call(
        _mla_prep_kernel,
        grid=(N // tr,),
        in_specs=[pl.BlockSpec((tr, pm.shape[1]), lambda i: (i, 0)),
                  pl.BlockSpec((tr, LANES), lambda i: (i, 0)),
                  pl.BlockSpec((tr, LANES), lambda i: (i, 0)),
                  full(qlat_g), full(kvlat_g), full(wq), full(wqr), full(wkv),
                  full(qn), full(qnr), full(kn), full(knr)],
        out_specs=(pl.BlockSpec((1, H, tr, LANES), lambda i: (i // nb, 0, i % nb, 0)),
                   pl.BlockSpec((1, H, tr, LANES), lambda i: (i // nb, 0, i % nb, 0)),
                   pl.BlockSpec((tr, H * MLA_V), lambda i: (i, 0))),
        out_shape=(jax.ShapeDtypeStruct((B, H, S, LANES), BF16),
                   jax.ShapeDtypeStruct((B, H, S, LANES), BF16),
                   jax.ShapeDtypeStruct((N, H * MLA_V), BF16)),
        compiler_params=_cparams(("parallel",)),
        name="mla_prep",
    )(pm, cos_t, sin_t, qlat_g, kvlat_g, wq, wqr, wkv, qn, qnr, kn, knr)


def _attn_kernel(q_ref, k_ref, v_ref, o_ref, m_ref, l_ref, acc_ref):
    qi = pl.program_id(2)
    T = q_ref.shape[2]

    m_ref[...] = jnp.full_like(m_ref, NEG_BIG)
    l_ref[...] = jnp.zeros_like(l_ref)
    acc_ref[...] = jnp.zeros_like(acc_ref)

    n_heads = q_ref.shape[1]

    def step(ki, masked, nk=None):
        nk = nk or T
        rows = pl.ds(pl.multiple_of(ki * T, T), nk)
        lane = lax.broadcasted_iota(jnp.int32, (nk, LANES), 1)
        if masked:
            row = lax.broadcasted_iota(jnp.int32, (T, T), 0)
            col = lax.broadcasted_iota(jnp.int32, (T, T), 1)
            keep = col <= row
        for h in range(n_heads):
            vblk = v_ref[0, rows, (h // 2) * LANES:(h // 2 + 1) * LANES]
            s = _dot_nt(q_ref[0, h], k_ref[0, h, rows, :])
            if masked:
                s = jnp.where(keep, s, NEG_BIG)
            m_old = m_ref[h]
            m_new = jnp.maximum(m_old, jnp.max(s, axis=-1, keepdims=True))
            alpha = jnp.exp2(m_old - m_new)
            p = jnp.exp2(s - jnp.tile(m_new, (1, nk // LANES)))
            l_ref[h] = alpha * l_ref[h] + jnp.sum(p, axis=-1, keepdims=True)
            vh = jnp.where((lane < MLA_V) if h % 2 == 0 else (lane >= MLA_V), vblk, jnp.zeros_like(vblk))
            acc_ref[h] = alpha * acc_ref[h] + _dot(p.astype(BF16), vh)
            m_ref[h] = m_new

    def below_diagonal(kp, c):
        step(2 * kp, False, 2 * T)
        return c

    lax.fori_loop(0, qi >> 1, below_diagonal, 0)

    @pl.when((qi & 1) == 1)
    def _():
        step(qi - 1, False)

    step(qi, True)
    for pair in range(n_heads // 2):
        h0, h1 = 2 * pair, 2 * pair + 1
        o_ref[0, :, pair * LANES:(pair + 1) * LANES] = (
            acc_ref[h0] / l_ref[h0] + acc_ref[h1] / l_ref[h1]).astype(o_ref.dtype)


def _attention(q, k, v3, B, S, T):
    H = MLA_HEADS
    hs = ATTN_HEADS
    w = hs // 2 * LANES
    n = S // T
    return pl.pallas_call(
        _attn_kernel,
        grid=(B, H // hs, n),
        in_specs=[pl.BlockSpec((1, hs, T, LANES), lambda b, hg, qi: (b, hg, qi, 0)),
                  pl.BlockSpec((1, hs, S, LANES), lambda b, hg, qi: (b, hg, 0, 0)),
                  pl.BlockSpec((1, S, w), lambda b, hg, qi: (b, 0, hg))],
        out_specs=pl.BlockSpec((1, T, w), lambda b, hg, qi: (b, qi, hg)),
        out_shape=jax.ShapeDtypeStruct((B, S, H * MLA_V), BF16),
        scratch_shapes=[pltpu.VMEM((hs, T, LANES), F32), pltpu.VMEM((hs, T, LANES), F32),
                        pltpu.VMEM((hs, T, LANES), F32)],
        compiler_params=_cparams(("parallel", "parallel", "arbitrary")),
        name="mla_attention",
    )(q, k, v3)


def _pool_kernel(u_ref, w_ref, sc_ref, o_ref):
    S = u_ref.shape[1]
    row = lax.broadcasted_iota(jnp.int32, (S, LANES), 0)
    t1 = (row + 1).astype(F32)
    for gi, win in enumerate(POOL_WINDOWS):
        u = u_ref[0, :, gi * LANES:(gi + 1) * LANES]
        s = u
        k = 1
        while k < win:
            s = s + jnp.where(row >= k, pltpu.roll(s, k, 0), 0.0)
            k *= 2
        pooled = s / jnp.minimum(t1, float(win))
        mixed = _dot((pooled - u).astype(BF16), w_ref[gi].astype(BF16))
        o_ref[0, :, gi * LANES:(gi + 1) * LANES] = (
            mixed * sc_ref[:, gi * LANES:(gi + 1) * LANES]).astype(o_ref.dtype)


def _pool(pp3, w_pool, scale):
    B, S, W = pp3.shape
    return pl.pallas_call(
        _pool_kernel,
        grid=(B,),
        in_specs=[pl.BlockSpec((1, S, W), lambda b: (b, 0, 0)),
                  pl.BlockSpec(w_pool.shape, lambda b: (0, 0, 0)),
                  pl.BlockSpec((1, W), lambda b: (0, 0))],
        out_specs=pl.BlockSpec((1, S, W), lambda b: (b, 0, 0)),
        out_shape=jax.ShapeDtypeStruct((B, S, W), BF16),
        compiler_params=_cparams(("parallel",)),
        name="pool_mixer",
    )(pp3, w_pool, scale)


def _merge_router_kernel(x_ref, gt_ref, ya_ref, yb_ref, yc_ref, wa_ref, wb_ref, wc_ref, wo_ref, g1_ref,
                         g_ref, sh_ref, sc_ref, wr_ref, br_ref, xo_ref, hb_ref, ls_ref, wt_ref, cnt_ref):
    D = x_ref.shape[1]
    merged = None
    for j, (y_ref, w_ref) in enumerate(((ya_ref, wa_ref), (yb_ref, wb_ref), (yc_ref, wc_ref))):
        y = _dot(y_ref[...], w_ref[...])
        gate = _sigmoid(gt_ref[:, j * D:(j + 1) * D].astype(F32))
        merged = gate * y if merged is None else merged + gate * y
    x = x_ref[...] + g1_ref[0] * _dot(merged.astype(BF16), wo_ref[...])
    xo_ref[...] = x
    _route(x, g_ref, sh_ref, sc_ref, wr_ref, br_ref, hb_ref, ls_ref, wt_ref, cnt_ref)


def _route(x, g_ref, sh_ref, sc_ref, wr_ref, br_ref, hb_ref, ls_ref, wt_ref, cnt_ref):
    tr = x.shape[0]

    ms = jnp.mean(x * x, axis=-1, keepdims=True)
    h = x * lax.rsqrt(ms + NORM_EPS) * g_ref[...]
    h = h * (1.0 + sc_ref[0]) + sh_ref[0]
    hb_ref[...] = h.astype(hb_ref.dtype)
    h1, h2, _ = _split3(h)
    w1, w2, _ = _split3(wr_ref[...])
    logits = _dot(h1, w1) + (_dot(h1, w2) + _dot(h2, w1)) + br_ref[...]
    lane = lax.broadcasted_iota(jnp.int32, logits.shape, 1)
    work = logits
    vals, ids = [], []
    for _k in range(TOP_K):
        m = jnp.max(work, axis=-1, keepdims=True)
        sel = jnp.min(jnp.where(work == m, lane, LANES), axis=-1, keepdims=True)
        vals.append(m)
        ids.append(sel)
        work = jnp.where(lane == sel, -jnp.inf, work)
    es = [jnp.exp(v - vals[0]) for v in vals]
    den = es[0] + es[1] + es[2] + es[3]
    onehot = jnp.zeros(logits.shape, F32)
    wt_out = jnp.zeros(logits.shape, F32)
    for k in range(TOP_K):
        onehot = onehot + (lane == ids[k]).astype(F32)
        wt_out = jnp.where(lane == k, es[k] / den, wt_out)
    wt_ref[...] = wt_out

    r = lax.broadcasted_iota(jnp.int32, (tr, tr), 0)
    c = lax.broadcasted_iota(jnp.int32, (tr, tr), 1)
    strict = (c < r).astype(BF16)
    before = _dot(strict, onehot.astype(BF16))
    cnt = jnp.sum(onehot, axis=0, keepdims=True)
    cnt_ref[0] = jnp.broadcast_to(cnt, cnt_ref.shape[1:])
    units = jnp.floor((cnt + (SEG_ALIGN - 1.0)) * (1.0 / SEG_ALIGN))
    er = lax.broadcasted_iota(jnp.int32, (LANES, LANES), 0)
    ec = lax.broadcasted_iota(jnp.int32, (LANES, LANES), 1)
    upper = (er < ec).astype(BF16)
    seg_start = _dot(jnp.broadcast_to(units, (SUBLANES, LANES)).astype(BF16), upper)[0:1] * float(SEG_ALIGN)
    pos = before + seg_start
    ls_out = jnp.full(logits.shape, -1.0, F32)
    for k in range(TOP_K):
        lk = jnp.sum(jnp.where(lane == ids[k], pos, 0.0), axis=-1, keepdims=True)
        ls_out = jnp.where(lane == k, lk, ls_out)
    ls_ref[...] = ls_out.astype(jnp.int32)


def _local_rows(tb):
    return TOP_K * tb + N_EXPERTS * SEG_ALIGN


def _sort_chunk(rl):
    return SORT_CHUNK if rl % SORT_CHUNK == 0 else SORT_CHUNK // 2


def _merge_router(x2, gates, ya, yb, yc, wa, wb, wc, wo, g, mod3, l, wr_p, br_p, B, S, tr):
    N, D = x2.shape
    nb = S // tr
    base = l * B * 6
    full = lambda a: pl.BlockSpec(a.shape, lambda i: (0,) * a.ndim)
    rows = lambda w: pl.BlockSpec((tr, w), lambda i: (i, 0))

    def modspec(j):
        return pl.BlockSpec((1, 1, D), lambda i: (base + (i // nb) * 6 + j, 0, 0))

    return pl.pallas_call(
        _merge_router_kernel,
        grid=(N // tr,),
        in_specs=[rows(D), rows(gates.shape[1]), rows(ya.shape[1]), rows(yb.shape[1]), rows(yc.shape[1]),
                  full(wa), full(wb), full(wc), full(wo), modspec(2),
                  full(g), modspec(3), modspec(4), full(wr_p), full(br_p)],
        out_specs=(rows(D), rows(D), rows(LANES), rows(LANES),
                   pl.BlockSpec((1, SUBLANES, LANES), lambda i: (i, 0, 0))),
        out_shape=(jax.ShapeDtypeStruct((N, D), F32),
                   jax.ShapeDtypeStruct((N, D), BF16),
                   jax.ShapeDtypeStruct((N, LANES), jnp.int32),
                   jax.ShapeDtypeStruct((N, LANES), F32),
                   jax.ShapeDtypeStruct((N // tr, SUBLANES, LANES), F32)),
        compiler_params=_cparams(("parallel",)),
        name="merge_router",
    )(x2, gates, ya, yb, yc, wa, wb, wc, wo, mod3, g, mod3, mod3, wr_p, br_p)


def _segment_copies(j, e, src_ref, dst_ref, len_ref, local_ref, global_ref, sem, to_global, n_bits):
    base = j * N_EXPERTS + e
    units = len_ref[base]
    s0 = src_ref[base]
    d0 = dst_ref[base]
    copies = []
    for bit in range(n_bits):
        rows = SEG_ALIGN << bit
        done = ((units >> (bit + 1)) << (bit + 1)) * SEG_ALIGN
        loc = local_ref.at[pl.ds(pl.multiple_of(s0 + done, SEG_ALIGN), rows)]
        glo = global_ref.at[pl.ds(pl.multiple_of(d0 + done, SEG_ALIGN), rows)]
        cp = pltpu.make_async_copy(loc, glo, sem) if to_global else pltpu.make_async_copy(glo, loc, sem)
        copies.append((((units >> bit) & 1) == 1, cp))
    return copies


def _each_segment_copy(j, src_ref, dst_ref, len_ref, local_ref, global_ref, sem, to_global, n_bits, action):
    def per_expert(e, c):
        for pred, cp in _segment_copies(j, e, src_ref, dst_ref, len_ref, local_ref, global_ref,
                                        sem, to_global, n_bits):
            @pl.when(pred)
            def _():
                action(cp)
        return c
    lax.fori_loop(0, N_EXPERTS, per_expert, 0)


def _sort_place_kernel(n_bits, src_ref, dst_ref, len_ref, zflag_ref, hb_ref, ls_ref, xs_hbm,
                       zbuf, xl_buf, zsem, sem):
    j = pl.program_id(0)
    last = pl.num_programs(0) - 1
    tr = hb_ref.shape[0]
    tm = zbuf.shape[0]
    rl = xl_buf.shape[1]
    n_tiles = xs_hbm.shape[0] // tm
    slot = j % 2

    def segments(block, buf_slot, action):
        _each_segment_copy(block, src_ref, dst_ref, len_ref, xl_buf.at[buf_slot], xs_hbm,
                           sem.at[buf_slot], True, n_bits, action)

    def zero_fill(action):
        def per_tile(t, c):
            @pl.when(zflag_ref[t] > 0)
            def _():
                action(pltpu.make_async_copy(zbuf, xs_hbm.at[pl.ds(pl.multiple_of(t * tm, tm), tm)], zsem))
            return c
        lax.fori_loop(0, n_tiles, per_tile, 0)

    @pl.when(j == 0)
    def _():
        zbuf[...] = jnp.zeros_like(zbuf)
        zero_fill(lambda cp: cp.start())

    @pl.when(j >= 2)
    def _():
        segments(j - 2, slot, lambda cp: cp.wait())

    ls_t = jnp.transpose(ls_ref[...].astype(F32))
    targets = [ls_t[k:k + 1, :].astype(jnp.int32) for k in range(TOP_K)]
    hb = hb_ref[...]
    ch = _sort_chunk(rl)
    for cidx in range(rl // ch):
        ri = lax.broadcasted_iota(jnp.int32, (ch, tr), 0) + cidx * ch
        hit = ri == targets[0]
        for k in range(1, TOP_K):
            hit = jnp.logical_or(hit, ri == targets[k])
        perm = jnp.where(hit, 1.0, 0.0).astype(BF16)
        xl_buf[slot, cidx * ch:(cidx + 1) * ch, :] = _dot(perm, hb).astype(xl_buf.dtype)

    @pl.when(j == 0)
    def _():
        zero_fill(lambda cp: cp.wait())

    segments(j, slot, lambda cp: cp.start())

    @pl.when(j == last)
    def _():
        @pl.when(j >= 1)
        def _():
            segments(j - 1, 1 - slot, lambda cp: cp.wait())
        segments(j, slot, lambda cp: cp.wait())


def _sort_place(seg_src, seg_dst, seg_len, zero_flags, hb, ls, n_rows, tr, tm, n_bits):
    N, D = hb.shape
    rl = _local_rows(tr)
    grid_spec = pltpu.PrefetchScalarGridSpec(
        num_scalar_prefetch=4,
        grid=(N // tr,),
        in_specs=[pl.BlockSpec((tr, D), lambda j, a, b, c, d: (j, 0)),
                  pl.BlockSpec((tr, LANES), lambda j, a, b, c, d: (j, 0))],
        out_specs=pl.BlockSpec(memory_space=pl.ANY),
        scratch_shapes=[pltpu.VMEM((tm, D), hb.dtype), pltpu.VMEM((2, rl, D), hb.dtype),
                        pltpu.SemaphoreType.DMA(()), pltpu.SemaphoreType.DMA((2,))],
    )
    return pl.pallas_call(
        functools.partial(_sort_place_kernel, n_bits),
        grid_spec=grid_spec,
        out_shape=jax.ShapeDtypeStruct((n_rows, D), hb.dtype),
        compiler_params=_cparams(("arbitrary",)),
        name="moe_sort_place",
    )(seg_src, seg_dst, seg_len, zero_flags, hb, ls)


def _expert_kernel(first_expert, te_ref, nxt_ref, rows_ref, nu_ref, x_ref, w1_hbm, b1_ref, w2_hbm, b2_ref,
                   o_ref, w1f, w2f, w1b, w2b, sem):
    t = pl.program_id(0)
    tm = x_ref.shape[0]
    F = w2b.shape[0]
    rows = rows_ref[t]

    def weight_copies(e):
        return (pltpu.make_async_copy(w1_hbm.at[first_expert + e], w1f, sem.at[0]),
                pltpu.make_async_copy(w2_hbm.at[first_expert + e], w2f, sem.at[1]))

    @pl.when(t == 0)
    def _():
        for cp in weight_copies(te_ref[0]):
            cp.start()

    @pl.when(jnp.logical_and(rows > 0, jnp.logical_or(t == 0, te_ref[t] != te_ref[jnp.maximum(t - 1, 0)])))
    def _():
        for cp in weight_copies(te_ref[t]):
            cp.wait()
        w1b[...] = w1f[...].astype(BF16)
        w2b[...] = w2f[...].astype(BF16)

        @pl.when(nxt_ref[t] >= 0)
        def _():
            for cp in weight_copies(nxt_ref[t]):
                cp.start()

    def mlp(m):
        a = _dot(x_ref[:m, :], w1b[...]) + b1_ref[0]
        glu = jnp.minimum(a[:, :F], SWIGLU_LIMIT)
        lin = jnp.clip(a[:, F:], -SWIGLU_LIMIT, SWIGLU_LIMIT)
        act = glu * _sigmoid(SWIGLU_ALPHA * glu) * (lin + 1.0)
        o_ref[:m, :] = (_dot(act.astype(BF16), w2b[...]) + b2_ref[0]).astype(o_ref.dtype)
        if m < tm:
            o_ref[m:, :] = jnp.zeros((tm - m, o_ref.shape[1]), o_ref.dtype)

    quarter = tm // EXPERT_ROW_STEPS
    for q in range(1, EXPERT_ROW_STEPS + 1):
        @pl.when(jnp.logical_and(rows > (q - 1) * quarter, rows <= q * quarter))
        def _():
            mlp(q * quarter)

    @pl.when(rows == 0)
    def _():
        o_ref[...] = jnp.zeros_like(o_ref)


def _experts(tile_expert, next_expert, tile_rows, n_used, xs, w1_all, b1_all, w2_all, b2_all, l, tm):
    n_tiles = tile_expert.shape[0]
    D = xs.shape[1]
    L, E, _, F2 = w1_all.shape
    F = w2_all.shape[2]
    w1 = w1_all.reshape(L * E, D, F2)
    w2 = w2_all.reshape(L * E, F, D)
    b1 = b1_all.reshape(L * E, F2)
    b2 = b2_all.reshape(L * E, D)
    E0 = l * E
    E = L * E
    grid_spec = pltpu.PrefetchScalarGridSpec(
        num_scalar_prefetch=4,
        grid=(n_tiles,),
        in_specs=[pl.BlockSpec((tm, D), lambda t, te, nx, tr, nu: (jnp.minimum(t, nu[0] - 1), 0)),
                  pl.BlockSpec(memory_space=pl.ANY),
                  pl.BlockSpec((1, 1, F2), lambda t, te, nx, tr, nu: (E0 + te[t], 0, 0)),
                  pl.BlockSpec(memory_space=pl.ANY),
                  pl.BlockSpec((1, 1, D), lambda t, te, nx, tr, nu: (E0 + te[t], 0, 0))],
        out_specs=pl.BlockSpec((tm, D), lambda t, te, nx, tr, nu: (t, 0)),
        scratch_shapes=[pltpu.VMEM((D, F2), F32), pltpu.VMEM((F, D), F32),
                        pltpu.VMEM((D, F2), BF16), pltpu.VMEM((F, D), BF16),
                        pltpu.SemaphoreType.DMA((2,))],
    )
    return pl.pallas_call(
        functools.partial(_expert_kernel, E0),
        grid_spec=grid_spec,
        out_shape=jax.ShapeDtypeStruct((n_tiles * tm, D), BF16),
        compiler_params=_cparams(("arbitrary",)),
        name="expert_mlp",
    )(tile_expert, next_expert, tile_rows, n_used, xs, w1, b1.reshape(E, 1, F2), w2, b2.reshape(E, 1, D))


def _combine_kernel(n_bits, src_ref, dst_ref, len_ref, ys_hbm, x_ref, ls_ref, wt_ref, g2_ref, o_ref,
                    ybuf, sem):
    j = pl.program_id(0)
    tn = x_ref.shape[0]
    rl = ybuf.shape[1]
    slot = j % 2

    def fetch(block, buf_slot, action):
        _each_segment_copy(block, src_ref, dst_ref, len_ref, ybuf.at[buf_slot], ys_hbm,
                           sem.at[buf_slot], False, n_bits, action)

    def start_fetch(block, buf_slot):
        ybuf[buf_slot] = jnp.zeros((rl, ybuf.shape[2]), ybuf.dtype)
        fetch(block, buf_slot, lambda cp: cp.start())

    @pl.when(j == 0)
    def _():
        start_fetch(j, slot)

    @pl.when(j + 1 < pl.num_programs(0))
    def _():
        start_fetch(j + 1, 1 - slot)

    fetch(j, slot, lambda cp: cp.wait())
    ls = ls_ref[...]
    wt = wt_ref[...]
    ch = _sort_chunk(rl)
    y = jnp.zeros(x_ref.shape, F32)
    for cidx in range(rl // ch):
        li = lax.broadcasted_iota(jnp.int32, (tn, ch), 1) + cidx * ch
        pw = jnp.zeros((tn, ch), F32)
        for k in range(TOP_K):
            pw = jnp.where(li == ls[:, k:k + 1], wt[:, k:k + 1], pw)
        y = y + _dot(pw.astype(BF16), ybuf[slot, cidx * ch:(cidx + 1) * ch, :])
    o_ref[...] = x_ref[...] + g2_ref[0] * y


def _combine(seg_src, seg_dst, seg_len, ys, x2, ls, wts, mod3, l, B, S, tn, n_bits):
    N, D = x2.shape
    nb = S // tn
    base = l * B * 6
    rl = _local_rows(tn)
    grid_spec = pltpu.PrefetchScalarGridSpec(
        num_scalar_prefetch=3,
        grid=(N // tn,),
        in_specs=[pl.BlockSpec(memory_space=pl.ANY),
                  pl.BlockSpec((tn, D), lambda i, a, b, c: (i, 0)),
                  pl.BlockSpec((tn, LANES), lambda i, a, b, c: (i, 0)),
                  pl.BlockSpec((tn, LANES), lambda i, a, b, c: (i, 0)),
                  pl.BlockSpec((1, 1, D), lambda i, a, b, c: (base + (i // nb) * 6 + 5, 0, 0))],
        out_specs=pl.BlockSpec((tn, D), lambda i, a, b, c: (i, 0)),
        scratch_shapes=[pltpu.VMEM((2, rl, D), ys.dtype), pltpu.SemaphoreType.DMA((2,))],
    )
    return pl.pallas_call(
        functools.partial(_combine_kernel, n_bits),
        grid_spec=grid_spec,
        out_shape=jax.ShapeDtypeStruct((N, D), F32),
        compiler_params=_cparams(("arbitrary",)),
        name="moe_combine",
    )(seg_src, seg_dst, seg_len, ys, x2, ls, wts, mod3)


def _tile_rows(S, want):
    t = min(S, want)
    while S % t:
        t //= 2
    return t


def _prep_in_weights(w_in_l):
    HW = HGRN_HEADS * HGRN_DIM
    o = 0
    wh = w_in_l[:, o:o + 4 * HW]; o += 4 * HW
    wql = w_in_l[:, o:o + MLA_Q_RANK]; o += MLA_Q_RANK
    wkvl = w_in_l[:, o:o + MLA_KV_RANK]; o += MLA_KV_RANK
    wkr = w_in_l[:, o:o + MLA_ROPE]; o += MLA_ROPE
    wp = w_in_l[:, o:o + HW]; o += HW
    wg = w_in_l[:, o:]
    D = w_in_l.shape[0]
    wkr_p = jnp.zeros((D, LANES), w_in_l.dtype).at[:, MLA_NOPE:MLA_NOPE + MLA_ROPE].set(wkr)
    wm = jnp.concatenate([wql, wkvl, wkr_p, _rope_partner(wkr_p)], axis=1)
    return wg.astype(BF16), wh.astype(BF16), wm.astype(BF16), wp.astype(BF16)


def _rope_partner(w):
    half = MLA_ROPE // 2
    w3 = w.reshape(w.shape[:-1] + (-1, LANES))
    first = w3[..., MLA_NOPE:MLA_NOPE + half]
    second = w3[..., MLA_NOPE + half:MLA_NOPE + MLA_ROPE]
    out = jnp.concatenate([jnp.zeros_like(w3[..., :MLA_NOPE]), second, first,
                           jnp.zeros_like(w3[..., MLA_NOPE + MLA_ROPE:])], axis=-1)
    return out.reshape(w.shape)


def _pad_heads(w, n_heads, width):
    K = w.shape[0]
    w3 = w.reshape(K, n_heads, width)
    return jnp.pad(w3, ((0, 0), (0, 0), (0, LANES - width))).reshape(K, n_heads * LANES)


def kernel(x, c, positions, ada_w, ada_b, norm1_g, norm2_g, w_in, hgrn_lb, hgrn_onorm_g, mla_qlat_g, mla_kvlat_g, w_uq, w_ukv, q_norm_g, k_norm_g, w_pool, pool_scale, w_br_a, w_br_b, w_br_c, w_out, w_router, b_router, w_exp1, b_exp1, w_exp2, b_exp2):
    B, S, D = x.shape
    L = ada_w.shape[0]
    N = B * S
    H = MLA_HEADS
    tr = _tile_rows(S, 512)
    tc = _tile_rows(S, 128)
    ta = _tile_rows(S, 512)
    tm = 512

    mod = _ada(c, ada_w, ada_b)
    mod3 = mod.reshape(L * B * 6, 1, D)

    inv_freq = 1.0 / (ROPE_THETA ** (jnp.arange(0, MLA_ROPE, 2, dtype=F32) / MLA_ROPE))
    freq_row = jnp.zeros((1, LANES), F32).at[0, MLA_NOPE:MLA_NOPE + MLA_ROPE].set(
        jnp.concatenate([inv_freq, inv_freq]))
    posf = positions.astype(F32).reshape(N, 1)
    cos_t, sin_t = _rope_tables(posf, freq_row, tr)

    max_rows = N * TOP_K + (N // tr) * N_EXPERTS * (SEG_ALIGN - 1)
    n_tiles = -(-max_rows // tm) + N_EXPERTS
    n_bits = (tr // SEG_ALIGN).bit_length()
    x2 = x.reshape(N, D)
    for l in range(L):
        wg, wh, wm, wp = _prep_in_weights(w_in[l])
        pg, ph, pm, pp = _inproj(x2, norm1_g[l].reshape(1, D), mod3, l, wg, wh, wm, wp, B, S, tr)

        ya = _hgrn(ph.reshape(B, S, -1), hgrn_lb, hgrn_onorm_g[l].reshape(1, HGRN_DIM), l, B, S, tc)

        wq_f = _pad_heads(w_uq[l], H, MLA_QK)
        wq_p = wq_f.astype(BF16)
        wqr_p = _rope_partner(wq_f).astype(BF16)
        wkv3 = w_ukv[l].reshape(MLA_KV_RANK, H, MLA_NOPE + MLA_V)
        wkv_p = jnp.concatenate(
            [_pad_heads(wkv3[:, :, :MLA_NOPE].reshape(MLA_KV_RANK, H * MLA_NOPE), H, MLA_NOPE),
             wkv3[:, :, MLA_NOPE:].reshape(MLA_KV_RANK, H * MLA_V)], axis=1).astype(BF16)
        qn_p = jnp.pad(q_norm_g[l], (0, LANES - MLA_QK)).reshape(1, LANES)
        kn_p = jnp.pad(k_norm_g[l], (0, LANES - MLA_QK)).reshape(1, LANES)
        q, k, v = _mla_prep(pm, cos_t, sin_t, mla_qlat_g[l].reshape(1, -1), mla_kvlat_g[l].reshape(1, -1),
                            wq_p, wqr_p, wkv_p, qn_p, _rope_partner(qn_p), kn_p, _rope_partner(kn_p),
                            B, S, tr)
        yb = _attention(q, k, v.reshape(B, S, H * MLA_V), B, S, ta)

        yc = _pool(pp.reshape(B, S, -1), w_pool[l], pool_scale[l].reshape(1, -1))

        wr_p = jnp.pad(w_router[l], ((0, 0), (0, LANES - N_EXPERTS)))
        br_p = jnp.concatenate([b_router[l], jnp.full((LANES - N_EXPERTS,), NEG_BIG, F32)]).reshape(1, LANES)
        x2, hb, ls, wts, cnt = _merge_router(
            x2, pg, ya.reshape(N, -1), yb.reshape(N, -1), yc.reshape(N, -1),
            w_br_a[l].astype(BF16), w_br_b[l].astype(BF16), w_br_c[l].astype(BF16), w_out[l].astype(BF16),
            norm2_g[l].reshape(1, D), mod3, l, wr_p, br_p, B, S, tr)

        seg = ((cnt[:, 0, :N_EXPERTS].astype(jnp.int32) + SEG_ALIGN - 1) // SEG_ALIGN) * SEG_ALIGN
        seg_src = jnp.cumsum(seg, axis=1) - seg
        group = jnp.sum(seg, axis=0)
        tiles_per = (group + tm - 1) // tm
        tile_end = jnp.cumsum(tiles_per)
        offsets = (tile_end - tiles_per) * tm
        seg_dst = offsets[None, :] + jnp.cumsum(seg, axis=0) - seg
        n_used = tile_end[-1:]
        tile_ids = jnp.minimum(jnp.arange(n_tiles, dtype=jnp.int32), n_used[0] - 1)
        tile_expert = jnp.minimum(
            jnp.sum((tile_end[None, :] <= tile_ids[:, None]).astype(jnp.int32), axis=1), N_EXPERTS - 1)
        all_tiles = jnp.arange(n_tiles, dtype=jnp.int32)
        is_tail = jnp.any((tile_end[None, :] - 1 == all_tiles[:, None]) & (tiles_per[None, :] > 0), axis=1)
        zero_flags = (is_tail | (all_tiles >= n_used[0])).astype(jnp.int32)
        seg_src = seg_src.reshape(-1).astype(jnp.int32)
        seg_dst = seg_dst.reshape(-1).astype(jnp.int32)
        seg_len = (seg // SEG_ALIGN).reshape(-1).astype(jnp.int32)

        mine = tile_expert[:, None] == jnp.arange(N_EXPERTS, dtype=jnp.int32)[None, :]
        first_tile = jnp.sum(jnp.where(mine, (tile_end - tiles_per)[None, :], 0), axis=1)
        group_rows = jnp.sum(jnp.where(mine, group[None, :], 0), axis=1)
        tile_rows = jnp.where(all_tiles < n_used[0],
                              jnp.clip(group_rows - (all_tiles - first_tile) * tm, 0, tm), 0)

        xs = _sort_place(seg_src, seg_dst, seg_len, zero_flags, hb, ls, n_tiles * tm, tr, tm, n_bits)
        group_end = jnp.sum(jnp.where(mine, tile_end[None, :], 0), axis=1)
        following = jnp.minimum(
            jnp.sum((tile_end[None, :] <= group_end[:, None]).astype(jnp.int32), axis=1), N_EXPERTS - 1)
        next_expert = jnp.where(group_end < n_used[0], following, -1)

        ys = _experts(tile_expert.astype(jnp.int32), next_expert.astype(jnp.int32), tile_rows.astype(jnp.int32),
                      n_used.astype(jnp.int32), xs, w_exp1, b_exp1, w_exp2, b_exp2, l, tm)
        x2 = _combine(seg_src, seg_dst, seg_len, ys, x2, ls, wts, mod3, l, B, S, tr, n_bits)
    return x2.reshape(B, S, D)
```
